```python
import math
import jax, jax.numpy as jnp
from jax import lax
import numpy as np

D_MODEL = 1024
BATCH = 2
SEQ = 16384
DEPTH = 1
DEC_BATCH = 8
DEC_SEQ = 64
PAST_LEN = 1024

CHUNK = 64
Q_BLOCK = 128
N_MEM = 256
EPS = 1e-6

A_HEADS = 4
A_DH = 64
A_WIDTH = A_HEADS * 2 * A_DH
B_HEADS = 8
B_DH = 64
B_WIDTH = B_HEADS * B_DH
IDX_HEADS = 8
IDX_DH = 64
TOPK_MAX = 256
M_HEADS = 4
M_DH = 128
M_WIDTH = M_HEADS * M_DH
N_BRANCH = 3

IN_SPLITS = (A_WIDTH, A_WIDTH, A_WIDTH, A_WIDTH,
             B_WIDTH, B_WIDTH, B_WIDTH, B_WIDTH,
             IDX_HEADS * IDX_DH, IDX_DH, IDX_HEADS,
             M_WIDTH, M_WIDTH,
             N_BRANCH * D_MODEL)
IN_WIDTH = sum(IN_SPLITS)

kernel_name = 'hybrid_diffattn_dsa_memory_stream'


def rmsnorm(x, g):
    xf = x.astype(jnp.float32)
    y = xf * lax.rsqrt(jnp.mean(xf * xf, axis=-1, keepdims=True) + EPS)
    return (y * g.astype(jnp.float32)).astype(x.dtype)


def alibi_slopes(n):
    return jnp.power(2.0, -8.0 * jnp.arange(1, n + 1, dtype=jnp.float32) / n)


def chunk_visible(q_pos, k_pos):
    return (k_pos[None, :] // CHUNK) <= (q_pos[:, None] // CHUNK)


def split_cols(z):
    outs = []
    off = 0
    for w in IN_SPLITS:
        outs.append(z[..., off:off + w])
        off += w
    return outs


def diff_attend(q, k, v, q_pos, k_pos, lam, lam_init, subln_g):
    s = jnp.einsum('bqhmd,bkhmd->bhmqk', q, k).astype(jnp.float32) * (A_DH ** -0.5)
    dist = jnp.abs(q_pos[:, None] - k_pos[None, :]).astype(jnp.float32)
    s = s - alibi_slopes(A_HEADS)[:, None, None, None] * dist
    s = jnp.where(chunk_visible(q_pos, k_pos), s, -jnp.inf)
    p = jax.nn.softmax(s, axis=-1)
    a = p[:, :, 0] - lam * p[:, :, 1]
    o = jnp.einsum('bhqk,bkhe->bqhe', a.astype(v.dtype), v)
    o = rmsnorm(o, subln_g) * (1.0 - lam_init)
    return o.reshape(o.shape[0], o.shape[1], A_WIDTH)


def dsa_attend(q, k, v, qi, w_idx, kidx, q_pos, k_pos, topk):
    logits = jnp.einsum('bqhd,bkd->bqhk', qi, kidx).astype(jnp.float32) * (IDX_DH ** -0.5)
    score = jnp.einsum('bqhk,bqh->bqk', jax.nn.relu(logits), w_idx.astype(jnp.float32))
    score = jnp.where(chunk_visible(q_pos, k_pos), score, -jnp.inf)
    _, idx = lax.top_k(score, topk)
    sel_pos = k_pos[idx]
    valid = (sel_pos // CHUNK) <= (q_pos[None, :, None] // CHUNK)
    gather = jax.vmap(lambda rows, ids: rows[ids])
    kg = gather(k, idx)
    vg = gather(v, idx)
    s = jnp.einsum('bqhd,bqkhd->bhqk', q, kg).astype(jnp.float32) * (B_DH ** -0.5)
    dist = jnp.abs(q_pos[None, :, None] - sel_pos).astype(jnp.float32)
    s = s - alibi_slopes(B_HEADS)[None, :, None, None] * dist[:, None]
    s = jnp.where(valid[:, None], s, -jnp.inf)
    p = jax.nn.softmax(s, axis=-1)
    o = jnp.einsum('bhqk,bqkhd->bqhd', p.astype(v.dtype), vg)
    return o.reshape(o.shape[0], o.shape[1], B_WIDTH)


def mem_attend(q, mk, mv):
    s = jnp.einsum('bqhd,bmhd->bhqm', q, mk).astype(jnp.float32) * (M_DH ** -0.5)
    p = jax.nn.softmax(s, axis=-1)
    o = jnp.einsum('bhqm,bmhd->bqhd', p.astype(mv.dtype), mv)
    return o.reshape(o.shape[0], o.shape[1], M_WIDTH)


def memory_kv(mem, mem_norm_g, w_mem_kv):
    kv = jnp.einsum('bmd,de->bme', rmsnorm(mem, mem_norm_g), w_mem_kv)
    bsz = mem.shape[0]
    mk = kv[..., :M_WIDTH].reshape(bsz, N_MEM, M_HEADS, M_DH)
    mv = kv[..., M_WIDTH:].reshape(bsz, N_MEM, M_HEADS, M_DH)
    return mk, mv


def layer_step(x, past, mk, mv, q_pos0, blocked, norm_g, w_in, b_gate, lam, lam_init,
               subln_g, w_br_a, w_br_b, w_br_m, w_out):
    bsz, n_q, _ = x.shape
    hn = rmsnorm(x, norm_g)
    z = jnp.einsum('btd,de->bte', hn, w_in)
    (qa, ka, va, ga, qb, kb, vb, gb, qi, kidx, widx, qm, gm, gates) = split_cols(z)
    qa = qa.reshape(bsz, n_q, A_HEADS, 2, A_DH)
    ka = ka.reshape(bsz, n_q, A_HEADS, 2, A_DH)
    va = va.reshape(bsz, n_q, A_HEADS, 2 * A_DH)
    qb = qb.reshape(bsz, n_q, B_HEADS, B_DH)
    kb = kb.reshape(bsz, n_q, B_HEADS, B_DH)
    vb = vb.reshape(bsz, n_q, B_HEADS, B_DH)
    qi = qi.reshape(bsz, n_q, IDX_HEADS, IDX_DH)
    qm = qm.reshape(bsz, n_q, M_HEADS, M_DH)
    new_rows = (ka, va, kb, vb, kidx)
    if past is None:
        full = new_rows
    else:
        full = tuple(jnp.concatenate([pr, nr], axis=1) for pr, nr in zip(past, new_rows))
    fa_k, fa_v, fb_k, fb_v, f_kidx = full
    n_k = fa_k.shape[1]
    k_pos = jnp.arange(n_k, dtype=jnp.int32)
    topk = min(TOPK_MAX, n_k // 4)

    def block(rel, size):
        sl = lambda t: lax.dynamic_slice_in_dim(t, rel, size, axis=1)
        q_pos = q_pos0 + rel + jnp.arange(size, dtype=jnp.int32)
        oa = diff_attend(sl(qa), fa_k, fa_v, q_pos, k_pos, lam, lam_init, subln_g)
        ob = dsa_attend(sl(qb), fb_k, fb_v, sl(qi), sl(widx), f_kidx, q_pos, k_pos, topk)
        return jnp.concatenate([oa, ob], axis=-1)

    if blocked:
        nb = n_q // Q_BLOCK
        o = lax.map(lambda i: block(i * Q_BLOCK, Q_BLOCK), jnp.arange(nb, dtype=jnp.int32))
        o = jnp.moveaxis(o, 0, 1).reshape(bsz, n_q, A_WIDTH + B_WIDTH)
    else:
        o = block(0, n_q)
    oa, ob = o[..., :A_WIDTH], o[..., A_WIDTH:]
    om = mem_attend(qm, mk, mv)
    pa = jnp.einsum('bte,ed->btd', jax.nn.silu(ga) * oa, w_br_a)
    pb = jnp.einsum('bte,ed->btd', jax.nn.silu(gb) * ob, w_br_b)
    pm = jnp.einsum('bte,ed->btd', jax.nn.silu(gm) * om, w_br_m)
    g = jax.nn.sigmoid(gates + b_gate).reshape(bsz, n_q, N_BRANCH, D_MODEL)
    merged = g[..., 0, :] * pa + g[..., 1, :] * pb + g[..., 2, :] * pm
    return x + jnp.einsum('btd,de->bte', merged, w_out), new_rows


def setup_inputs(seed: int = 0) -> dict:
    key = jax.random.key(seed)
    ks = jax.random.split(key, 32)
    nrm = lambda k, shape, scale=1.0: scale * jax.random.normal(k, shape, jnp.float32)
    gain = lambda k, shape: 1.0 + 0.02 * jax.random.normal(k, shape, jnp.float32)
    return {
        'x_prompt': nrm(ks[0], (BATCH, SEQ, D_MODEL)),
        'x_sample': nrm(ks[1], (DEC_BATCH, DEC_SEQ, D_MODEL)),
        'mem_prompt': nrm(ks[2], (BATCH, N_MEM, D_MODEL)),
        'cache_a_k': nrm(ks[3], (DEPTH, DEC_BATCH, PAST_LEN, A_HEADS, 2, A_DH)),
        'cache_a_v': nrm(ks[4], (DEPTH, DEC_BATCH, PAST_LEN, A_HEADS, 2 * A_DH)),
        'cache_b_k': nrm(ks[5], (DEPTH, DEC_BATCH, PAST_LEN, B_HEADS, B_DH)),
        'cache_b_v': nrm(ks[6], (DEPTH, DEC_BATCH, PAST_LEN, B_HEADS, B_DH)),
        'cache_b_kidx': nrm(ks[7], (DEPTH, DEC_BATCH, PAST_LEN, IDX_DH)),
        'cache_mem_k': nrm(ks[8], (DEPTH, DEC_BATCH, N_MEM, M_HEADS, M_DH)),
        'cache_mem_v': nrm(ks[9], (DEPTH, DEC_BATCH, N_MEM, M_HEADS, M_DH)),
        'norm_g': gain(ks[10], (DEPTH, D_MODEL)),
        'w_in': nrm(ks[11], (DEPTH, D_MODEL, IN_WIDTH), D_MODEL ** -0.5),
        'b_gate': nrm(ks[12], (DEPTH, N_BRANCH * D_MODEL), 0.02),
        'lam_q1': nrm(ks[13], (DEPTH, A_DH), 0.1),
        'lam_k1': nrm(ks[14], (DEPTH, A_DH), 0.1),
        'lam_q2': nrm(ks[15], (DEPTH, A_DH), 0.1),
        'lam_k2': nrm(ks[16], (DEPTH, A_DH), 0.1),
        'subln_g': gain(ks[17], (DEPTH, 2 * A_DH)),
        'mem_norm_g': gain(ks[18], (DEPTH, D_MODEL)),
        'w_mem_kv': nrm(ks[19], (DEPTH, D_MODEL, 2 * M_WIDTH), D_MODEL ** -0.5),
        'w_br_a': nrm(ks[20], (DEPTH, A_WIDTH, D_MODEL), A_WIDTH ** -0.5),
        'w_br_b': nrm(ks[21], (DEPTH, B_WIDTH, D_MODEL), B_WIDTH ** -0.5),
        'w_br_m': nrm(ks[22], (DEPTH, M_WIDTH, D_MODEL), M_WIDTH ** -0.5),
        'w_out': nrm(ks[23], (DEPTH, D_MODEL, D_MODEL), D_MODEL ** -0.5),
        'final_g': gain(ks[24], (D_MODEL,)),
    }


def reference(x_prompt, x_sample, mem_prompt, cache_a_k, cache_a_v, cache_b_k, cache_b_v,
              cache_b_kidx, cache_mem_k, cache_mem_v, norm_g, w_in, b_gate, lam_q1, lam_k1,
              lam_q2, lam_k2, subln_g, mem_norm_g, w_mem_kv, w_br_a, w_br_b, w_br_m, w_out,
              final_g):
    h_p, h_s = x_prompt, x_sample
    p_rows, s_rows, mem_ks, mem_vs = [], [], [], []
    for l in range(DEPTH):
        lam_init = 0.8 - 0.6 * math.exp(-0.3 * l)
        lam = (jnp.exp(jnp.sum(lam_q1[l].astype(jnp.float32) * lam_k1[l].astype(jnp.float32)))
               - jnp.exp(jnp.sum(lam_q2[l].astype(jnp.float32) * lam_k2[l].astype(jnp.float32)))
               + lam_init)
        shared = (norm_g[l], w_in[l], b_gate[l], lam, lam_init, subln_g[l],
                  w_br_a[l], w_br_b[l], w_br_m[l], w_out[l])
        mk_p, mv_p = memory_kv(mem_prompt, mem_norm_g[l], w_mem_kv[l])
        h_p, rows_p = layer_step(h_p, None, mk_p, mv_p, 0, True, *shared)
        past = (cache_a_k[l], cache_a_v[l], cache_b_k[l], cache_b_v[l], cache_b_kidx[l])
        h_s, rows_s = layer_step(h_s, past, cache_mem_k[l], cache_mem_v[l], PAST_LEN, False,
                                 *shared)
        p_rows.append(rows_p)
        s_rows.append(rows_s)
        mem_ks.append(mk_p)
        mem_vs.append(mv_p)
    y_prompt = rmsnorm(h_p, final_g)
    y_sample = rmsnorm(h_s, final_g)
    stack = lambda rows, i: jnp.stack([r[i] for r in rows], axis=0)
    return (y_prompt, y_sample,
            stack(p_rows, 0), stack(p_rows, 1), stack(p_rows, 2), stack(p_rows, 3),
            stack(p_rows, 4), jnp.stack(mem_ks, axis=0), jnp.stack(mem_vs, axis=0),
            stack(s_rows, 0), stack(s_rows, 1), stack(s_rows, 2), stack(s_rows, 3),
            stack(s_rows, 4))
```

```python
import functools
import math

import jax
import jax.numpy as jnp
from jax import lax
from jax.experimental import pallas as pl
from jax.experimental.pallas import tpu as pltpu

F32 = jnp.float32
BF16 = jnp.bfloat16
I32 = jnp.int32

EPS = 1e-6
CHUNK = 64
CHUNK_SHIFT = 6
A_HEADS, A_DH = 4, 64
B_HEADS, B_DH = 8, 64
IDX_HEADS, IDX_DH = 8, 64
M_HEADS, M_DH = 4, 128
N_BRANCH = 3
TOPK_MAX = 256
WIDTH = 512
LANES = 128
LOG2E = 1.4426950408889634
NEG = -1e30
INT_MIN = -2147483648
INT_MAX = 2147483647
VMEM_LIMIT = 56 * 1024 * 1024

C_QA, C_KA, C_VA, C_GA = 0, 512, 1024, 1536
C_QB, C_KB, C_VB, C_GB = 2048, 2560, 3072, 3584
C_QI, C_KW, C_QM, C_GM, C_GATES = 4096, 4608, 4736, 5248, 5760
KW_PAD = LANES - IDX_DH - IDX_HEADS
W_COLS = C_GATES + N_BRANCH * 1024
IN_KW_END = 4680


def _resident(block_shape, index_map):
    return pl.BlockSpec(block_shape, index_map, pipeline_mode=pl.Buffered(1))


def _alibi_slope(h, n):
    return 2.0 ** (-8.0 * (h + 1) / n)


def _proj_kernel(x_ref, g_ref, w_ref, qa, ka16, va16, ga, qb, kb16, vb16, gb, qi, qm, gm,
                 gates, ka, va, kb, vb, kw):
    x = x_ref[...]
    hn = (x * lax.rsqrt(jnp.mean(x * x, axis=-1, keepdims=True) + EPS) * g_ref[...]).astype(BF16)

    def mm(c0, width):
        return jnp.dot(hn, w_ref[:, c0:c0 + width], preferred_element_type=F32)

    qa[...] = (mm(C_QA, WIDTH) * (A_DH ** -0.5 * LOG2E)).astype(BF16)
    qb[...] = (mm(C_QB, WIDTH) * (B_DH ** -0.5 * LOG2E)).astype(BF16)
    qi[...] = (mm(C_QI, WIDTH) * (IDX_DH ** -0.5)).astype(BF16)
    qm[...] = (mm(C_QM, WIDTH) * (M_DH ** -0.5 * LOG2E)).astype(BF16)
    for c0, o32, o16 in ((C_KA, ka, ka16), (C_VA, va, va16), (C_KB, kb, kb16), (C_VB, vb, vb16)):
        z = mm(c0, WIDTH)
        o32[...] = z
        o16[...] = z.astype(BF16)
    for c0, o16 in ((C_GA, ga), (C_GB, gb), (C_GM, gm)):
        o16[...] = mm(c0, WIDTH).astype(BF16)
    kw[...] = mm(C_KW, LANES)
    for c in range(N_BRANCH * 1024 // WIDTH):
        gates[:, c * WIDTH:(c + 1) * WIDTH] = mm(C_GATES + c * WIDTH, WIDTH).astype(BF16)


def _proj(x2d, norm_g, w16, tm):
    n, d = x2d.shape
    row = lambda i: (i, 0)
    fixed = lambda i: (0, 0)
    widths16 = [WIDTH] * 11 + [N_BRANCH * 1024]
    widths32 = [WIDTH] * 4 + [LANES]
    out_shape = ([jax.ShapeDtypeStruct((n, w), BF16) for w in widths16]
                 + [jax.ShapeDtypeStruct((n, w), F32) for w in widths32])
    out_specs = [pl.BlockSpec((tm, w), row) for w in widths16 + widths32]
    return pl.pallas_call(
        _proj_kernel,
        grid=(n // tm,),
        in_specs=[pl.BlockSpec((tm, d), row), pl.BlockSpec((1, d), fixed),
                  _resident((d, W_COLS), fixed)],
        out_specs=out_specs,
        out_shape=out_shape,
        compiler_params=pltpu.CompilerParams(dimension_semantics=("arbitrary",),
                                             vmem_limit_bytes=VMEM_LIMIT),
        name="proj",
    )(x2d, norm_g.reshape(1, d), w16)


def _memkv_kernel(x_ref, g_ref, w_ref, k32, v32, k16, v16):
    x = x_ref[...]
    hn = (x * lax.rsqrt(jnp.mean(x * x, axis=-1, keepdims=True) + EPS) * g_ref[...]).astype(BF16)
    zk = jnp.dot(hn, w_ref[:, :WIDTH], preferred_element_type=F32)
    zv = jnp.dot(hn, w_ref[:, WIDTH:], preferred_element_type=F32)
    k32[...] = zk
    v32[...] = zv
    k16[...] = zk.astype(BF16)
    v16[...] = zv.astype(BF16)


def _memkv(mem2d, g, w16, tm):
    n, d = mem2d.shape
    row = lambda i: (i, 0)
    fixed = lambda i: (0, 0)
    return pl.pallas_call(
        _memkv_kernel,
        grid=(n // tm,),
        in_specs=[pl.BlockSpec((tm, d), row), pl.BlockSpec((1, d), fixed),
                  pl.BlockSpec((d, 2 * WIDTH), fixed)],
        out_specs=[pl.BlockSpec((tm, WIDTH), row)] * 4,
        out_shape=[jax.ShapeDtypeStruct((n, WIDTH), F32)] * 2
        + [jax.ShapeDtypeStruct((n, WIDTH), BF16)] * 2,
        compiler_params=pltpu.CompilerParams(dimension_semantics=("arbitrary",)),
        name="memkv",
    )(mem2d, g.reshape(1, d), w16)


def _visible_chunks(q0, tq, tk):
    n_vis = (lax.shift_right_logical(q0 + tq - 1, CHUNK_SHIFT) + 1) * CHUNK
    return lax.div(n_vis + tk - 1, tk)


def _nt_dot(a, b):
    return lax.dot_general(a, b, (((1,), (1,)), ((), ())), preferred_element_type=F32)


def _softmax_step(s, v_aug, m_prev, acc):
    m_new = jnp.maximum(m_prev, jnp.max(s, axis=-1, keepdims=True))
    alpha = jnp.exp2(m_prev - m_new)
    p = jnp.exp2(s - m_new).astype(BF16)
    acc = alpha * acc + jnp.dot(p, v_aug, preferred_element_type=F32)
    return m_new, acc


def _attn_a_kernel(lq1, lk1, lq2, lk2, subg_ref, q_ref, k_ref, v_ref, o_ref, *, tq, tk,
                   q_pos0, lam_init):
    q0 = q_pos0 + pl.program_id(1) * tq
    n_chunks = _visible_chunks(q0, tq, tk)
    lam = (jnp.exp(jnp.sum(lq1[...] * lk1[...], axis=-1, keepdims=True))
           - jnp.exp(jnp.sum(lq2[...] * lk2[...], axis=-1, keepdims=True)) + lam_init)
    qpos = q0 + lax.broadcasted_iota(I32, (tq, 1), 0)
    qchunk = lax.shift_right_logical(qpos, CHUNK_SHIFT)
    qposf = qpos.astype(F32)
    lane = lax.broadcasted_iota(I32, (1, LANES), 1)
    ones_col = jnp.broadcast_to(jnp.where(lane == 0, 1.0, 0.0).astype(BF16), (tk, LANES))

    for h in range(A_HEADS):
        hs = slice(h * LANES, (h + 1) * LANES)
        qh = q_ref[0, :, hs]
        slope = _alibi_slope(h, A_HEADS) * LOG2E
        maps = []
        for mi in range(2):
            in_map = (lane < A_DH) if mi == 0 else (lane >= A_DH)
            qz = jnp.where(in_map, qh, jnp.zeros_like(qh))

            def step(c, carry, qz=qz, hs=hs, slope=slope):
                off = pl.multiple_of(c * tk, tk)
                kh = k_ref[0, pl.ds(off, tk), hs]
                vh = v_ref[0, pl.ds(off, tk), hs]
                kpos = off + lax.broadcasted_iota(I32, (1, tk), 1)
                s = _nt_dot(qz, kh) - slope * jnp.abs(qposf - kpos.astype(F32))
                vis = lax.shift_right_logical(kpos, CHUNK_SHIFT) <= qchunk
                s = jnp.where(vis, s, NEG)
                v_aug = jnp.concatenate([vh, ones_col], axis=1)
                return _softmax_step(s, v_aug, *carry)

            init = (jnp.full((tq, 1), NEG, F32), jnp.zeros((tq, 2 * LANES), F32))
            _, acc = lax.fori_loop(0, n_chunks, step, init)
            maps.append(acc[:, :LANES] / acc[:, LANES:LANES + 1])
        o = maps[0] - lam * maps[1]
        o = o * lax.rsqrt(jnp.mean(o * o, axis=-1, keepdims=True) + EPS) * subg_ref[...]
        o_ref[0, :, hs] = (o * (1.0 - lam_init)).astype(BF16)


def _attn_a(qa, ka16, va16, lam_params, subln_g, *, tq, tk, q_pos0, lam_init):
    b, lq, _ = qa.shape
    lkp = ka16.shape[1]
    small = lambda bi, i: (0, 0)
    kv_spec = _resident((1, lkp, WIDTH), lambda bi, i: (bi, 0, 0))
    return pl.pallas_call(
        functools.partial(_attn_a_kernel, tq=tq, tk=tk, q_pos0=q_pos0, lam_init=lam_init),
        grid=(b, lq // tq),
        in_specs=[pl.BlockSpec((1, A_DH), small)] * 4 + [pl.BlockSpec((1, 2 * A_DH), small),
                  pl.BlockSpec((1, tq, WIDTH), lambda bi, i: (bi, i, 0)), kv_spec, kv_spec],
        out_specs=pl.BlockSpec((1, tq, WIDTH), lambda bi, i: (bi, i, 0)),
        out_shape=jax.ShapeDtypeStruct((b, lq, WIDTH), BF16),
        compiler_params=pltpu.CompilerParams(dimension_semantics=("arbitrary", "arbitrary"),
                                             vmem_limit_bytes=VMEM_LIMIT),
        name="attn_a",
    )(*lam_params, subln_g.reshape(1, 2 * A_DH), qa, ka16, va16)


def _attn_b_kernel(qb_ref, qi_ref, kw_ref, kidx_t_ref, kb_ref, vb_ref, o_ref, key_sc, j_sc, *,
                   tq, tk, q_pos0, topk, idx_bits):
    q0 = q_pos0 + pl.program_id(1) * tq
    n_chunks = _visible_chunks(q0, tq, tk)
    qpos = q0 + lax.broadcasted_iota(I32, (tq, 1), 0)
    qchunk = lax.shift_right_logical(qpos, CHUNK_SHIFT)
    qposf = qpos.astype(F32)
    lane = lax.broadcasted_iota(I32, (1, LANES), 1)
    low_half = lane < B_DH
    topk_f = float(topk)

    def chunk_off(c):
        return pl.multiple_of(c * tk, tk)

    def key_pos(off):
        return off + lax.broadcasted_iota(I32, (1, tk), 1)

    def lane_sum(x):
        acc = x[:, :LANES]
        for j in range(1, tk // LANES):
            acc = acc + x[:, j * LANES:(j + 1) * LANES]
        return acc

    w_idx = kw_ref[0, :, IDX_DH:IDX_DH + IDX_HEADS]
    qi_z = []
    for h in range(IDX_HEADS):
        pair = qi_ref[0, :, (h // 2) * LANES:(h // 2 + 1) * LANES]
        mine = low_half if h % 2 == 0 else jnp.logical_not(low_half)
        qi_z.append(jnp.where(mine, pair, jnp.zeros_like(pair)))

    def score_chunk(c, _):
        off = chunk_off(c)
        k_t = kidx_t_ref[0, :, pl.ds(off, tk)]
        score = jnp.zeros((tq, tk), F32)
        for h in range(IDX_HEADS):
            logit = jnp.dot(qi_z[h], k_t, preferred_element_type=F32)
            score = score + jnp.maximum(logit, 0.0) * w_idx[:, h:h + 1]
        score = jnp.where(score == 0.0, 0.0, score)
        vis = lax.shift_right_logical(key_pos(off), CHUNK_SHIFT) <= qchunk
        score = jnp.where(vis, score, -jnp.inf)
        bits = pltpu.bitcast(score, I32)
        key_sc[:, pl.ds(off, tk)] = bits ^ (lax.shift_right_arithmetic(bits, 31) & INT_MAX)
        return 0

    lax.fori_loop(0, n_chunks, score_chunk, 0)

    def count(pred):
        def body(c, acc):
            off = chunk_off(c)
            hit = pred(key_sc[:, pl.ds(off, tk)], key_pos(off))
            return acc + lane_sum(jnp.where(hit, 1.0, 0.0))
        acc = lax.fori_loop(0, n_chunks, body, jnp.zeros((tq, LANES), F32))
        return jnp.sum(acc, axis=-1, keepdims=True)

    def descend_cond(st):
        b, _, cnt_t = st
        return jnp.logical_and(b < 32, jnp.max(jnp.abs(cnt_t - topk_f)) > 0.0)

    def descend(st):
        b, thr, cnt_t = st
        cand = thr + lax.shift_left(jnp.int32(1), 31 - b)
        cnt = count(lambda key, _: key >= cand)
        take = cnt >= topk_f
        return b + 1, jnp.where(take, cand, thr), jnp.where(take, cnt, cnt_t)

    cnt0 = jnp.broadcast_to((n_chunks * tk).astype(F32), (tq, 1))
    _, thr, cnt_t = lax.while_loop(
        descend_cond, descend, (jnp.int32(0), jnp.full((tq, 1), INT_MIN, I32), cnt0))

    j_sc[...] = jnp.full((tq, 1), INT_MAX, I32)

    @pl.when(jnp.max(jnp.abs(cnt_t - topk_f)) > 0.0)
    def _():
        room = topk_f - count(lambda key, _: key > thr)
        jmax = jnp.zeros((tq, 1), I32)
        for bit in range(idx_bits - 1, -1, -1):
            cand = jmax | (1 << bit)
            g = count(lambda key, pos: jnp.logical_and(key == thr, pos <= cand))
            jmax = jnp.where(g <= room, cand, jmax)
        j_sc[...] = jmax

    jmax = j_sc[...]

    def bias_chunk(c, _):
        off = chunk_off(c)
        key = key_sc[:, pl.ds(off, tk)]
        kpos = key_pos(off)
        sel = jnp.logical_or(key > thr, jnp.logical_and(key == thr, kpos <= jmax))
        sel = jnp.logical_and(sel, lax.shift_right_logical(kpos, CHUNK_SHIFT) <= qchunk)
        bias = jnp.where(sel, -jnp.abs(qposf - kpos.astype(F32)), NEG)
        key_sc[:, pl.ds(off, tk)] = pltpu.bitcast(bias, I32)
        return 0

    lax.fori_loop(0, n_chunks, bias_chunk, 0)

    one_at_mid = jnp.where(lane == B_DH, 1.0, 0.0).astype(BF16)
    one_at_0 = jnp.where(lane == 0, 1.0, 0.0).astype(BF16)
    for pair_i in range(B_HEADS // 2):
        ps = slice(pair_i * LANES, (pair_i + 1) * LANES)
        q_pair = qb_ref[0, :, ps]
        outs = []
        for sub in range(2):
            h = 2 * pair_i + sub
            mine = low_half if sub == 0 else jnp.logical_not(low_half)
            qz = jnp.where(mine, q_pair, jnp.zeros_like(q_pair))
            slope = _alibi_slope(h, B_HEADS) * LOG2E
            ones_col = one_at_mid if sub == 0 else one_at_0

            def step(c, carry, qz=qz, ps=ps, mine=mine, slope=slope, ones_col=ones_col):
                off = chunk_off(c)
                k_pair = kb_ref[0, pl.ds(off, tk), ps]
                v_pair = vb_ref[0, pl.ds(off, tk), ps]
                bias = pltpu.bitcast(key_sc[:, pl.ds(off, tk)], F32)
                s = _nt_dot(qz, k_pair) + slope * bias
                v_aug = jnp.where(mine, v_pair, jnp.broadcast_to(ones_col, v_pair.shape))
                return _softmax_step(s, v_aug, *carry)

            init = (jnp.full((tq, 1), NEG, F32), jnp.zeros((tq, LANES), F32))
            _, acc = lax.fori_loop(0, n_chunks, step, init)
            denom = acc[:, B_DH:B_DH + 1] if sub == 0 else acc[:, 0:1]
            outs.append(acc / denom)
        o_ref[0, :, ps] = jnp.where(low_half, outs[0], outs[1]).astype(BF16)


def _attn_b(qb, qi, kw, kidx_t2, kb16, vb16, *, tq, tk, q_pos0, topk):
    b, lq, _ = qb.shape
    lkp = kb16.shape[1]
    blk = lambda bi, i: (bi, i, 0)
    per_batch = lambda bi, i: (bi, 0, 0)
    return pl.pallas_call(
        functools.partial(_attn_b_kernel, tq=tq, tk=tk, q_pos0=q_pos0, topk=topk,
                          idx_bits=max(1, (lkp - 1).bit_length())),
        grid=(b, lq // tq),
        in_specs=[pl.BlockSpec((1, tq, WIDTH), blk), pl.BlockSpec((1, tq, WIDTH), blk),
                  pl.BlockSpec((1, tq, LANES), blk),
                  _resident((1, 2 * IDX_DH, lkp), per_batch),
                  _resident((1, lkp, WIDTH), per_batch), _resident((1, lkp, WIDTH), per_batch)],
        out_specs=pl.BlockSpec((1, tq, WIDTH), blk),
        out_shape=jax.ShapeDtypeStruct((b, lq, WIDTH), BF16),
        scratch_shapes=[pltpu.VMEM((tq, lkp), I32), pltpu.VMEM((tq, 1), I32)],
        compiler_params=pltpu.CompilerParams(dimension_semantics=("arbitrary", "arbitrary"),
                                             vmem_limit_bytes=VMEM_LIMIT),
        name="attn_b",
    )(qb, qi, kw, kidx_t2, kb16, vb16)


def _post_kernel(x_ref, oa_ref, ob_ref, qm_ref, ga_ref, gb_ref, gm_ref, gates_ref, mk_ref,
                 mv_ref, bg_ref, wa_ref, wb_ref, wm_ref, wo_ref, fg_ref, y_ref):
    d = x_ref.shape[-1]
    om = []
    for h in range(M_HEADS):
        hs = slice(h * M_DH, (h + 1) * M_DH)
        s = _nt_dot(qm_ref[0, :, hs], mk_ref[0, :, hs])
        p = jnp.exp2(s - jnp.max(s, axis=-1, keepdims=True))
        o = jnp.dot(p.astype(BF16), mv_ref[0, :, hs], preferred_element_type=F32)
        om.append(o / jnp.sum(p, axis=-1, keepdims=True))
    om = jnp.concatenate(om, axis=1)

    def branch(gate_ref, o, w_ref):
        g = gate_ref[0].astype(F32)
        return jnp.dot((g * jax.nn.sigmoid(g) * o).astype(BF16), w_ref[...],
                       preferred_element_type=F32)

    pa = branch(ga_ref, oa_ref[0].astype(F32), wa_ref)
    pb = branch(gb_ref, ob_ref[0].astype(F32), wb_ref)
    pm = branch(gm_ref, om, wm_ref)
    g = jax.nn.sigmoid(gates_ref[0].astype(F32) + bg_ref[...])
    merged = g[:, :d] * pa + g[:, d:2 * d] * pb + g[:, 2 * d:] * pm
    y = x_ref[0] + jnp.dot(merged.astype(BF16), wo_ref[...], preferred_element_type=F32)
    y_ref[0] = y * lax.rsqrt(jnp.mean(y * y, axis=-1, keepdims=True) + EPS) * fg_ref[...]


def _post(x, oa, ob, qm, ga, gb, gm, gates, mk16, mv16, b_gate, wa, wb, wm, wo, final_g, tm):
    b, lq, d = x.shape
    n_mem = mk16.shape[1]
    blk = lambda w: pl.BlockSpec((1, tm, w), lambda bi, i: (bi, i, 0))
    per_batch = pl.BlockSpec((1, n_mem, WIDTH), lambda bi, i: (bi, 0, 0))
    full = lambda r, c: pl.BlockSpec((r, c), lambda bi, i: (0, 0))
    return pl.pallas_call(
        _post_kernel,
        grid=(b, lq // tm),
        in_specs=[blk(d)] + [blk(WIDTH)] * 6 + [blk(N_BRANCH * d), per_batch, per_batch,
                  full(1, N_BRANCH * d), full(WIDTH, d), full(WIDTH, d), full(WIDTH, d),
                  full(d, d), full(1, d)],
        out_specs=blk(d),
        out_shape=jax.ShapeDtypeStruct((b, lq, d), F32),
        compiler_params=pltpu.CompilerParams(dimension_semantics=("arbitrary", "arbitrary"),
                                             vmem_limit_bytes=VMEM_LIMIT),
        name="post",
    )(x, oa, ob, qm, ga, gb, gm, gates, mk16, mv16, b_gate.reshape(1, -1), wa, wb, wm, wo,
      final_g.reshape(1, d))


def _pad_keys(x, lkp, axis):
    pad = lkp - x.shape[axis]
    if pad == 0:
        return x
    widths = [(0, 0)] * x.ndim
    widths[axis] = (0, pad)
    return jnp.pad(x, widths)


def _layer(x, past, mk16, mv16, q_pos0, norm_g, w16, b_gate, lam_params, lam_init, subln_g,
           wa, wb, wm, wo, final_g, *, tm, tq_a, tq_b, tk):
    b, lq, d = x.shape
    (qa, ka16, va16, ga, qb, kb16, vb16, gb, qi, qm, gm, gates, ka, va, kb, vb, kw) = [
        t.reshape(b, lq, -1) for t in _proj(x.reshape(b * lq, d), norm_g, w16, tm)]
    kidx = kw[..., :IDX_DH]
    new_rows = (ka.reshape(b, lq, A_HEADS, 2, A_DH), va.reshape(b, lq, A_HEADS, 2 * A_DH),
                kb.reshape(b, lq, B_HEADS, B_DH), vb.reshape(b, lq, B_HEADS, B_DH), kidx)
    kidx16 = kidx.astype(BF16)
    if past is not None:
        pa_k, pa_v, pb_k, pb_v, p_kidx = past
        n_past = pa_k.shape[1]
        cat = lambda p, n: jnp.concatenate([p.reshape(b, n_past, -1).astype(BF16), n], axis=1)
        ka16, va16, kb16, vb16, kidx16 = (cat(pa_k, ka16), cat(pa_v, va16), cat(pb_k, kb16),
                                          cat(pb_v, vb16), cat(p_kidx, kidx16))
    lk = ka16.shape[1]
    topk = min(TOPK_MAX, lk // 4)
    lkp = -(-lk // tk) * tk
    ka16, va16, kb16, vb16, kidx16 = [_pad_keys(t, lkp, 1)
                                      for t in (ka16, va16, kb16, vb16, kidx16)]
    kidx_t = jnp.swapaxes(kidx16, 1, 2)
    kidx_t2 = jnp.concatenate([kidx_t, kidx_t], axis=1)

    oa = _attn_a(qa, ka16, va16, lam_params, subln_g, tq=tq_a, tk=tk, q_pos0=q_pos0,
                 lam_init=lam_init)
    ob = _attn_b(qb, qi, kw, kidx_t2, kb16, vb16, tq=tq_b, tk=tk, q_pos0=q_pos0, topk=topk)
    y = _post(x, oa, ob, qm, ga, gb, gm, gates, mk16, mv16, b_gate, wa, wb, wm, wo, final_g, tm)
    return y, new_rows


def kernel(x_prompt, x_sample, mem_prompt, cache_a_k, cache_a_v, cache_b_k, cache_b_v, cache_b_kidx, cache_mem_k, cache_mem_v, norm_g, w_in, b_gate, lam_q1, lam_k1, lam_q2, lam_k2, subln_g, mem_norm_g, w_mem_kv, w_br_a, w_br_b, w_br_m, w_out, final_g):
    depth, d, _ = w_in.shape
    assert depth == 1, "single-layer step only"
    l = 0
    lam_init = 0.8 - 0.6 * math.exp(-0.3 * l)
    w16 = jnp.concatenate([w_in[l][:, :IN_KW_END], jnp.zeros((d, KW_PAD), w_in.dtype),
                           w_in[l][:, IN_KW_END:]], axis=1).astype(BF16)
    lam_params = [p[l].reshape(1, A_DH) for p in (lam_q1, lam_k1, lam_q2, lam_k2)]
    shared = (norm_g[l], w16, b_gate[l], lam_params, lam_init, subln_g[l],
              w_br_a[l].astype(BF16), w_br_b[l].astype(BF16), w_br_m[l].astype(BF16),
              w_out[l].astype(BF16), final_g)

    bp, n_mem, _ = mem_prompt.shape
    mk32, mv32, mk16, mv16 = _memkv(mem_prompt.reshape(bp * n_mem, d), mem_norm_g[l],
                                    w_mem_kv[l].astype(BF16), tm=256)
    y_p, rows_p = _layer(x_prompt, None, mk16.reshape(bp, n_mem, WIDTH),
                         mv16.reshape(bp, n_mem, WIDTH), 0, *shared,
                         tm=256, tq_a=256, tq_b=128, tk=512)

    bs, n_past = cache_a_k.shape[1], cache_a_k.shape[2]
    past = (cache_a_k[l], cache_a_v[l], cache_b_k[l], cache_b_v[l], cache_b_kidx[l])
    ls = x_sample.shape[1]
    y_s, rows_s = _layer(x_sample, past, cache_mem_k[l].reshape(bs, n_mem, WIDTH).astype(BF16),
                         cache_mem_v[l].reshape(bs, n_mem, WIDTH).astype(BF16), n_past, *shared,
                         tm=ls, tq_a=ls, tq_b=ls, tk=384)

    mem_shape = (1, bp, n_mem, M_HEADS, M_DH)
    return (y_p, y_s, *[r[None] for r in rows_p], mk32.reshape(mem_shape),
            mv32.reshape(mem_shape), *[r[None] for r in rows_s])
```

```python
import functools
import math

import jax
import jax.numpy as jnp
import numpy as np
from jax import lax
from jax.experimental import pallas as pl
from jax.experimental.pallas import tpu as pltpu

F32 = jnp.float32
BF16 = jnp.bfloat16
I32 = jnp.int32

EPS = 1e-6
CHUNK = 64
CHUNK_SHIFT = 6
A_HEADS, A_DH = 4, 64
B_HEADS, B_DH = 8, 64
IDX_HEADS, IDX_DH = 8, 64
M_HEADS, M_DH = 4, 128
N_BRANCH = 3
TOPK_MAX = 256
WIDTH = 512
LANES = 128
PAIR = 128
ONES_ROWS = 16
LOG2E = 1.4426950408889634
NEG = -1e30
INT_MAX = 2147483647
KEY_NEG_INF = -2139095041
VMEM_LIMIT = 56 * 1024 * 1024
MAX_SEARCH_STEPS = 100
SUB_KEYS = 256
MAX_EXP2 = 64.0

C_QA, C_KA, C_VA, C_GA = 0, 512, 1024, 1536
C_QB, C_KB, C_VB, C_GB = 2048, 2560, 3072, 3584
C_QI, C_KW, C_QM, C_GM, C_GATES = 4096, 4608, 4736, 5248, 5760
KW_PAD = LANES - IDX_DH - IDX_HEADS
W_COLS = C_GATES + N_BRANCH * 1024
IN_KW_END = 4680


def _bf16_pieces(x, n):
    out, rest = [], float(x)
    for _ in range(n):
        p = float(np.asarray(rest, np.float32).astype(BF16).astype(np.float32))
        out.append(p)
        rest -= p
    return out


LOG2E_PIECES = _bf16_pieces(LOG2E, 3)
POS_SPLIT = 256


def _resident(block_shape, index_map):
    return pl.BlockSpec(block_shape, index_map, pipeline_mode=pl.Buffered(1))


def _alibi_slope(h, n):
    return 2.0 ** (-8.0 * (h + 1) / n)


def _proj_kernel(x_ref, g_ref, w_ref, qa, ka16, va16, ga, qb, kb16, vb16, gb, qi, qm, gm,
                 gates, ka, va, kb, vb, kw):
    x = x_ref[...]
    hn = (x * lax.rsqrt(jnp.mean(x * x, axis=-1, keepdims=True) + EPS) * g_ref[...]).astype(BF16)

    def mm(c0, width):
        return jnp.dot(hn, w_ref[:, c0:c0 + width], preferred_element_type=F32)

    qa[...] = (mm(C_QA, WIDTH) * (A_DH ** -0.5 * LOG2E)).astype(BF16)
    qb[...] = (mm(C_QB, WIDTH) * (B_DH ** -0.5 * LOG2E)).astype(BF16)
    qi[...] = (mm(C_QI, WIDTH) * (IDX_DH ** -0.5)).astype(BF16)
    qm[...] = (mm(C_QM, WIDTH) * (M_DH ** -0.5 * LOG2E)).astype(BF16)
    for c0, o32, o16 in ((C_KA, ka, ka16), (C_VA, va, va16), (C_KB, kb, kb16), (C_VB, vb, vb16)):
        z = mm(c0, WIDTH)
        o32[...] = z
        o16[...] = z.astype(BF16)
    for c0, o16 in ((C_GA, ga), (C_GB, gb), (C_GM, gm)):
        o16[...] = mm(c0, WIDTH).astype(BF16)
    kw[...] = mm(C_KW, LANES)
    for c in range(N_BRANCH * 1024 // WIDTH):
        gates[:, c * WIDTH:(c + 1) * WIDTH] = mm(C_GATES + c * WIDTH, WIDTH).astype(BF16)


def _proj(x2d, norm_g, w16, tm):
    n, d = x2d.shape
    row = lambda i: (i, 0)
    fixed = lambda i: (0, 0)
    widths16 = [WIDTH] * 11 + [N_BRANCH * 1024]
    widths32 = [WIDTH] * 4 + [LANES]
    out_shape = ([jax.ShapeDtypeStruct((n, w), BF16) for w in widths16]
                 + [jax.ShapeDtypeStruct((n, w), F32) for w in widths32])
    out_specs = [pl.BlockSpec((tm, w), row) for w in widths16 + widths32]
    return pl.pallas_call(
        _proj_kernel,
        grid=(n // tm,),
        in_specs=[pl.BlockSpec((tm, d), row), pl.BlockSpec((1, d), fixed),
                  _resident((d, W_COLS), fixed)],
        out_specs=out_specs,
        out_shape=out_shape,
        compiler_params=pltpu.CompilerParams(dimension_semantics=("arbitrary",),
                                             vmem_limit_bytes=VMEM_LIMIT),
        name="proj",
    )(x2d, norm_g.reshape(1, d), w16)


def _memkv_kernel(x_ref, g_ref, w_ref, k32, v32, k16, v16):
    x = x_ref[...]
    hn = (x * lax.rsqrt(jnp.mean(x * x, axis=-1, keepdims=True) + EPS) * g_ref[...]).astype(BF16)
    zk = jnp.dot(hn, w_ref[:, :WIDTH], preferred_element_type=F32)
    zv = jnp.dot(hn, w_ref[:, WIDTH:], preferred_element_type=F32)
    k32[...] = zk
    v32[...] = zv
    k16[...] = zk.astype(BF16)
    v16[...] = zv.astype(BF16)


def _memkv(mem2d, g, w16, tm):
    n, d = mem2d.shape
    row = lambda i: (i, 0)
    fixed = lambda i: (0, 0)
    return pl.pallas_call(
        _memkv_kernel,
        grid=(n // tm,),
        in_specs=[pl.BlockSpec((tm, d), row), pl.BlockSpec((1, d), fixed),
                  pl.BlockSpec((d, 2 * WIDTH), fixed)],
        out_specs=[pl.BlockSpec((tm, WIDTH), row)] * 4,
        out_shape=[jax.ShapeDtypeStruct((n, WIDTH), F32)] * 2
        + [jax.ShapeDtypeStruct((n, WIDTH), BF16)] * 2,
        compiler_params=pltpu.CompilerParams(dimension_semantics=("arbitrary",)),
        name="memkv",
    )(mem2d, g.reshape(1, d), w16)


def _visible_chunks(q0, tq, tk):
    n_vis = (lax.shift_right_logical(q0 + tq - 1, CHUNK_SHIFT) + 1) * CHUNK
    return (n_vis + tk - 1) // tk


def _alibi_key_tile(tk):
    pos = np.arange(tk)
    lo = pos % POS_SPLIT
    tile = np.zeros((tk, LANES), np.float32)
    tile[:, 0:3] = lo[:, None]
    tile[:, 3:6] = (pos - lo)[:, None]
    return jnp.asarray(tile, BF16)


def _alibi_query_rows(slopes_row, width):
    row = lax.broadcasted_iota(I32, (LANES, 1), 0)
    piece = jnp.zeros((LANES, 1), F32)
    for j, p in enumerate(LOG2E_PIECES):
        piece = jnp.where(jnp.logical_or(row == j, row == j + 3), p, piece)
    return (piece * slopes_row).astype(BF16) + jnp.zeros((LANES, width), BF16)


def _split_halves(q_t, tq):
    row = lax.broadcasted_iota(I32, (PAIR, 1), 0)
    zero = jnp.zeros_like(q_t)
    return jnp.concatenate([jnp.where(row < PAIR // 2, q_t, zero),
                            jnp.where(row >= PAIR // 2, q_t, zero)], axis=1)


def _diag_distance(off, q0, tk, tq):
    kpos = off + lax.broadcasted_iota(I32, (tk, 1), 0)
    qpos = q0 + lax.broadcasted_iota(I32, (1, tq), 1)
    return jnp.maximum(kpos - qpos, 0).astype(F32), kpos, qpos


def _online_update(s, v_aug, chunk_base, m_ref, acc_ref, idx):
    m_old = m_ref[idx]
    m_new = jnp.maximum(m_old, jnp.max(s, axis=0, keepdims=True) + chunk_base)
    p = jnp.exp2(s - (m_new - chunk_base)).astype(BF16)
    pv = jnp.dot(v_aug, p, preferred_element_type=F32)
    acc_ref[idx] = jnp.exp2(m_old - m_new) * acc_ref[idx] + pv
    m_ref[idx] = m_new


def _attend(n_groups, n_chunks, tk, width, scores, values, base, frame, m_ref, acc_ref):
    last = n_chunks - 1
    sub = SUB_KEYS if tk % SUB_KEYS == 0 else LANES

    def chunk_off(c):
        return pl.multiple_of(c * tk, tk)

    acc_ref[...] = jnp.zeros(acc_ref.shape, F32)

    def stream_chunk(c, diagonal, tops):
        off = chunk_off(c)
        new_tops = []
        for g in range(n_groups):
            shift = frame(g) - base(g, off)
            top, pv = tops[g], None
            for j in range(tk // sub):
                s = scores(g, off, j, sub, diagonal)
                top = jnp.maximum(top, jnp.max(s, axis=0, keepdims=True) - shift)
                p = jnp.exp2(s - shift).astype(BF16)
                d = jnp.dot(values(g, off + j * sub, sub), p, preferred_element_type=F32)
                pv = d if pv is None else pv + d
            acc_ref[g] += pv
            new_tops.append(top)
        return tuple(new_tops)

    tops = tuple(jnp.full((1, width), -jnp.inf, F32) for _ in range(n_groups))
    tops = stream_chunk(last, True, tops)
    tops = lax.fori_loop(0, last, lambda c, t: stream_chunk(c, False, t), tops)
    off_scale = jnp.abs(tops[0])
    for t in tops[1:]:
        off_scale = jnp.maximum(off_scale, jnp.abs(t))

    @pl.when(jnp.logical_not(jnp.max(off_scale) <= MAX_EXP2))
    def _():
        m_ref[...] = jnp.full(m_ref.shape, NEG, F32)
        acc_ref[...] = jnp.zeros(acc_ref.shape, F32)

        def chunk(c, diagonal):
            off = chunk_off(c)
            for g in range(n_groups):
                _online_update(scores(g, off, 0, tk, diagonal), values(g, off, tk),
                               base(g, off), m_ref, acc_ref, g)

        def body(c, carry):
            chunk(c, False)
            return carry

        lax.fori_loop(0, last, body, 0)
        chunk(last, True)


def _attn_a_kernel(lq1, lk1, lq2, lk2, subg_ref, e_ref, qt_ref, k_ref, vt_ref, o_ref,
                   m_sc, acc_sc, *, tq, tk, q_pos0, lam_init):
    q0 = q_pos0 + pl.program_id(1) * tq
    n_chunks = _visible_chunks(q0, tq, tk)
    lam = (jnp.exp(jnp.sum(lq1[...] * lk1[...], axis=-1, keepdims=True))
           - jnp.exp(jnp.sum(lq2[...] * lk2[...], axis=-1, keepdims=True)) + lam_init)
    slopes = [_alibi_slope(h, A_HEADS) for h in range(A_HEADS)]
    q_aug = []
    for h in range(A_HEADS):
        q_t = qt_ref[0, h * PAIR:(h + 1) * PAIR, :]
        q_aug.append(jnp.concatenate([_split_halves(q_t, tq),
                                      _alibi_query_rows(slopes[h], 2 * tq)], axis=0))
    ahead, kpos, qpos = _diag_distance((n_chunks - 1) * tk, q0, tk, tq)
    ahead = jnp.where(lax.shift_right_logical(kpos, CHUNK_SHIFT)
                      <= lax.shift_right_logical(qpos, CHUNK_SHIFT), ahead * (-2.0 * LOG2E), NEG)

    def scores(h, off, j, size, diagonal):
        hs = slice(h * PAIR, (h + 1) * PAIR)
        start = pl.multiple_of(off + j * size, size)
        k_aug = jnp.concatenate([k_ref[0, pl.ds(start, size), hs],
                                 e_ref[j * size:(j + 1) * size, :]], axis=1)
        s = jnp.dot(k_aug, q_aug[h], preferred_element_type=F32)
        if diagonal:
            fix = ahead[j * size:(j + 1) * size, :] * slopes[h]
            s = s + jnp.concatenate([fix, fix], axis=1)
        return s

    def values(h, start, size):
        start = pl.multiple_of(start, size)
        return jnp.concatenate([vt_ref[0, h * PAIR:(h + 1) * PAIR, pl.ds(start, size)],
                                jnp.ones((ONES_ROWS, size), BF16)], axis=0)

    def base(h, off):
        return off.astype(F32) * (slopes[h] * LOG2E)

    qpos2 = jnp.concatenate([qpos, qpos], axis=1).astype(F32)

    def frame(h):
        return qpos2 * (slopes[h] * LOG2E)

    _attend(A_HEADS, n_chunks, tk, 2 * tq, scores, values, base, frame, m_sc, acc_sc)

    for h in range(A_HEADS):
        acc = acc_sc[h]
        o = acc[:PAIR] / acc[PAIR:PAIR + 1]
        o = o[:, :tq] - lam * o[:, tq:]
        o = o * lax.rsqrt(jnp.mean(o * o, axis=0, keepdims=True) + EPS) * subg_ref[...]
        o_ref[0, :, h * PAIR:(h + 1) * PAIR] = (o * (1.0 - lam_init)).T.astype(BF16)


def _attn_a(qa_t, ka16, va_t, lam_params, subln_g, *, tq, tk, q_pos0, lam_init):
    b, _, lq = qa_t.shape
    lkp = ka16.shape[1]
    small = lambda bi, i: (0, 0)
    return pl.pallas_call(
        functools.partial(_attn_a_kernel, tq=tq, tk=tk, q_pos0=q_pos0, lam_init=lam_init),
        grid=(b, lq // tq),
        in_specs=[pl.BlockSpec((1, A_DH), small)] * 4 + [
            pl.BlockSpec((2 * A_DH, 1), small), pl.BlockSpec((tk, LANES), small),
            pl.BlockSpec((1, WIDTH, tq), lambda bi, i: (bi, 0, i)),
            _resident((1, lkp, WIDTH), lambda bi, i: (bi, 0, 0)),
            _resident((1, WIDTH, lkp), lambda bi, i: (bi, 0, 0))],
        out_specs=pl.BlockSpec((1, tq, WIDTH), lambda bi, i: (bi, i, 0)),
        out_shape=jax.ShapeDtypeStruct((b, lq, WIDTH), BF16),
        scratch_shapes=[pltpu.VMEM((A_HEADS, 1, 2 * tq), F32),
                        pltpu.VMEM((A_HEADS, PAIR + ONES_ROWS, 2 * tq), F32)],
        compiler_params=pltpu.CompilerParams(dimension_semantics=("arbitrary", "arbitrary"),
                                             vmem_limit_bytes=VMEM_LIMIT),
        name="attn_a",
    )(*lam_params, subln_g.reshape(2 * A_DH, 1), _alibi_key_tile(tk), qa_t, ka16, va_t)


def _ordered_key(x):
    bits = pltpu.bitcast(x, I32)
    return bits ^ (lax.shift_right_arithmetic(bits, 31) & INT_MAX)


def _ordered_key_inv(k):
    return pltpu.bitcast(k ^ (lax.shift_right_arithmetic(k, 31) & INT_MAX), F32)


def _attn_b_kernel(e_ref, qbt_ref, qit_ref, wt_ref, kidx_ref, kb_ref, vbt_ref, o_ref,
                   sc, thr_sc, m_sc, acc_sc, *, tq, tk, q_pos0, topk, idx_bits):
    q0 = q_pos0 + pl.program_id(1) * tq
    n_chunks = _visible_chunks(q0, tq, tk)
    last = n_chunks - 1
    topk_f = float(topk)
    qpos = q0 + lax.broadcasted_iota(I32, (1, tq), 1)
    qchunk = lax.shift_right_logical(qpos, CHUNK_SHIFT)
    n_valid = ((qchunk + 1) * CHUNK).astype(F32)

    def chunk_off(c):
        return pl.multiple_of(c * tk, tk)

    def key_pos(off):
        return off + lax.broadcasted_iota(I32, (tk, 1), 0)

    def visible(off):
        return lax.shift_right_logical(key_pos(off), CHUNK_SHIFT) <= qchunk

    qi_all = jnp.concatenate([qit_ref[0, h * IDX_DH:(h + 1) * IDX_DH, :]
                              for h in range(IDX_HEADS)], axis=1)
    w_rows = [wt_ref[0, h:h + 1, :] for h in range(IDX_HEADS)]

    def score_chunk(c, masked):
        off = chunk_off(c)
        logits = jnp.dot(kidx_ref[0, pl.ds(off, tk), :], qi_all, preferred_element_type=F32)
        score = jnp.zeros((tk, tq), F32)
        for h in range(IDX_HEADS):
            score = score + jnp.maximum(logits[:, h * tq:(h + 1) * tq], 0.0) * w_rows[h]
        score = jnp.where(score == 0.0, 0.0, score)
        if masked:
            score = jnp.where(visible(off), score, -jnp.inf)
        sc[pl.ds(off, tk), :] = score
        finite = jnp.where(visible(off), score, 0.0) if masked else score
        return jnp.max(score, axis=0, keepdims=True), jnp.sum(finite, axis=0, keepdims=True)

    def score_body(c, carry):
        mx, sm = score_chunk(c, False)
        return jnp.maximum(carry[0], mx), carry[1] + sm

    row_max, row_sum = lax.fori_loop(
        0, last, score_body, (jnp.full((1, tq), -jnp.inf, F32), jnp.zeros((1, tq), F32)))
    mx, sm = score_chunk(last, True)
    row_max, row_sum = jnp.maximum(row_max, mx), row_sum + sm

    def count(pred):
        def body(c, acc):
            off = chunk_off(c)
            hit = jnp.where(pred(sc[pl.ds(off, tk), :], off), 1.0, 0.0)
            rows = tk
            while rows > 8 and rows % 16 == 0:
                rows //= 2
                hit = hit[:rows] + hit[rows:]
            return acc + jnp.sum(hit.reshape(rows // 8, 8, tq), axis=0)
        acc = lax.fori_loop(0, n_chunks, body, jnp.zeros((8, tq), F32))
        return jnp.sum(acc, axis=0, keepdims=True)

    def count_ge(t):
        return count(lambda s, off: s >= t)

    key_lo0 = jnp.full((1, tq), KEY_NEG_INF, I32)
    key_hi0 = _ordered_key(row_max) + 1
    n_scanned = jnp.broadcast_to((n_chunks * tk).astype(F32), (1, tq))
    mean = row_sum / n_valid
    state0 = (jnp.int32(0), key_lo0, key_hi0, n_scanned,
              mean, jnp.log2(jnp.maximum(0.5 * n_valid, topk_f + 0.5)),
              row_max, jnp.full((1, tq), -0.5, F32))

    def span(klo, khi):
        return khi - klo

    def finished(klo, khi, cnt_lo):
        d = span(klo, khi)
        tight = jnp.logical_or(d == 0, d == 1)
        return jnp.logical_or(jnp.logical_or(cnt_lo == topk_f, tight), n_valid <= topk_f)

    def search_cond(st):
        it, klo, khi, cnt_lo = st[:4]
        todo = jnp.where(finished(klo, khi, cnt_lo), 0.0, 1.0)
        return jnp.logical_and(it < MAX_SEARCH_STEPS, jnp.max(todo) > 0.0)

    def search_body(st):
        it, klo, khi, cnt_lo, ta, la, tb, lb = st
        done = finished(klo, khi, cnt_lo)
        target = math.log2(topk_f)
        guess = ta + (tb - ta) * ((la - target) / (la - lb))
        guess = jnp.where(guess == guess, guess, ta)
        guess = jnp.clip(guess, -3e38, 3e38)
        mid = klo + lax.shift_right_logical(span(klo, khi), 1)
        cand = jnp.where(it % 3 == 2, mid, _ordered_key(guess))
        cand = jnp.minimum(jnp.maximum(cand, klo + 1), khi - 1)
        cand = jnp.where(done, klo, cand)
        t = _ordered_key_inv(cand)
        cnt = count_ge(t)
        up = jnp.logical_and(cnt >= topk_f, jnp.logical_not(done))
        down = jnp.logical_and(cnt < topk_f, jnp.logical_not(done))
        lcnt = jnp.log2(jnp.maximum(cnt, 0.5))
        return (it + 1, jnp.where(up, cand, klo), jnp.where(down, cand, khi),
                jnp.where(up, cnt, cnt_lo),
                jnp.where(up, t, ta), jnp.where(up, lcnt, la),
                jnp.where(down, t, tb), jnp.where(down, lcnt, lb))

    st = lax.while_loop(search_cond, search_body, state0)
    key_lo, cnt_lo = st[1], st[3]
    thr = _ordered_key_inv(key_lo)

    thr_sc[...] = jnp.full((1, tq), INT_MAX, I32)
    tied = jnp.where(jnp.logical_and(cnt_lo != topk_f, n_valid > topk_f), 1.0, 0.0)

    @pl.when(jnp.max(tied) > 0.0)
    def _():
        room = topk_f - count(lambda s, off: s > thr)
        jmax = jnp.zeros((1, tq), I32)
        for bit in range(idx_bits - 1, -1, -1):
            cand = jmax | (1 << bit)
            g = count(lambda s, off: jnp.logical_and(s == thr, key_pos(off) <= cand))
            jmax = jnp.where(g <= room, cand, jmax)
        thr_sc[...] = jmax

    jmax = thr_sc[...]

    ahead, _, _ = _diag_distance(chunk_off(last), q0, tk, tq)

    def mask_chunk(c, masked):
        off = chunk_off(c)
        s = sc[pl.ds(off, tk), :]
        sel = jnp.logical_or(s > thr, jnp.logical_and(s == thr, key_pos(off) <= jmax))
        eff = key_pos(off).astype(F32)
        if masked:
            sel = jnp.logical_and(sel, visible(off))
            eff = eff - 2.0 * ahead
        sc[pl.ds(off, tk), :] = jnp.where(sel, 0.0, NEG)
        return jnp.max(jnp.where(sel, eff, -jnp.inf), axis=0, keepdims=True)

    nearest = lax.fori_loop(0, last, lambda c, n: jnp.maximum(n, mask_chunk(c, False)),
                            jnp.full((1, tq), -jnp.inf, F32))
    nearest = jnp.maximum(nearest, mask_chunk(last, True))

    lane2 = lax.broadcasted_iota(I32, (1, 2 * tq), 1)
    n_pairs = B_HEADS // 2
    slope_rows, q_aug = [], []
    for pi in range(n_pairs):
        slope_row = jnp.where(lane2 < tq, _alibi_slope(2 * pi, B_HEADS),
                              _alibi_slope(2 * pi + 1, B_HEADS)).astype(F32)
        slope_rows.append(slope_row)
        q_t = qbt_ref[0, pi * PAIR:(pi + 1) * PAIR, :]
        q_aug.append(jnp.concatenate([_split_halves(q_t, tq),
                                      _alibi_query_rows(slope_row, 2 * tq)], axis=0))
    ahead2 = jnp.concatenate([ahead, ahead], axis=1) * (-2.0 * LOG2E)
    nearest2 = jnp.concatenate([nearest, nearest], axis=1)

    def scores(pi, off, j, size, diagonal):
        start = pl.multiple_of(off + j * size, size)
        k_aug = jnp.concatenate([kb_ref[0, pl.ds(start, size), pi * PAIR:(pi + 1) * PAIR],
                                 e_ref[j * size:(j + 1) * size, :]], axis=1)
        mask = sc[pl.ds(start, size), :]
        s = (jnp.dot(k_aug, q_aug[pi], preferred_element_type=F32)
             + jnp.concatenate([mask, mask], axis=1))
        if diagonal:
            s = s + ahead2[j * size:(j + 1) * size, :] * slope_rows[pi]
        return s

    def frame(pi):
        return nearest2 * (slope_rows[pi] * LOG2E)

    def values(pi, start, size):
        start = pl.multiple_of(start, size)
        return jnp.concatenate([vbt_ref[0, pi * PAIR:(pi + 1) * PAIR, pl.ds(start, size)],
                                jnp.ones((ONES_ROWS, size), BF16)], axis=0)

    def base(pi, off):
        return off.astype(F32) * (slope_rows[pi] * LOG2E)

    _attend(n_pairs, n_chunks, tk, 2 * tq, scores, values, base, frame, m_sc, acc_sc)

    for pi in range(n_pairs):
        acc = acc_sc[pi]
        o = acc[:PAIR] / acc[PAIR:PAIR + 1]
        o = jnp.concatenate([o[:B_DH, :tq], o[B_DH:, tq:]], axis=0)
        o_ref[0, :, pi * PAIR:(pi + 1) * PAIR] = o.T.astype(BF16)


def _attn_b(qb_t, qi_t, w_t, kidx16, kb16, vb_t, *, tq, tk, q_pos0, topk):
    b, _, lq = qb_t.shape
    lkp = kb16.shape[1]
    q_blk = lambda rows: pl.BlockSpec((1, rows, tq), lambda bi, i: (bi, 0, i))
    per_batch = lambda bi, i: (bi, 0, 0)
    return pl.pallas_call(
        functools.partial(_attn_b_kernel, tq=tq, tk=tk, q_pos0=q_pos0, topk=topk,
                          idx_bits=max(1, (lkp - 1).bit_length())),
        grid=(b, lq // tq),
        in_specs=[pl.BlockSpec((tk, LANES), lambda bi, i: (0, 0)),
                  q_blk(WIDTH), q_blk(WIDTH), q_blk(IDX_HEADS),
                  _resident((1, lkp, IDX_DH), per_batch),
                  _resident((1, lkp, WIDTH), per_batch), _resident((1, WIDTH, lkp), per_batch)],
        out_specs=pl.BlockSpec((1, tq, WIDTH), lambda bi, i: (bi, i, 0)),
        out_shape=jax.ShapeDtypeStruct((b, lq, WIDTH), BF16),
        scratch_shapes=[pltpu.VMEM((lkp, tq), F32), pltpu.VMEM((1, tq), I32),
                        pltpu.VMEM((B_HEADS // 2, 1, 2 * tq), F32),
                        pltpu.VMEM((B_HEADS // 2, PAIR + ONES_ROWS, 2 * tq), F32)],
        compiler_params=pltpu.CompilerParams(dimension_semantics=("arbitrary", "arbitrary"),
                                             vmem_limit_bytes=VMEM_LIMIT),
        name="attn_b",
    )(_alibi_key_tile(tk), qb_t, qi_t, w_t, kidx16, kb16, vb_t)


def _nt_dot(a, b):
    return lax.dot_general(a, b, (((1,), (1,)), ((), ())), preferred_element_type=F32)


def _post_kernel(x_ref, oa_ref, ob_ref, qm_ref, ga_ref, gb_ref, gm_ref, gates_ref, mk_ref,
                 mv_ref, bg_ref, wa_ref, wb_ref, wm_ref, wo_ref, fg_ref, y_ref):
    d = x_ref.shape[-1]
    om = []
    for h in range(M_HEADS):
        hs = slice(h * M_DH, (h + 1) * M_DH)
        s = _nt_dot(qm_ref[0, :, hs], mk_ref[0, :, hs])
        p = jnp.exp2(s - jnp.max(s, axis=-1, keepdims=True))
        o = jnp.dot(p.astype(BF16), mv_ref[0, :, hs], preferred_element_type=F32)
        om.append(o / jnp.sum(p, axis=-1, keepdims=True))
    om = jnp.concatenate(om, axis=1)

    def branch(gate_ref, o, w_ref):
        g = gate_ref[0].astype(F32)
        return jnp.dot((g * jax.nn.sigmoid(g) * o).astype(BF16), w_ref[...],
                       preferred_element_type=F32)

    pa = branch(ga_ref, oa_ref[0].astype(F32), wa_ref)
    pb = branch(gb_ref, ob_ref[0].astype(F32), wb_ref)
    pm = branch(gm_ref, om, wm_ref)
    g = jax.nn.sigmoid(gates_ref[0].astype(F32) + bg_ref[...])
    merged = g[:, :d] * pa + g[:, d:2 * d] * pb + g[:, 2 * d:] * pm
    y = x_ref[0] + jnp.dot(merged.astype(BF16), wo_ref[...], preferred_element_type=F32)
    y_ref[0] = y * lax.rsqrt(jnp.mean(y * y, axis=-1, keepdims=True) + EPS) * fg_ref[...]


def _post(x, oa, ob, qm, ga, gb, gm, gates, mk16, mv16, b_gate, wa, wb, wm, wo, final_g, tm):
    b, lq, d = x.shape
    n_mem = mk16.shape[1]
    blk = lambda w: pl.BlockSpec((1, tm, w), lambda bi, i: (bi, i, 0))
    per_batch = pl.BlockSpec((1, n_mem, WIDTH), lambda bi, i: (bi, 0, 0))
    full = lambda r, c: pl.BlockSpec((r, c), lambda bi, i: (0, 0))
    return pl.pallas_call(
        _post_kernel,
        grid=(b, lq // tm),
        in_specs=[blk(d)] + [blk(WIDTH)] * 6 + [blk(N_BRANCH * d), per_batch, per_batch,
                  full(1, N_BRANCH * d), full(WIDTH, d), full(WIDTH, d), full(WIDTH, d),
                  full(d, d), full(1, d)],
        out_specs=blk(d),
        out_shape=jax.ShapeDtypeStruct((b, lq, d), F32),
        compiler_params=pltpu.CompilerParams(dimension_semantics=("arbitrary", "arbitrary"),
                                             vmem_limit_bytes=VMEM_LIMIT),
        name="post",
    )(x, oa, ob, qm, ga, gb, gm, gates, mk16, mv16, b_gate.reshape(1, -1), wa, wb, wm, wo,
      final_g.reshape(1, d))


def _pad_axis(x, size, axis):
    pad = size - x.shape[axis]
    if pad == 0:
        return x
    widths = [(0, 0)] * x.ndim
    widths[axis] = (0, pad)
    return jnp.pad(x, widths)


def _layer(x, past, mk16, mv16, q_pos0, norm_g, w16, b_gate, lam_params, lam_init, subln_g,
           wa, wb, wm, wo, final_g, *, tm, tq, tk):
    b, lq, d = x.shape
    (qa, ka16, va16, ga, qb, kb16, vb16, gb, qi, qm, gm, gates, ka, va, kb, vb, kw) = [
        t.reshape(b, lq, -1) for t in _proj(x.reshape(b * lq, d), norm_g, w16, tm)]
    kidx = kw[..., :IDX_DH]
    new_rows = (ka.reshape(b, lq, A_HEADS, 2, A_DH), va.reshape(b, lq, A_HEADS, 2 * A_DH),
                kb.reshape(b, lq, B_HEADS, B_DH), vb.reshape(b, lq, B_HEADS, B_DH), kidx)
    kidx16 = kidx.astype(BF16)
    if past is not None:
        pa_k, pa_v, pb_k, pb_v, p_kidx = past
        n_past = pa_k.shape[1]
        cat = lambda p, n: jnp.concatenate([p.reshape(b, n_past, -1).astype(BF16), n], axis=1)
        ka16, va16, kb16, vb16, kidx16 = (cat(pa_k, ka16), cat(pa_v, va16), cat(pb_k, kb16),
                                          cat(pb_v, vb16), cat(p_kidx, kidx16))
    lk = ka16.shape[1]
    topk = min(TOPK_MAX, lk // 4)
    lkp = -(-lk // tk) * tk
    ka16, va16, kb16, vb16, kidx16 = [_pad_axis(t, lkp, 1)
                                      for t in (ka16, va16, kb16, vb16, kidx16)]
    lqp = -(-lq // tq) * tq
    to_lanes = lambda t: jnp.swapaxes(_pad_axis(t, lqp, 1), 1, 2)
    qa_t, qb_t, qi_t = to_lanes(qa), to_lanes(qb), to_lanes(qi)
    w_t = to_lanes(kw[..., IDX_DH:IDX_DH + IDX_HEADS])
    va_t, vb_t = jnp.swapaxes(va16, 1, 2), jnp.swapaxes(vb16, 1, 2)

    oa = _attn_a(qa_t, ka16, va_t, lam_params, subln_g, tq=tq, tk=tk, q_pos0=q_pos0,
                 lam_init=lam_init)[:, :lq]
    ob = _attn_b(qb_t, qi_t, w_t, kidx16, kb16, vb_t, tq=tq, tk=tk, q_pos0=q_pos0,
                 topk=topk)[:, :lq]
    y = _post(x, oa, ob, qm, ga, gb, gm, gates, mk16, mv16, b_gate, wa, wb, wm, wo, final_g, tm)
    return y, new_rows


def kernel(x_prompt, x_sample, mem_prompt, cache_a_k, cache_a_v, cache_b_k, cache_b_v, cache_b_kidx, cache_mem_k, cache_mem_v, norm_g, w_in, b_gate, lam_q1, lam_k1, lam_q2, lam_k2, subln_g, mem_norm_g, w_mem_kv, w_br_a, w_br_b, w_br_m, w_out, final_g):
    depth, d, _ = w_in.shape
    assert depth == 1, "single-layer step only"
    l = 0
    lam_init = 0.8 - 0.6 * math.exp(-0.3 * l)
    w16 = jnp.concatenate([w_in[l][:, :IN_KW_END], jnp.zeros((d, KW_PAD), w_in.dtype),
                           w_in[l][:, IN_KW_END:]], axis=1).astype(BF16)
    lam_params = [p[l].reshape(1, A_DH) for p in (lam_q1, lam_k1, lam_q2, lam_k2)]
    shared = (norm_g[l], w16, b_gate[l], lam_params, lam_init, subln_g[l],
              w_br_a[l].astype(BF16), w_br_b[l].astype(BF16), w_br_m[l].astype(BF16),
              w_out[l].astype(BF16), final_g)

    bp, n_mem, _ = mem_prompt.shape
    mk32, mv32, mk16, mv16 = _memkv(mem_prompt.reshape(bp * n_mem, d), mem_norm_g[l],
                                    w_mem_kv[l].astype(BF16), tm=256)
    y_p, rows_p = _layer(x_prompt, None, mk16.reshape(bp, n_mem, WIDTH),
                         mv16.reshape(bp, n_mem, WIDTH), 0, *shared, tm=256, tq=128, tk=1024)

    bs, n_past = cache_a_k.shape[1], cache_a_k.shape[2]
    past = (cache_a_k[l], cache_a_v[l], cache_b_k[l], cache_b_v[l], cache_b_kidx[l])
    ls = x_sample.shape[1]
    y_s, rows_s = _layer(x_sample, past, cache_mem_k[l].reshape(bs, n_mem, WIDTH).astype(BF16),
                         cache_mem_v[l].reshape(bs, n_mem, WIDTH).astype(BF16), n_past, *shared,
                         tm=ls, tq=128, tk=384)

    mem_shape = (1, bp, n_mem, M_HEADS, M_DH)
    return (y_p, y_s, *[r[None] for r in rows_p], mk32.reshape(mem_shape),
            mv32.reshape(mem_shape), *[r[None] for r in rows_s])
```

```python
import functools
import math

import jax
import jax.numpy as jnp
import numpy as np
from jax import lax
from jax.experimental import pallas as pl
from jax.experimental.pallas import tpu as pltpu

F32 = jnp.float32
BF16 = jnp.bfloat16
I32 = jnp.int32

EPS = 1e-6
CHUNK = 64
CHUNK_SHIFT = 6
A_HEADS, A_DH = 4, 64
B_HEADS, B_DH = 8, 64
IDX_HEADS, IDX_DH = 8, 64
M_HEADS, M_DH = 4, 128
N_BRANCH = 3
TOPK_MAX = 256
WIDTH = 512
LANES = 128
PAIR = 128
ONES_ROWS = 16
V_ROWS = PAIR + ONES_ROWS
LOG2E = 1.4426950408889634
NEG = -1e30
INT_MAX = 2147483647
KEY_NEG_INF = -2139095041
VMEM_LIMIT = 60 * 1024 * 1024
BISECT_EVERY = 8
MAX_SEARCH_STEPS = BISECT_EVERY * 34
ZERO_CODE = 4096.0
SUB_KEYS = 256
MAX_EXP2 = 64.0

C_QA, C_KA, C_VA, C_GA = 0, 512, 1024, 1536
C_QB, C_KB, C_VB, C_GB = 2048, 2560, 3072, 3584
C_QI, C_KW, C_QM, C_GM, C_GATES = 4096, 4608, 4736, 5248, 5760
KW_PAD = LANES - IDX_DH - IDX_HEADS
W_COLS = C_GATES + N_BRANCH * 1024
IN_KW_END = 4680


def _bf16_pieces(x, n):
    out, rest = [], float(x)
    for _ in range(n):
        p = float(np.asarray(rest, np.float32).astype(BF16).astype(np.float32))
        out.append(p)
        rest -= p
    return out


LOG2E_PIECES = _bf16_pieces(LOG2E, 3)
POS_SPLIT = 256


def _resident(block_shape, index_map):
    return pl.BlockSpec(block_shape, index_map, pipeline_mode=pl.Buffered(1))


def _alibi_slope(h, n):
    return 2.0 ** (-8.0 * (h + 1) / n)


def _proj_kernel(x_ref, g_ref, w_ref, qa, ka16, va16, ga, qb, kb16, vb16, gb, qi, qm, gm,
                 gates, ka, va, kb, vb, kw):
    x = x_ref[...]
    hn = (x * lax.rsqrt(jnp.mean(x * x, axis=-1, keepdims=True) + EPS) * g_ref[...]).astype(BF16)

    def mm(c0, width):
        return jnp.dot(hn, w_ref[:, c0:c0 + width], preferred_element_type=F32)

    qa[...] = (mm(C_QA, WIDTH) * (A_DH ** -0.5 * LOG2E)).astype(BF16)
    qb[...] = (mm(C_QB, WIDTH) * (B_DH ** -0.5 * LOG2E)).astype(BF16)
    qi[...] = (mm(C_QI, WIDTH) * (IDX_DH ** -0.5)).astype(BF16)
    qm[...] = (mm(C_QM, WIDTH) * (M_DH ** -0.5 * LOG2E)).astype(BF16)
    for c0, o32, o16 in ((C_KA, ka, ka16), (C_VA, va, va16), (C_KB, kb, kb16), (C_VB, vb, vb16)):
        z = mm(c0, WIDTH)
        o32[...] = z
        o16[...] = z.astype(BF16)
    for c0, o16 in ((C_GA, ga), (C_GB, gb), (C_GM, gm)):
        o16[...] = mm(c0, WIDTH).astype(BF16)
    kw[...] = mm(C_KW, LANES)
    for c in range(N_BRANCH * 1024 // WIDTH):
        gates[:, c * WIDTH:(c + 1) * WIDTH] = mm(C_GATES + c * WIDTH, WIDTH).astype(BF16)


def _proj(x2d, norm_g, w16, tm):
    n, d = x2d.shape
    row = lambda i: (i, 0)
    fixed = lambda i: (0, 0)
    widths16 = [WIDTH] * 11 + [N_BRANCH * 1024]
    widths32 = [WIDTH] * 4 + [LANES]
    out_shape = ([jax.ShapeDtypeStruct((n, w), BF16) for w in widths16]
                 + [jax.ShapeDtypeStruct((n, w), F32) for w in widths32])
    out_specs = [pl.BlockSpec((tm, w), row) for w in widths16 + widths32]
    return pl.pallas_call(
        _proj_kernel,
        grid=(n // tm,),
        in_specs=[pl.BlockSpec((tm, d), row), pl.BlockSpec((1, d), fixed),
                  _resident((d, W_COLS), fixed)],
        out_specs=out_specs,
        out_shape=out_shape,
        compiler_params=pltpu.CompilerParams(dimension_semantics=("arbitrary",),
                                             vmem_limit_bytes=VMEM_LIMIT),
        name="proj",
    )(x2d, norm_g.reshape(1, d), w16)


def _memkv_kernel(x_ref, g_ref, w_ref, k32, v32, k16, v16):
    x = x_ref[...]
    hn = (x * lax.rsqrt(jnp.mean(x * x, axis=-1, keepdims=True) + EPS) * g_ref[...]).astype(BF16)
    zk = jnp.dot(hn, w_ref[:, :WIDTH], preferred_element_type=F32)
    zv = jnp.dot(hn, w_ref[:, WIDTH:], preferred_element_type=F32)
    k32[...] = zk
    v32[...] = zv
    k16[...] = zk.astype(BF16)
    v16[...] = zv.astype(BF16)


def _memkv(mem2d, g, w16, tm):
    n, d = mem2d.shape
    row = lambda i: (i, 0)
    fixed = lambda i: (0, 0)
    return pl.pallas_call(
        _memkv_kernel,
        grid=(n // tm,),
        in_specs=[pl.BlockSpec((tm, d), row), pl.BlockSpec((1, d), fixed),
                  pl.BlockSpec((d, 2 * WIDTH), fixed)],
        out_specs=[pl.BlockSpec((tm, WIDTH), row)] * 4,
        out_shape=[jax.ShapeDtypeStruct((n, WIDTH), F32)] * 2
        + [jax.ShapeDtypeStruct((n, WIDTH), BF16)] * 2,
        compiler_params=pltpu.CompilerParams(dimension_semantics=("arbitrary",)),
        name="memkv",
    )(mem2d, g.reshape(1, d), w16)


def _visible_chunks(q0, tq, tk):
    n_vis = (lax.shift_right_logical(q0 + tq - 1, CHUNK_SHIFT) + 1) * CHUNK
    return (n_vis + tk - 1) // tk


def _alibi_key_tile(tk):
    pos = np.arange(tk)
    lo = pos % POS_SPLIT
    tile = np.zeros((tk, LANES), np.float32)
    tile[:, 0:3] = lo[:, None]
    tile[:, 3:6] = (pos - lo)[:, None]
    return jnp.asarray(tile, BF16)


def _alibi_query_rows(slopes_row, width):
    row = lax.broadcasted_iota(I32, (LANES, 1), 0)
    piece = jnp.zeros((LANES, 1), F32)
    for j, p in enumerate(LOG2E_PIECES):
        piece = jnp.where(jnp.logical_or(row == j, row == j + 3), p, piece)
    return (piece * slopes_row).astype(BF16) + jnp.zeros((LANES, width), BF16)


def _split_halves(q_t, tq):
    row = lax.broadcasted_iota(I32, (PAIR, 1), 0)
    zero = jnp.zeros_like(q_t)
    return jnp.concatenate([jnp.where(row < PAIR // 2, q_t, zero),
                            jnp.where(row >= PAIR // 2, q_t, zero)], axis=1)


def _diag_distance(off, q0, tk, tq):
    kpos = off + lax.broadcasted_iota(I32, (tk, 1), 0)
    qpos = q0 + lax.broadcasted_iota(I32, (1, tq), 1)
    return jnp.maximum(kpos - qpos, 0).astype(F32), kpos, qpos


def _online_update(s, v_aug, chunk_base, m_ref, acc_ref, idx):
    m_old = m_ref[idx]
    m_new = jnp.maximum(m_old, jnp.max(s, axis=0, keepdims=True) + chunk_base)
    p = jnp.exp2(s - (m_new - chunk_base)).astype(BF16)
    pv = jnp.dot(v_aug, p, preferred_element_type=F32)
    acc_ref[idx] = jnp.exp2(m_old - m_new) * acc_ref[idx] + pv
    m_ref[idx] = m_new


def _attend(n_groups, n_chunks, tk, width, scores, values, base, frame, m_ref, acc_ref, p_ref):
    last = n_chunks - 1
    sub = SUB_KEYS if tk % SUB_KEYS == 0 else LANES

    def chunk_off(c):
        return pl.multiple_of(c * tk, tk)

    acc_ref[...] = jnp.zeros(acc_ref.shape, F32)

    def weigh(c, diagonal, slot, tops):
        off = chunk_off(c)
        new_tops = []
        for g in range(n_groups):
            shift = frame(g) - base(g, off)
            top = tops[g]
            for j in range(tk // sub):
                s = scores(g, off, j, sub, diagonal)
                top = jnp.maximum(top, jnp.max(s, axis=0, keepdims=True) - shift)
                p_ref[slot, g, j * sub:(j + 1) * sub, :] = jnp.exp2(s - shift).astype(BF16)
            new_tops.append(top)
        return tuple(new_tops)

    def gather(c, slot):
        off = chunk_off(c)
        for g in range(n_groups):
            acc_ref[g] += jnp.dot(values(g, off, tk), p_ref[slot, g],
                                  preferred_element_type=F32)

    def step(i, tops):
        slot = lax.rem(i, 2)
        gather(jnp.where(i == 0, last, i - 1), slot)
        return weigh(i, False, 1 - slot, tops)

    tops = tuple(jnp.full((1, width), -jnp.inf, F32) for _ in range(n_groups))
    tops = weigh(last, True, 0, tops)
    tops = lax.fori_loop(0, last, step, tops)
    gather(jnp.where(last == 0, 0, last - 1), lax.rem(last, 2))
    off_scale = jnp.abs(tops[0])
    for t in tops[1:]:
        off_scale = jnp.maximum(off_scale, jnp.abs(t))

    @pl.when(jnp.logical_not(jnp.max(off_scale) <= MAX_EXP2))
    def _():
        m_ref[...] = jnp.full(m_ref.shape, NEG, F32)
        acc_ref[...] = jnp.zeros(acc_ref.shape, F32)

        def chunk(c, diagonal):
            off = chunk_off(c)
            for g in range(n_groups):
                _online_update(scores(g, off, 0, tk, diagonal), values(g, off, tk),
                               base(g, off), m_ref, acc_ref, g)

        def body(c, carry):
            chunk(c, False)
            return carry

        lax.fori_loop(0, last, body, 0)
        chunk(last, True)


def _attn_a_kernel(lq1, lk1, lq2, lk2, subg_ref, e_ref, qt_ref, k_ref, vt_ref, o_ref,
                   m_sc, acc_sc, p_sc, *, tq, tk, q_pos0, lam_init):
    q0 = q_pos0 + pl.program_id(1) * tq
    n_chunks = _visible_chunks(q0, tq, tk)
    lam = (jnp.exp(jnp.sum(lq1[...] * lk1[...], axis=-1, keepdims=True))
           - jnp.exp(jnp.sum(lq2[...] * lk2[...], axis=-1, keepdims=True)) + lam_init)
    slopes = [_alibi_slope(h, A_HEADS) for h in range(A_HEADS)]
    q_aug = []
    for h in range(A_HEADS):
        q_t = qt_ref[0, h * PAIR:(h + 1) * PAIR, :]
        q_aug.append(jnp.concatenate([_split_halves(q_t, tq),
                                      _alibi_query_rows(slopes[h], 2 * tq)], axis=0))
    ahead, kpos, qpos = _diag_distance((n_chunks - 1) * tk, q0, tk, tq)
    ahead = jnp.where(lax.shift_right_logical(kpos, CHUNK_SHIFT)
                      <= lax.shift_right_logical(qpos, CHUNK_SHIFT), ahead * (-2.0 * LOG2E), NEG)

    def scores(h, off, j, size, diagonal):
        hs = slice(h * PAIR, (h + 1) * PAIR)
        start = pl.multiple_of(off + j * size, size)
        k_aug = jnp.concatenate([k_ref[0, pl.ds(start, size), hs],
                                 e_ref[j * size:(j + 1) * size, :]], axis=1)
        s = jnp.dot(k_aug, q_aug[h], preferred_element_type=F32)
        if diagonal:
            fix = ahead[j * size:(j + 1) * size, :] * slopes[h]
            s = s + jnp.concatenate([fix, fix], axis=1)
        return s

    def values(h, start, size):
        start = pl.multiple_of(start, size)
        return vt_ref[0, h * V_ROWS:(h + 1) * V_ROWS, pl.ds(start, size)]

    def base(h, off):
        return off.astype(F32) * (slopes[h] * LOG2E)

    qpos2 = jnp.concatenate([qpos, qpos], axis=1).astype(F32)

    def frame(h):
        return qpos2 * (slopes[h] * LOG2E)

    _attend(A_HEADS, n_chunks, tk, 2 * tq, scores, values, base, frame, m_sc, acc_sc, p_sc)

    for h in range(A_HEADS):
        acc = acc_sc[h]
        o = acc[:PAIR] / acc[PAIR:PAIR + 1]
        o = o[:, :tq] - lam * o[:, tq:]
        o = o * lax.rsqrt(jnp.mean(o * o, axis=0, keepdims=True) + EPS) * subg_ref[...]
        o_ref[0, :, h * PAIR:(h + 1) * PAIR] = (o * (1.0 - lam_init)).T.astype(BF16)


def _attn_a(qa_t, ka16, va_t, lam_params, subln_g, *, tq, tk, q_pos0, lam_init):
    b, _, lq = qa_t.shape
    lkp = ka16.shape[1]
    small = lambda bi, i: (0, 0)
    return pl.pallas_call(
        functools.partial(_attn_a_kernel, tq=tq, tk=tk, q_pos0=q_pos0, lam_init=lam_init),
        grid=(b, lq // tq),
        in_specs=[pl.BlockSpec((1, A_DH), small)] * 4 + [
            pl.BlockSpec((2 * A_DH, 1), small), pl.BlockSpec((tk, LANES), small),
            pl.BlockSpec((1, WIDTH, tq), lambda bi, i: (bi, 0, i)),
            _resident((1, lkp, WIDTH), lambda bi, i: (bi, 0, 0)),
            _resident((1, A_HEADS * V_ROWS, lkp), lambda bi, i: (bi, 0, 0))],
        out_specs=pl.BlockSpec((1, tq, WIDTH), lambda bi, i: (bi, i, 0)),
        out_shape=jax.ShapeDtypeStruct((b, lq, WIDTH), BF16),
        scratch_shapes=[pltpu.VMEM((A_HEADS, 1, 2 * tq), F32),
                        pltpu.VMEM((A_HEADS, PAIR + ONES_ROWS, 2 * tq), F32),
                        pltpu.VMEM((2, A_HEADS, tk, 2 * tq), BF16)],
        compiler_params=pltpu.CompilerParams(dimension_semantics=("arbitrary", "arbitrary"),
                                             vmem_limit_bytes=VMEM_LIMIT),
        name="attn_a",
    )(*lam_params, subln_g.reshape(2 * A_DH, 1), _alibi_key_tile(tk), qa_t, ka16, va_t)


def _sum_keys(x, tk, tq):
    rows = tk
    while rows > 8 and rows % 16 == 0:
        rows //= 2
        x = x[:rows] + x[rows:]
    return jnp.sum(x, axis=0, keepdims=True)


def _ordered_key(x):
    bits = pltpu.bitcast(x, I32)
    return bits ^ (lax.shift_right_arithmetic(bits, 31) & INT_MAX)


def _ordered_key_inv(k):
    return pltpu.bitcast(k ^ (lax.shift_right_arithmetic(k, 31) & INT_MAX), F32)


def _attn_b_kernel(e_ref, qbt_ref, qit_ref, wt_ref, kidx_ref, kb_ref, vbt_ref, o_ref,
                   sc, thr_sc, m_sc, acc_sc, p_sc, *, tq, tk, q_pos0, topk, idx_bits):
    q0 = q_pos0 + pl.program_id(1) * tq
    n_chunks = _visible_chunks(q0, tq, tk)
    last = n_chunks - 1
    topk_f = float(topk)
    qpos = q0 + lax.broadcasted_iota(I32, (1, tq), 1)
    qchunk = lax.shift_right_logical(qpos, CHUNK_SHIFT)
    n_valid = ((qchunk + 1) * CHUNK).astype(F32)

    def chunk_off(c):
        return pl.multiple_of(c * tk, tk)

    def key_pos(off):
        return off + lax.broadcasted_iota(I32, (tk, 1), 0)

    def visible(off):
        return lax.shift_right_logical(key_pos(off), CHUNK_SHIFT) <= qchunk

    qi_all = jnp.concatenate([qit_ref[0, h * IDX_DH:(h + 1) * IDX_DH, :]
                              for h in range(IDX_HEADS)], axis=1)
    w_rows = [wt_ref[0, h:h + 1, :] for h in range(IDX_HEADS)]

    def score_chunk(c, masked):
        off = chunk_off(c)
        logits = jnp.dot(kidx_ref[0, pl.ds(off, tk), :], qi_all, preferred_element_type=F32)
        score = jnp.zeros((tk, tq), F32)
        for h in range(IDX_HEADS):
            score = score + jnp.maximum(logits[:, h * tq:(h + 1) * tq], 0.0) * w_rows[h]
        score = jnp.where(score == 0.0, 0.0, score)
        if masked:
            score = jnp.where(visible(off), score, -jnp.inf)
        sc[pl.ds(off, tk), :] = score
        lowest = jnp.where(visible(off), score, jnp.inf) if masked else score
        signs = jnp.where(score > 0.0, ZERO_CODE + 1.0, jnp.where(score >= 0.0, ZERO_CODE, 0.0))
        coded = _sum_keys(signs, tk, tq)
        at_least_0 = jnp.floor(coded * (1.0 / ZERO_CODE))
        return (jnp.max(score, axis=0, keepdims=True), jnp.min(lowest, axis=0, keepdims=True),
                at_least_0, coded - ZERO_CODE * at_least_0)

    def merge_stats(a, b):
        return (jnp.maximum(a[0], b[0]), jnp.minimum(a[1], b[1]), a[2] + b[2], a[3] + b[3])

    stats = lax.fori_loop(
        0, last, lambda c, st: merge_stats(st, score_chunk(c, False)),
        (jnp.full((1, tq), -jnp.inf, F32), jnp.full((1, tq), jnp.inf, F32),
         jnp.zeros((1, tq), F32), jnp.zeros((1, tq), F32)))
    row_max, row_min, n_ge0, n_pos = merge_stats(stats, score_chunk(last, True))

    def count(pred):
        def body(c, acc):
            off = chunk_off(c)
            return acc + _sum_keys(jnp.where(pred(sc[pl.ds(off, tk), :], off), 1.0, 0.0),
                                   tk, tq)
        return lax.fori_loop(0, n_chunks, body, jnp.zeros((1, tq), F32))

    def count_ge(t):
        return count(lambda s, off: s >= t)

    def spread(cnt):
        c = jnp.clip(cnt, 0.5, n_valid - 0.5)
        return jnp.log2(c / (n_valid - c))

    target = spread(jnp.full((1, tq), topk_f, F32))
    key_lo, key_hi = _ordered_key(row_min), _ordered_key(row_max) + 1
    cnt_lo = n_valid
    t_a, f_a = row_min, spread(n_valid) - target
    t_b, f_b = row_max, spread(jnp.zeros((1, tq), F32)) - target
    zero_up = n_ge0 >= topk_f
    key_lo = jnp.where(zero_up, jnp.maximum(key_lo, 0), key_lo)
    cnt_lo = jnp.where(zero_up, n_ge0, cnt_lo)
    t_a = jnp.where(zero_up, 0.0, t_a)
    f_a = jnp.where(zero_up, spread(n_ge0) - target, f_a)
    zero_down = n_pos < topk_f
    key_hi = jnp.where(zero_down, jnp.minimum(key_hi, 1), key_hi)
    t_b = jnp.where(zero_down, 0.0, t_b)
    f_b = jnp.where(zero_down, spread(n_pos) - target, f_b)
    state0 = (jnp.int32(0), key_lo, key_hi, cnt_lo, t_a, f_a, t_b, f_b, jnp.zeros((1, tq), F32))

    def span(klo, khi):
        return khi - klo

    def finished(klo, khi, cnt_lo):
        d = span(klo, khi)
        tight = jnp.logical_or(d == 0, d == 1)
        return jnp.logical_or(jnp.logical_or(cnt_lo == topk_f, tight), n_valid <= topk_f)

    def search_cond(st):
        it, klo, khi, cnt_lo = st[:4]
        todo = jnp.where(finished(klo, khi, cnt_lo), 0.0, 1.0)
        return jnp.logical_and(it < MAX_SEARCH_STEPS, jnp.max(todo) > 0.0)

    def search_body(st):
        it, klo, khi, cnt_lo, ta, fa, tb, fb, side = st
        done = finished(klo, khi, cnt_lo)
        guess = ta + (tb - ta) * (fa / (fa - fb))
        guess = jnp.where(guess == guess, guess, ta)
        guess = jnp.clip(guess, -3e38, 3e38)
        mid = klo + lax.shift_right_logical(span(klo, khi), 1)
        cand = jnp.where(it % BISECT_EVERY == BISECT_EVERY - 1, mid, _ordered_key(guess))
        cand = jnp.minimum(jnp.maximum(cand, klo + 1), khi - 1)
        cand = jnp.where(done, klo, cand)
        t = _ordered_key_inv(cand)
        cnt = count_ge(t)
        up = jnp.logical_and(cnt >= topk_f, jnp.logical_not(done))
        down = jnp.logical_and(cnt < topk_f, jnp.logical_not(done))
        f = spread(cnt) - target
        fb = jnp.where(jnp.logical_and(up, side > 0.0), 0.5 * fb, fb)
        fa = jnp.where(jnp.logical_and(down, side < 0.0), 0.5 * fa, fa)
        return (it + 1, jnp.where(up, cand, klo), jnp.where(down, cand, khi),
                jnp.where(up, cnt, cnt_lo),
                jnp.where(up, t, ta), jnp.where(up, f, fa),
                jnp.where(down, t, tb), jnp.where(down, f, fb),
                jnp.where(up, 1.0, jnp.where(down, -1.0, side)))

    st = lax.while_loop(search_cond, search_body, state0)
    key_lo, cnt_lo = st[1], st[3]
    thr = _ordered_key_inv(key_lo)

    thr_sc[...] = jnp.full((1, tq), INT_MAX, I32)
    tied = jnp.where(jnp.logical_and(cnt_lo != topk_f, n_valid > topk_f), 1.0, 0.0)

    @pl.when(jnp.max(tied) > 0.0)
    def _():
        room = topk_f - count(lambda s, off: s > thr)
        jmax = jnp.zeros((1, tq), I32)
        for bit in range(idx_bits - 1, -1, -1):
            cand = jmax | (1 << bit)
            g = count(lambda s, off: jnp.logical_and(s == thr, key_pos(off) <= cand))
            jmax = jnp.where(g <= room, cand, jmax)
        thr_sc[...] = jmax

    jmax = thr_sc[...]

    ahead, _, _ = _diag_distance(chunk_off(last), q0, tk, tq)

    def mask_chunk(c, masked):
        off = chunk_off(c)
        s = sc[pl.ds(off, tk), :]
        sel = jnp.logical_or(s > thr, jnp.logical_and(s == thr, key_pos(off) <= jmax))
        eff = key_pos(off).astype(F32)
        if masked:
            sel = jnp.logical_and(sel, visible(off))
            eff = eff - 2.0 * ahead
        sc[pl.ds(off, tk), :] = jnp.where(sel, 0.0, NEG)
        return jnp.max(jnp.where(sel, eff, -jnp.inf), axis=0, keepdims=True)

    nearest = lax.fori_loop(0, last, lambda c, n: jnp.maximum(n, mask_chunk(c, False)),
                            jnp.full((1, tq), -jnp.inf, F32))
    nearest = jnp.maximum(nearest, mask_chunk(last, True))

    lane2 = lax.broadcasted_iota(I32, (1, 2 * tq), 1)
    n_pairs = B_HEADS // 2
    slope_rows, q_aug = [], []
    for pi in range(n_pairs):
        slope_row = jnp.where(lane2 < tq, _alibi_slope(2 * pi, B_HEADS),
                              _alibi_slope(2 * pi + 1, B_HEADS)).astype(F32)
        slope_rows.append(slope_row)
        q_t = qbt_ref[0, pi * PAIR:(pi + 1) * PAIR, :]
        q_aug.append(jnp.concatenate([_split_halves(q_t, tq),
                                      _alibi_query_rows(slope_row, 2 * tq)], axis=0))
    ahead2 = jnp.concatenate([ahead, ahead], axis=1) * (-2.0 * LOG2E)
    nearest2 = jnp.concatenate([nearest, nearest], axis=1)

    def scores(pi, off, j, size, diagonal):
        start = pl.multiple_of(off + j * size, size)
        k_aug = jnp.concatenate([kb_ref[0, pl.ds(start, size), pi * PAIR:(pi + 1) * PAIR],
                                 e_ref[j * size:(j + 1) * size, :]], axis=1)
        mask = sc[pl.ds(start, size), :]
        s = (jnp.dot(k_aug, q_aug[pi], preferred_element_type=F32)
             + jnp.concatenate([mask, mask], axis=1))
        if diagonal:
            s = s + ahead2[j * size:(j + 1) * size, :] * slope_rows[pi]
        return s

    def frame(pi):
        return nearest2 * (slope_rows[pi] * LOG2E)

    def values(pi, start, size):
        start = pl.multiple_of(start, size)
        return vbt_ref[0, pi * V_ROWS:(pi + 1) * V_ROWS, pl.ds(start, size)]

    def base(pi, off):
        return off.astype(F32) * (slope_rows[pi] * LOG2E)

    _attend(n_pairs, n_chunks, tk, 2 * tq, scores, values, base, frame, m_sc, acc_sc, p_sc)

    for pi in range(n_pairs):
        acc = acc_sc[pi]
        o = acc[:PAIR] / acc[PAIR:PAIR + 1]
        o = jnp.concatenate([o[:B_DH, :tq], o[B_DH:, tq:]], axis=0)
        o_ref[0, :, pi * PAIR:(pi + 1) * PAIR] = o.T.astype(BF16)


def _attn_b(qb_t, qi_t, w_t, kidx16, kb16, vb_t, *, tq, tk, q_pos0, topk):
    b, _, lq = qb_t.shape
    lkp = kb16.shape[1]
    q_blk = lambda rows: pl.BlockSpec((1, rows, tq), lambda bi, i: (bi, 0, i))
    per_batch = lambda bi, i: (bi, 0, 0)
    return pl.pallas_call(
        functools.partial(_attn_b_kernel, tq=tq, tk=tk, q_pos0=q_pos0, topk=topk,
                          idx_bits=max(1, (lkp - 1).bit_length())),
        grid=(b, lq // tq),
        in_specs=[pl.BlockSpec((tk, LANES), lambda bi, i: (0, 0)),
                  q_blk(WIDTH), q_blk(WIDTH), q_blk(IDX_HEADS),
                  _resident((1, lkp, IDX_DH), per_batch),
                  _resident((1, lkp, WIDTH), per_batch),
                  _resident((1, B_HEADS // 2 * V_ROWS, lkp), per_batch)],
        out_specs=pl.BlockSpec((1, tq, WIDTH), lambda bi, i: (bi, i, 0)),
        out_shape=jax.ShapeDtypeStruct((b, lq, WIDTH), BF16),
        scratch_shapes=[pltpu.VMEM((lkp, tq), F32), pltpu.VMEM((1, tq), I32),
                        pltpu.VMEM((B_HEADS // 2, 1, 2 * tq), F32),
                        pltpu.VMEM((B_HEADS // 2, PAIR + ONES_ROWS, 2 * tq), F32),
                        pltpu.VMEM((2, B_HEADS // 2, tk, 2 * tq), BF16)],
        compiler_params=pltpu.CompilerParams(dimension_semantics=("arbitrary", "arbitrary"),
                                             vmem_limit_bytes=VMEM_LIMIT),
        name="attn_b",
    )(_alibi_key_tile(tk), qb_t, qi_t, w_t, kidx16, kb16, vb_t)


def _nt_dot(a, b):
    return lax.dot_general(a, b, (((1,), (1,)), ((), ())), preferred_element_type=F32)


def _post_kernel(x_ref, oa_ref, ob_ref, qm_ref, ga_ref, gb_ref, gm_ref, gates_ref, mk_ref,
                 mv_ref, bg_ref, wa_ref, wb_ref, wm_ref, wo_ref, fg_ref, y_ref):
    d = x_ref.shape[-1]
    om = []
    for h in range(M_HEADS):
        hs = slice(h * M_DH, (h + 1) * M_DH)
        s = _nt_dot(qm_ref[0, :, hs], mk_ref[0, :, hs])
        p = jnp.exp2(s - jnp.max(s, axis=-1, keepdims=True))
        o = jnp.dot(p.astype(BF16), mv_ref[0, :, hs], preferred_element_type=F32)
        om.append(o / jnp.sum(p, axis=-1, keepdims=True))
    om = jnp.concatenate(om, axis=1)

    def branch(gate_ref, o, w_ref):
        g = gate_ref[0].astype(F32)
        return jnp.dot((g * jax.nn.sigmoid(g) * o).astype(BF16), w_ref[...],
                       preferred_element_type=F32)

    pa = branch(ga_ref, oa_ref[0].astype(F32), wa_ref)
    pb = branch(gb_ref, ob_ref[0].astype(F32), wb_ref)
    pm = branch(gm_ref, om, wm_ref)
    g = jax.nn.sigmoid(gates_ref[0].astype(F32) + bg_ref[...])
    merged = g[:, :d] * pa + g[:, d:2 * d] * pb + g[:, 2 * d:] * pm
    y = x_ref[0] + jnp.dot(merged.astype(BF16), wo_ref[...], preferred_element_type=F32)
    y_ref[0] = y * lax.rsqrt(jnp.mean(y * y, axis=-1, keepdims=True) + EPS) * fg_ref[...]


def _post(x, oa, ob, qm, ga, gb, gm, gates, mk16, mv16, b_gate, wa, wb, wm, wo, final_g, tm):
    b, lq, d = x.shape
    n_mem = mk16.shape[1]
    blk = lambda w: pl.BlockSpec((1, tm, w), lambda bi, i: (bi, i, 0))
    per_batch = pl.BlockSpec((1, n_mem, WIDTH), lambda bi, i: (bi, 0, 0))
    full = lambda r, c: pl.BlockSpec((r, c), lambda bi, i: (0, 0))
    return pl.pallas_call(
        _post_kernel,
        grid=(b, lq // tm),
        in_specs=[blk(d)] + [blk(WIDTH)] * 6 + [blk(N_BRANCH * d), per_batch, per_batch,
                  full(1, N_BRANCH * d), full(WIDTH, d), full(WIDTH, d), full(WIDTH, d),
                  full(d, d), full(1, d)],
        out_specs=blk(d),
        out_shape=jax.ShapeDtypeStruct((b, lq, d), F32),
        compiler_params=pltpu.CompilerParams(dimension_semantics=("arbitrary", "arbitrary"),
                                             vmem_limit_bytes=VMEM_LIMIT),
        name="post",
    )(x, oa, ob, qm, ga, gb, gm, gates, mk16, mv16, b_gate.reshape(1, -1), wa, wb, wm, wo,
      final_g.reshape(1, d))


def _values_on_lanes(v16):
    b, lk, _ = v16.shape
    v_t = jnp.swapaxes(v16, 1, 2).reshape(b, WIDTH // PAIR, PAIR, lk)
    ones = jnp.ones((b, WIDTH // PAIR, ONES_ROWS, lk), v16.dtype)
    return jnp.concatenate([v_t, ones], axis=2).reshape(b, WIDTH // PAIR * V_ROWS, lk)


def _pad_axis(x, size, axis):
    pad = size - x.shape[axis]
    if pad == 0:
        return x
    widths = [(0, 0)] * x.ndim
    widths[axis] = (0, pad)
    return jnp.pad(x, widths)


def _layer(x, past, mk16, mv16, q_pos0, norm_g, w16, b_gate, lam_params, lam_init, subln_g,
           wa, wb, wm, wo, final_g, *, tm, tq, tk):
    b, lq, d = x.shape
    (qa, ka16, va16, ga, qb, kb16, vb16, gb, qi, qm, gm, gates, ka, va, kb, vb, kw) = [
        t.reshape(b, lq, -1) for t in _proj(x.reshape(b * lq, d), norm_g, w16, tm)]
    kidx = kw[..., :IDX_DH]
    new_rows = (ka.reshape(b, lq, A_HEADS, 2, A_DH), va.reshape(b, lq, A_HEADS, 2 * A_DH),
                kb.reshape(b, lq, B_HEADS, B_DH), vb.reshape(b, lq, B_HEADS, B_DH), kidx)
    kidx16 = kidx.astype(BF16)
    if past is not None:
        pa_k, pa_v, pb_k, pb_v, p_kidx = past
        n_past = pa_k.shape[1]
        cat = lambda p, n: jnp.concatenate([p.reshape(b, n_past, -1).astype(BF16), n], axis=1)
        ka16, va16, kb16, vb16, kidx16 = (cat(pa_k, ka16), cat(pa_v, va16), cat(pb_k, kb16),
                                          cat(pb_v, vb16), cat(p_kidx, kidx16))
    lk = ka16.shape[1]
    topk = min(TOPK_MAX, lk // 4)
    lkp = -(-lk // tk) * tk
    ka16, va16, kb16, vb16, kidx16 = [_pad_axis(t, lkp, 1)
                                      for t in (ka16, va16, kb16, vb16, kidx16)]
    lqp = -(-lq // tq) * tq
    to_lanes = lambda t: jnp.swapaxes(_pad_axis(t, lqp, 1), 1, 2)
    qa_t, qb_t, qi_t = to_lanes(qa), to_lanes(qb), to_lanes(qi)
    w_t = to_lanes(kw[..., IDX_DH:IDX_DH + IDX_HEADS])
    va_t, vb_t = _values_on_lanes(va16), _values_on_lanes(vb16)

    oa = _attn_a(qa_t, ka16, va_t, lam_params, subln_g, tq=tq, tk=tk, q_pos0=q_pos0,
                 lam_init=lam_init)[:, :lq]
    ob = _attn_b(qb_t, qi_t, w_t, kidx16, kb16, vb_t, tq=tq, tk=tk, q_pos0=q_pos0,
                 topk=topk)[:, :lq]
    y = _post(x, oa, ob, qm, ga, gb, gm, gates, mk16, mv16, b_gate, wa, wb, wm, wo, final_g, tm)
    return y, new_rows


def kernel(x_prompt, x_sample, mem_prompt, cache_a_k, cache_a_v, cache_b_k, cache_b_v, cache_b_kidx, cache_mem_k, cache_mem_v, norm_g, w_in, b_gate, lam_q1, lam_k1, lam_q2, lam_k2, subln_g, mem_norm_g, w_mem_kv, w_br_a, w_br_b, w_br_m, w_out, final_g):
    depth, d, _ = w_in.shape
    assert depth == 1, "single-layer step only"
    l = 0
    lam_init = 0.8 - 0.6 * math.exp(-0.3 * l)
    w16 = jnp.concatenate([w_in[l][:, :IN_KW_END], jnp.zeros((d, KW_PAD), w_in.dtype),
                           w_in[l][:, IN_KW_END:]], axis=1).astype(BF16)
    lam_params = [p[l].reshape(1, A_DH) for p in (lam_q1, lam_k1, lam_q2, lam_k2)]
    shared = (norm_g[l], w16, b_gate[l], lam_params, lam_init, subln_g[l],
              w_br_a[l].astype(BF16), w_br_b[l].astype(BF16), w_br_m[l].astype(BF16),
              w_out[l].astype(BF16), final_g)

    bp, n_mem, _ = mem_prompt.shape
    mk32, mv32, mk16, mv16 = _memkv(mem_prompt.reshape(bp * n_mem, d), mem_norm_g[l],
                                    w_mem_kv[l].astype(BF16), tm=256)
    y_p, rows_p = _layer(x_prompt, None, mk16.reshape(bp, n_mem, WIDTH),
                         mv16.reshape(bp, n_mem, WIDTH), 0, *shared, tm=256, tq=128, tk=1024)

    bs, n_past = cache_a_k.shape[1], cache_a_k.shape[2]
    past = (cache_a_k[l], cache_a_v[l], cache_b_k[l], cache_b_v[l], cache_b_kidx[l])
    ls = x_sample.shape[1]
    y_s, rows_s = _layer(x_sample, past, cache_mem_k[l].reshape(bs, n_mem, WIDTH).astype(BF16),
                         cache_mem_v[l].reshape(bs, n_mem, WIDTH).astype(BF16), n_past, *shared,
                         tm=ls, tq=128, tk=384)

    mem_shape = (1, bp, n_mem, M_HEADS, M_DH)
    return (y_p, y_s, *[r[None] for r in rows_p], mk32.reshape(mem_shape),
            mv32.reshape(mem_shape), *[r[None] for r in rows_s])
```

```python
import functools
import math

import jax
import jax.numpy as jnp
import numpy as np
from jax import lax
from jax.experimental import pallas as pl
from jax.experimental.pallas import tpu as pltpu

F32 = jnp.float32
BF16 = jnp.bfloat16
I32 = jnp.int32

EPS = 1e-6
CHUNK = 64
CHUNK_SHIFT = 6
A_HEADS, A_DH = 4, 64
B_HEADS, B_DH = 8, 64
IDX_HEADS, IDX_DH = 8, 64
M_HEADS, M_DH = 4, 128
N_BRANCH = 3
TOPK_MAX = 256
WIDTH = 512
LANES = 128
PAIR = 128
ONES_ROWS = 16
V_ROWS = PAIR + ONES_ROWS
LOG2E = 1.4426950408889634
NEG = -1e30
INT_MAX = 2147483647
KEY_NEG_INF = -2139095041
VMEM_LIMIT = 60 * 1024 * 1024
BISECT_EVERY = 8
MAX_SEARCH_STEPS = BISECT_EVERY * 34
SUB_KEYS = 256
MAX_EXP2 = 64.0

C_QA, C_KA, C_VA, C_GA = 0, 512, 1024, 1536
C_QB, C_KB, C_VB, C_GB = 2048, 2560, 3072, 3584
C_QI, C_KW, C_QM, C_GM, C_GATES = 4096, 4608, 4736, 5248, 5760
KW_PAD = LANES - IDX_DH - IDX_HEADS
W_COLS = C_GATES + N_BRANCH * 1024
IN_KW_END = 4680


def _bf16_pieces(x, n):
    out, rest = [], float(x)
    for _ in range(n):
        p = float(np.asarray(rest, np.float32).astype(BF16).astype(np.float32))
        out.append(p)
        rest -= p
    return out


LOG2E_PIECES = _bf16_pieces(LOG2E, 3)
POS_SPLIT = 256


def _resident(block_shape, index_map):
    return pl.BlockSpec(block_shape, index_map, pipeline_mode=pl.Buffered(1))


def _alibi_slope(h, n):
    return 2.0 ** (-8.0 * (h + 1) / n)


def _proj_kernel(x_ref, g_ref, w_ref, qa, ka16, va16, ga, qb, kb16, vb16, gb, qi, qm, gm,
                 gates, ka, va, kb, vb, kw):
    x = x_ref[...]
    hn = (x * lax.rsqrt(jnp.mean(x * x, axis=-1, keepdims=True) + EPS) * g_ref[...]).astype(BF16)

    def mm(c0, width):
        return jnp.dot(hn, w_ref[:, c0:c0 + width], preferred_element_type=F32)

    qa[...] = (mm(C_QA, WIDTH) * (A_DH ** -0.5 * LOG2E)).astype(BF16)
    qb[...] = (mm(C_QB, WIDTH) * (B_DH ** -0.5 * LOG2E)).astype(BF16)
    qi[...] = (mm(C_QI, WIDTH) * (IDX_DH ** -0.5)).astype(BF16)
    qm[...] = (mm(C_QM, WIDTH) * (M_DH ** -0.5 * LOG2E)).astype(BF16)
    for c0, o32, o16 in ((C_KA, ka, ka16), (C_VA, va, va16), (C_KB, kb, kb16), (C_VB, vb, vb16)):
        z = mm(c0, WIDTH)
        o32[...] = z
        o16[...] = z.astype(BF16)
    for c0, o16 in ((C_GA, ga), (C_GB, gb), (C_GM, gm)):
        o16[...] = mm(c0, WIDTH).astype(BF16)
    kw[...] = mm(C_KW, LANES)
    for c in range(N_BRANCH * 1024 // WIDTH):
        gates[:, c * WIDTH:(c + 1) * WIDTH] = mm(C_GATES + c * WIDTH, WIDTH).astype(BF16)


def _proj(x2d, norm_g, w16, tm):
    n, d = x2d.shape
    row = lambda i: (i, 0)
    fixed = lambda i: (0, 0)
    widths16 = [WIDTH] * 11 + [N_BRANCH * 1024]
    widths32 = [WIDTH] * 4 + [LANES]
    out_shape = ([jax.ShapeDtypeStruct((n, w), BF16) for w in widths16]
                 + [jax.ShapeDtypeStruct((n, w), F32) for w in widths32])
    out_specs = [pl.BlockSpec((tm, w), row) for w in widths16 + widths32]
    return pl.pallas_call(
        _proj_kernel,
        grid=(n // tm,),
        in_specs=[pl.BlockSpec((tm, d), row), pl.BlockSpec((1, d), fixed),
                  _resident((d, W_COLS), fixed)],
        out_specs=out_specs,
        out_shape=out_shape,
        compiler_params=pltpu.CompilerParams(dimension_semantics=("arbitrary",),
                                             vmem_limit_bytes=VMEM_LIMIT),
        name="proj",
    )(x2d, norm_g.reshape(1, d), w16)


def _memkv_kernel(x_ref, g_ref, w_ref, k32, v32, k16, v16):
    x = x_ref[...]
    hn = (x * lax.rsqrt(jnp.mean(x * x, axis=-1, keepdims=True) + EPS) * g_ref[...]).astype(BF16)
    zk = jnp.dot(hn, w_ref[:, :WIDTH], preferred_element_type=F32)
    zv = jnp.dot(hn, w_ref[:, WIDTH:], preferred_element_type=F32)
    k32[...] = zk
    v32[...] = zv
    k16[...] = zk.astype(BF16)
    v16[...] = zv.astype(BF16)


def _memkv(mem2d, g, w16, tm):
    n, d = mem2d.shape
    row = lambda i: (i, 0)
    fixed = lambda i: (0, 0)
    return pl.pallas_call(
        _memkv_kernel,
        grid=(n // tm,),
        in_specs=[pl.BlockSpec((tm, d), row), pl.BlockSpec((1, d), fixed),
                  pl.BlockSpec((d, 2 * WIDTH), fixed)],
        out_specs=[pl.BlockSpec((tm, WIDTH), row)] * 4,
        out_shape=[jax.ShapeDtypeStruct((n, WIDTH), F32)] * 2
        + [jax.ShapeDtypeStruct((n, WIDTH), BF16)] * 2,
        compiler_params=pltpu.CompilerParams(dimension_semantics=("arbitrary",)),
        name="memkv",
    )(mem2d, g.reshape(1, d), w16)


def _visible_chunks(q0, tq, tk):
    n_vis = (lax.shift_right_logical(q0 + tq - 1, CHUNK_SHIFT) + 1) * CHUNK
    return (n_vis + tk - 1) // tk


def _alibi_key_tile(tk):
    pos = np.arange(tk)
    lo = pos % POS_SPLIT
    tile = np.zeros((tk, LANES), np.float32)
    tile[:, 0:3] = lo[:, None]
    tile[:, 3:6] = (pos - lo)[:, None]
    return jnp.asarray(tile, BF16)


def _alibi_query_rows(slopes_row, width):
    row = lax.broadcasted_iota(I32, (LANES, 1), 0)
    piece = jnp.zeros((LANES, 1), F32)
    for j, p in enumerate(LOG2E_PIECES):
        piece = jnp.where(jnp.logical_or(row == j, row == j + 3), p, piece)
    return (piece * slopes_row).astype(BF16) + jnp.zeros((LANES, width), BF16)


def _split_halves(q_t, tq):
    row = lax.broadcasted_iota(I32, (PAIR, 1), 0)
    zero = jnp.zeros_like(q_t)
    return jnp.concatenate([jnp.where(row < PAIR // 2, q_t, zero),
                            jnp.where(row >= PAIR // 2, q_t, zero)], axis=1)


def _diag_distance(off, q0, tk, tq):
    kpos = off + lax.broadcasted_iota(I32, (tk, 1), 0)
    qpos = q0 + lax.broadcasted_iota(I32, (1, tq), 1)
    return jnp.maximum(kpos - qpos, 0).astype(F32), kpos, qpos


def _online_update(s, v_aug, chunk_base, m_ref, acc_ref, idx):
    m_old = m_ref[idx]
    m_new = jnp.maximum(m_old, jnp.max(s, axis=0, keepdims=True) + chunk_base)
    p = jnp.exp2(s - (m_new - chunk_base)).astype(BF16)
    pv = jnp.dot(v_aug, p, preferred_element_type=F32)
    acc_ref[idx] = jnp.exp2(m_old - m_new) * acc_ref[idx] + pv
    m_ref[idx] = m_new


def _attend(n_groups, n_chunks, tk, width, scores, values, base, frame, m_ref, acc_ref, p_ref):
    last = n_chunks - 1
    sub = SUB_KEYS if tk % SUB_KEYS == 0 else LANES

    def chunk_off(c):
        return pl.multiple_of(c * tk, tk)

    acc_ref[...] = jnp.zeros(acc_ref.shape, F32)

    def weigh(c, diagonal, slot, tops):
        off = chunk_off(c)
        new_tops = []
        for g in range(n_groups):
            shift = frame(g) - base(g, off)
            top = tops[g]
            for j in range(tk // sub):
                s = scores(g, off, j, sub, diagonal)
                top = jnp.maximum(top, jnp.max(s, axis=0, keepdims=True) - shift)
                p_ref[slot, g, j * sub:(j + 1) * sub, :] = jnp.exp2(s - shift).astype(BF16)
            new_tops.append(top)
        return tuple(new_tops)

    def gather(c, slot):
        off = chunk_off(c)
        for g in range(n_groups):
            acc_ref[g] += jnp.dot(values(g, off, tk), p_ref[slot, g],
                                  preferred_element_type=F32)

    def step(i, slot, tops):
        gather(jnp.where(i == 0, last, i - 1), slot)
        return weigh(i, False, 1 - slot, tops)

    def two_steps(i2, tops):
        return step(2 * i2 + 1, 1, step(2 * i2, 0, tops))

    tops = tuple(jnp.full((1, width), -jnp.inf, F32) for _ in range(n_groups))
    tops = weigh(last, True, 0, tops)
    tops = lax.fori_loop(0, last // 2, two_steps, tops)
    tops = lax.cond(lax.rem(last, 2) == 1, lambda t: step(last - 1, 0, t), lambda t: t, tops)
    gather(jnp.where(last == 0, 0, last - 1), lax.rem(last, 2))
    off_scale = jnp.abs(tops[0])
    for t in tops[1:]:
        off_scale = jnp.maximum(off_scale, jnp.abs(t))

    @pl.when(jnp.logical_not(jnp.max(off_scale) <= MAX_EXP2))
    def _():
        m_ref[...] = jnp.full(m_ref.shape, NEG, F32)
        acc_ref[...] = jnp.zeros(acc_ref.shape, F32)

        def chunk(c, diagonal):
            off = chunk_off(c)
            for g in range(n_groups):
                _online_update(scores(g, off, 0, tk, diagonal), values(g, off, tk),
                               base(g, off), m_ref, acc_ref, g)

        def body(c, carry):
            chunk(c, False)
            return carry

        lax.fori_loop(0, last, body, 0)
        chunk(last, True)


def _attn_a_kernel(lq1, lk1, lq2, lk2, subg_ref, e_ref, qt_ref, k_ref, vt_ref, o_ref,
                   m_sc, acc_sc, p_sc, *, tq, tk, q_pos0, lam_init):
    q0 = q_pos0 + pl.program_id(1) * tq
    n_chunks = _visible_chunks(q0, tq, tk)
    lam = (jnp.exp(jnp.sum(lq1[...] * lk1[...], axis=-1, keepdims=True))
           - jnp.exp(jnp.sum(lq2[...] * lk2[...], axis=-1, keepdims=True)) + lam_init)
    slopes = [_alibi_slope(h, A_HEADS) for h in range(A_HEADS)]
    q_aug = []
    for h in range(A_HEADS):
        q_t = qt_ref[0, h * PAIR:(h + 1) * PAIR, :]
        q_aug.append(jnp.concatenate([_split_halves(q_t, tq),
                                      _alibi_query_rows(slopes[h], 2 * tq)], axis=0))
    ahead, kpos, qpos = _diag_distance((n_chunks - 1) * tk, q0, tk, tq)
    ahead = jnp.where(lax.shift_right_logical(kpos, CHUNK_SHIFT)
                      <= lax.shift_right_logical(qpos, CHUNK_SHIFT), ahead * (-2.0 * LOG2E), NEG)

    def scores(h, off, j, size, diagonal):
        hs = slice(h * PAIR, (h + 1) * PAIR)
        start = pl.multiple_of(off + j * size, size)
        k_aug = jnp.concatenate([k_ref[0, pl.ds(start, size), hs],
                                 e_ref[j * size:(j + 1) * size, :]], axis=1)
        s = jnp.dot(k_aug, q_aug[h], preferred_element_type=F32)
        if diagonal:
            fix = ahead[j * size:(j + 1) * size, :] * slopes[h]
            s = s + jnp.concatenate([fix, fix], axis=1)
        return s

    def values(h, start, size):
        start = pl.multiple_of(start, size)
        return vt_ref[0, h * V_ROWS:(h + 1) * V_ROWS, pl.ds(start, size)]

    def base(h, off):
        return off.astype(F32) * (slopes[h] * LOG2E)

    qpos2 = jnp.concatenate([qpos, qpos], axis=1).astype(F32)

    def frame(h):
        return qpos2 * (slopes[h] * LOG2E)

    _attend(A_HEADS, n_chunks, tk, 2 * tq, scores, values, base, frame, m_sc, acc_sc, p_sc)

    for h in range(A_HEADS):
        acc = acc_sc[h]
        o = acc[:PAIR] / acc[PAIR:PAIR + 1]
        o = o[:, :tq] - lam * o[:, tq:]
        o = o * lax.rsqrt(jnp.mean(o * o, axis=0, keepdims=True) + EPS) * subg_ref[...]
        o_ref[0, :, h * PAIR:(h + 1) * PAIR] = (o * (1.0 - lam_init)).T.astype(BF16)


def _attn_a(qa_t, ka16, va_t, lam_params, subln_g, *, tq, tk, q_pos0, lam_init):
    b, _, lq = qa_t.shape
    lkp = ka16.shape[1]
    small = lambda bi, i: (0, 0)
    return pl.pallas_call(
        functools.partial(_attn_a_kernel, tq=tq, tk=tk, q_pos0=q_pos0, lam_init=lam_init),
        grid=(b, lq // tq),
        in_specs=[pl.BlockSpec((1, A_DH), small)] * 4 + [
            pl.BlockSpec((2 * A_DH, 1), small), pl.BlockSpec((tk, LANES), small),
            pl.BlockSpec((1, WIDTH, tq), lambda bi, i: (bi, 0, i)),
            _resident((1, lkp, WIDTH), lambda bi, i: (bi, 0, 0)),
            _resident((1, A_HEADS * V_ROWS, lkp), lambda bi, i: (bi, 0, 0))],
        out_specs=pl.BlockSpec((1, tq, WIDTH), lambda bi, i: (bi, i, 0)),
        out_shape=jax.ShapeDtypeStruct((b, lq, WIDTH), BF16),
        scratch_shapes=[pltpu.VMEM((A_HEADS, 1, 2 * tq), F32),
                        pltpu.VMEM((A_HEADS, PAIR + ONES_ROWS, 2 * tq), F32),
                        pltpu.VMEM((2, A_HEADS, tk, 2 * tq), BF16)],
        compiler_params=pltpu.CompilerParams(dimension_semantics=("arbitrary", "arbitrary"),
                                             vmem_limit_bytes=VMEM_LIMIT),
        name="attn_a",
    )(*lam_params, subln_g.reshape(2 * A_DH, 1), _alibi_key_tile(tk), qa_t, ka16, va_t)


def _sum_keys(x):
    rows, tq = x.shape
    lanes_of_adds = 64
    if rows % lanes_of_adds == 0 and rows > lanes_of_adds:
        x = jnp.sum(x.reshape(rows // lanes_of_adds, lanes_of_adds, tq), axis=0)
    return jnp.sum(x, axis=0, keepdims=True)


def _count_true(hit):
    return _sum_keys(jnp.where(hit, 1.0, 0.0))


def _ordered_key(x):
    bits = pltpu.bitcast(x, I32)
    return bits ^ (lax.shift_right_arithmetic(bits, 31) & INT_MAX)


def _ordered_key_inv(k):
    return pltpu.bitcast(k ^ (lax.shift_right_arithmetic(k, 31) & INT_MAX), F32)


def _attn_b_kernel(e_ref, qbt_ref, qit_ref, wt_ref, kidx_ref, kb_ref, vbt_ref, o_ref,
                   sc, thr_sc, m_sc, acc_sc, p_sc, *, tq, tk, q_pos0, topk, idx_bits):
    q0 = q_pos0 + pl.program_id(1) * tq
    n_chunks = _visible_chunks(q0, tq, tk)
    last = n_chunks - 1
    topk_f = float(topk)
    qpos = q0 + lax.broadcasted_iota(I32, (1, tq), 1)
    qchunk = lax.shift_right_logical(qpos, CHUNK_SHIFT)
    n_valid = ((qchunk + 1) * CHUNK).astype(F32)

    def chunk_off(c):
        return pl.multiple_of(c * tk, tk)

    def key_pos(off):
        return off + lax.broadcasted_iota(I32, (tk, 1), 0)

    def visible(off):
        return lax.shift_right_logical(key_pos(off), CHUNK_SHIFT) <= qchunk

    qi_all = jnp.concatenate([qit_ref[0, h * IDX_DH:(h + 1) * IDX_DH, :]
                              for h in range(IDX_HEADS)], axis=1)
    w_rows = [wt_ref[0, h:h + 1, :] for h in range(IDX_HEADS)]

    sub = SUB_KEYS if tk % SUB_KEYS == 0 else LANES

    def merge_stats(a, b):
        return (jnp.maximum(a[0], b[0]), jnp.minimum(a[1], b[1]), a[2] + b[2], a[3] + b[3])

    def score_chunk(c, masked):
        stats = None
        for j in range(tk // sub):
            start = pl.multiple_of(c * tk + j * sub, sub)
            logits = jnp.dot(kidx_ref[0, pl.ds(start, sub), :], qi_all,
                             preferred_element_type=F32)
            score = jnp.zeros((sub, tq), F32)
            for h in range(IDX_HEADS):
                score = score + jnp.maximum(logits[:, h * tq:(h + 1) * tq], 0.0) * w_rows[h]
            score = jnp.where(score == 0.0, 0.0, score)
            lowest = score
            if masked:
                vis = (lax.shift_right_logical(start + lax.broadcasted_iota(I32, (sub, 1), 0),
                                               CHUNK_SHIFT) <= qchunk)
                score = jnp.where(vis, score, -jnp.inf)
                lowest = jnp.where(vis, score, jnp.inf)
            sc[pl.ds(start, sub), :] = score
            part = (jnp.max(score, axis=0, keepdims=True), jnp.min(lowest, axis=0, keepdims=True),
                    _count_true(score >= 0.0), _count_true(score > 0.0))
            stats = part if stats is None else merge_stats(stats, part)
        return stats

    stats = lax.fori_loop(
        0, last, lambda c, st: merge_stats(st, score_chunk(c, False)),
        (jnp.full((1, tq), -jnp.inf, F32), jnp.full((1, tq), jnp.inf, F32),
         jnp.zeros((1, tq), F32), jnp.zeros((1, tq), F32)))
    row_max, row_min, n_ge0, n_pos = merge_stats(stats, score_chunk(last, True))

    def count(pred):
        def one(c):
            off = chunk_off(c)
            return _count_true(pred(sc[pl.ds(off, tk), :], off))

        def body(i, acc):
            second = jnp.minimum(2 * i + 1, last)
            keep = jnp.where(2 * i + 1 <= last, 1.0, 0.0)
            return acc + one(2 * i) + keep * one(second)
        return lax.fori_loop(0, (n_chunks + 1) // 2, body, jnp.zeros((1, tq), F32))

    def count_ge(t):
        return count(lambda s, off: s >= t)

    def spread(cnt):
        c = jnp.clip(cnt, 0.5, n_valid - 0.5)
        return jnp.log2(c / (n_valid - c))

    target = spread(jnp.full((1, tq), topk_f, F32))
    key_lo, key_hi = _ordered_key(row_min), _ordered_key(row_max) + 1
    cnt_lo = n_valid
    t_a, f_a = row_min, spread(n_valid) - target
    t_b, f_b = row_max, spread(jnp.zeros((1, tq), F32)) - target
    zero_up = n_ge0 >= topk_f
    key_lo = jnp.where(zero_up, jnp.maximum(key_lo, 0), key_lo)
    cnt_lo = jnp.where(zero_up, n_ge0, cnt_lo)
    t_a = jnp.where(zero_up, 0.0, t_a)
    f_a = jnp.where(zero_up, spread(n_ge0) - target, f_a)
    zero_down = n_pos < topk_f
    key_hi = jnp.where(zero_down, jnp.minimum(key_hi, 1), key_hi)
    t_b = jnp.where(zero_down, 0.0, t_b)
    f_b = jnp.where(zero_down, spread(n_pos) - target, f_b)
    state0 = (jnp.int32(0), key_lo, key_hi, cnt_lo, t_a, f_a, t_b, f_b, jnp.zeros((1, tq), F32))

    def span(klo, khi):
        return khi - klo

    def finished(klo, khi, cnt_lo):
        d = span(klo, khi)
        tight = jnp.logical_or(d == 0, d == 1)
        return jnp.logical_or(jnp.logical_or(cnt_lo == topk_f, tight), n_valid <= topk_f)

    def search_cond(st):
        it, klo, khi, cnt_lo = st[:4]
        todo = jnp.where(finished(klo, khi, cnt_lo), 0.0, 1.0)
        return jnp.logical_and(it < MAX_SEARCH_STEPS, jnp.max(todo) > 0.0)

    def search_body(st):
        it, klo, khi, cnt_lo, ta, fa, tb, fb, side = st
        done = finished(klo, khi, cnt_lo)
        guess = ta + (tb - ta) * (fa / (fa - fb))
        guess = jnp.where(guess == guess, guess, ta)
        guess = jnp.clip(guess, -3e38, 3e38)
        mid = klo + lax.shift_right_logical(span(klo, khi), 1)
        cand = jnp.where(it % BISECT_EVERY == BISECT_EVERY - 1, mid, _ordered_key(guess))
        cand = jnp.minimum(jnp.maximum(cand, klo + 1), khi - 1)
        cand = jnp.where(done, klo, cand)
        t = _ordered_key_inv(cand)
        cnt = count_ge(t)
        up = jnp.logical_and(cnt >= topk_f, jnp.logical_not(done))
        down = jnp.logical_and(cnt < topk_f, jnp.logical_not(done))
        f = spread(cnt) - target
        fb = jnp.where(jnp.logical_and(up, side > 0.0), 0.5 * fb, fb)
        fa = jnp.where(jnp.logical_and(down, side < 0.0), 0.5 * fa, fa)
        return (it + 1, jnp.where(up, cand, klo), jnp.where(down, cand, khi),
                jnp.where(up, cnt, cnt_lo),
                jnp.where(up, t, ta), jnp.where(up, f, fa),
                jnp.where(down, t, tb), jnp.where(down, f, fb),
                jnp.where(up, 1.0, jnp.where(down, -1.0, side)))

    st = lax.while_loop(search_cond, search_body, state0)
    key_lo, cnt_lo = st[1], st[3]
    thr = _ordered_key_inv(key_lo)

    thr_sc[...] = jnp.full((1, tq), INT_MAX, I32)
    tied = jnp.where(jnp.logical_and(cnt_lo != topk_f, n_valid > topk_f), 1.0, 0.0)

    @pl.when(jnp.max(tied) > 0.0)
    def _():
        room = topk_f - count(lambda s, off: s > thr)
        jmax = jnp.zeros((1, tq), I32)
        for bit in range(idx_bits - 1, -1, -1):
            cand = jmax | (1 << bit)
            g = count(lambda s, off: jnp.logical_and(s == thr, key_pos(off) <= cand))
            jmax = jnp.where(g <= room, cand, jmax)
        thr_sc[...] = jmax

    jmax = thr_sc[...]

    ahead, _, _ = _diag_distance(chunk_off(last), q0, tk, tq)

    def mask_chunk(c, masked):
        off = chunk_off(c)
        s = sc[pl.ds(off, tk), :]
        sel = jnp.logical_or(s > thr, jnp.logical_and(s == thr, key_pos(off) <= jmax))
        eff = key_pos(off).astype(F32)
        if masked:
            sel = jnp.logical_and(sel, visible(off))
            eff = eff - 2.0 * ahead
        sc[pl.ds(off, tk), :] = jnp.where(sel, 0.0, NEG)
        return jnp.max(jnp.where(sel, eff, -jnp.inf), axis=0, keepdims=True)

    nearest = lax.fori_loop(0, last, lambda c, n: jnp.maximum(n, mask_chunk(c, False)),
                            jnp.full((1, tq), -jnp.inf, F32))
    nearest = jnp.maximum(nearest, mask_chunk(last, True))

    lane2 = lax.broadcasted_iota(I32, (1, 2 * tq), 1)
    n_pairs = B_HEADS // 2
    slope_rows, q_aug = [], []
    for pi in range(n_pairs):
        slope_row = jnp.where(lane2 < tq, _alibi_slope(2 * pi, B_HEADS),
                              _alibi_slope(2 * pi + 1, B_HEADS)).astype(F32)
        slope_rows.append(slope_row)
        q_t = qbt_ref[0, pi * PAIR:(pi + 1) * PAIR, :]
        q_aug.append(jnp.concatenate([_split_halves(q_t, tq),
                                      _alibi_query_rows(slope_row, 2 * tq)], axis=0))
    ahead2 = jnp.concatenate([ahead, ahead], axis=1) * (-2.0 * LOG2E)
    nearest2 = jnp.concatenate([nearest, nearest], axis=1)

    def scores(pi, off, j, size, diagonal):
        start = pl.multiple_of(off + j * size, size)
        k_aug = jnp.concatenate([kb_ref[0, pl.ds(start, size), pi * PAIR:(pi + 1) * PAIR],
                                 e_ref[j * size:(j + 1) * size, :]], axis=1)
        mask = sc[pl.ds(start, size), :]
        s = (jnp.dot(k_aug, q_aug[pi], preferred_element_type=F32)
             + jnp.concatenate([mask, mask], axis=1))
        if diagonal:
            s = s + ahead2[j * size:(j + 1) * size, :] * slope_rows[pi]
        return s

    def frame(pi):
        return nearest2 * (slope_rows[pi] * LOG2E)

    def values(pi, start, size):
        start = pl.multiple_of(start, size)
        return vbt_ref[0, pi * V_ROWS:(pi + 1) * V_ROWS, pl.ds(start, size)]

    def base(pi, off):
        return off.astype(F32) * (slope_rows[pi] * LOG2E)

    _attend(n_pairs, n_chunks, tk, 2 * tq, scores, values, base, frame, m_sc, acc_sc, p_sc)

    for pi in range(n_pairs):
        acc = acc_sc[pi]
        o = acc[:PAIR] / acc[PAIR:PAIR + 1]
        o = jnp.concatenate([o[:B_DH, :tq], o[B_DH:, tq:]], axis=0)
        o_ref[0, :, pi * PAIR:(pi + 1) * PAIR] = o.T.astype(BF16)


def _attn_b(qb_t, qi_t, w_t, kidx16, kb16, vb_t, *, tq, tk, q_pos0, topk):
    b, _, lq = qb_t.shape
    lkp = kb16.shape[1]
    q_blk = lambda rows: pl.BlockSpec((1, rows, tq), lambda bi, i: (bi, 0, i))
    per_batch = lambda bi, i: (bi, 0, 0)
    return pl.pallas_call(
        functools.partial(_attn_b_kernel, tq=tq, tk=tk, q_pos0=q_pos0, topk=topk,
                          idx_bits=max(1, (lkp - 1).bit_length())),
        grid=(b, lq // tq),
        in_specs=[pl.BlockSpec((tk, LANES), lambda bi, i: (0, 0)),
                  q_blk(WIDTH), q_blk(WIDTH), q_blk(IDX_HEADS),
                  _resident((1, lkp, IDX_DH), per_batch),
                  _resident((1, lkp, WIDTH), per_batch),
                  _resident((1, B_HEADS // 2 * V_ROWS, lkp), per_batch)],
        out_specs=pl.BlockSpec((1, tq, WIDTH), lambda bi, i: (bi, i, 0)),
        out_shape=jax.ShapeDtypeStruct((b, lq, WIDTH), BF16),
        scratch_shapes=[pltpu.VMEM((lkp, tq), F32), pltpu.VMEM((1, tq), I32),
                        pltpu.VMEM((B_HEADS // 2, 1, 2 * tq), F32),
                        pltpu.VMEM((B_HEADS // 2, PAIR + ONES_ROWS, 2 * tq), F32),
                        pltpu.VMEM((2, B_HEADS // 2, tk, 2 * tq), BF16)],
        compiler_params=pltpu.CompilerParams(dimension_semantics=("arbitrary", "arbitrary"),
                                             vmem_limit_bytes=VMEM_LIMIT),
        name="attn_b",
    )(_alibi_key_tile(tk), qb_t, qi_t, w_t, kidx16, kb16, vb_t)


def _nt_dot(a, b):
    return lax.dot_general(a, b, (((1,), (1,)), ((), ())), preferred_element_type=F32)


def _post_kernel(x_ref, oa_ref, ob_ref, qm_ref, ga_ref, gb_ref, gm_ref, gates_ref, mk_ref,
                 mv_ref, bg_ref, wa_ref, wb_ref, wm_ref, wo_ref, fg_ref, y_ref):
    d = x_ref.shape[-1]
    om = []
    for h in range(M_HEADS):
        hs = slice(h * M_DH, (h + 1) * M_DH)
        s = _nt_dot(qm_ref[0, :, hs], mk_ref[0, :, hs])
        p = jnp.exp2(s - jnp.max(s, axis=-1, keepdims=True))
        o = jnp.dot(p.astype(BF16), mv_ref[0, :, hs], preferred_element_type=F32)
        om.append(o / jnp.sum(p, axis=-1, keepdims=True))
    om = jnp.concatenate(om, axis=1)

    def branch(gate_ref, o, w_ref):
        g = gate_ref[0].astype(F32)
        return jnp.dot((g * jax.nn.sigmoid(g) * o).astype(BF16), w_ref[...],
                       preferred_element_type=F32)

    pa = branch(ga_ref, oa_ref[0].astype(F32), wa_ref)
    pb = branch(gb_ref, ob_ref[0].astype(F32), wb_ref)
    pm = branch(gm_ref, om, wm_ref)
    g = jax.nn.sigmoid(gates_ref[0].astype(F32) + bg_ref[...])
    merged = g[:, :d] * pa + g[:, d:2 * d] * pb + g[:, 2 * d:] * pm
    y = x_ref[0] + jnp.dot(merged.astype(BF16), wo_ref[...], preferred_element_type=F32)
    y_ref[0] = y * lax.rsqrt(jnp.mean(y * y, axis=-1, keepdims=True) + EPS) * fg_ref[...]


def _post(x, oa, ob, qm, ga, gb, gm, gates, mk16, mv16, b_gate, wa, wb, wm, wo, final_g, tm):
    b, lq, d = x.shape
    n_mem = mk16.shape[1]
    blk = lambda w: pl.BlockSpec((1, tm, w), lambda bi, i: (bi, i, 0))
    per_batch = pl.BlockSpec((1, n_mem, WIDTH), lambda bi, i: (bi, 0, 0))
    full = lambda r, c: pl.BlockSpec((r, c), lambda bi, i: (0, 0))
    return pl.pallas_call(
        _post_kernel,
        grid=(b, lq // tm),
        in_specs=[blk(d)] + [blk(WIDTH)] * 6 + [blk(N_BRANCH * d), per_batch, per_batch,
                  full(1, N_BRANCH * d), full(WIDTH, d), full(WIDTH, d), full(WIDTH, d),
                  full(d, d), full(1, d)],
        out_specs=blk(d),
        out_shape=jax.ShapeDtypeStruct((b, lq, d), F32),
        compiler_params=pltpu.CompilerParams(dimension_semantics=("arbitrary", "arbitrary"),
                                             vmem_limit_bytes=VMEM_LIMIT),
        name="post",
    )(x, oa, ob, qm, ga, gb, gm, gates, mk16, mv16, b_gate.reshape(1, -1), wa, wb, wm, wo,
      final_g.reshape(1, d))


def _values_on_lanes(v16):
    b, lk, _ = v16.shape
    v_t = jnp.swapaxes(v16, 1, 2).reshape(b, WIDTH // PAIR, PAIR, lk)
    ones = jnp.ones((b, WIDTH // PAIR, ONES_ROWS, lk), v16.dtype)
    return jnp.concatenate([v_t, ones], axis=2).reshape(b, WIDTH // PAIR * V_ROWS, lk)


def _pad_axis(x, size, axis):
    pad = size - x.shape[axis]
    if pad == 0:
        return x
    widths = [(0, 0)] * x.ndim
    widths[axis] = (0, pad)
    return jnp.pad(x, widths)


def _layer(x, past, mk16, mv16, q_pos0, norm_g, w16, b_gate, lam_params, lam_init, subln_g,
           wa, wb, wm, wo, final_g, *, tm, tq, tk):
    b, lq, d = x.shape
    (qa, ka16, va16, ga, qb, kb16, vb16, gb, qi, qm, gm, gates, ka, va, kb, vb, kw) = [
        t.reshape(b, lq, -1) for t in _proj(x.reshape(b * lq, d), norm_g, w16, tm)]
    kidx = kw[..., :IDX_DH]
    new_rows = (ka.reshape(b, lq, A_HEADS, 2, A_DH), va.reshape(b, lq, A_HEADS, 2 * A_DH),
                kb.reshape(b, lq, B_HEADS, B_DH), vb.reshape(b, lq, B_HEADS, B_DH), kidx)
    kidx16 = kidx.astype(BF16)
    if past is not None:
        pa_k, pa_v, pb_k, pb_v, p_kidx = past
        n_past = pa_k.shape[1]
        cat = lambda p, n: jnp.concatenate([p.reshape(b, n_past, -1).astype(BF16), n], axis=1)
        ka16, va16, kb16, vb16, kidx16 = (cat(pa_k, ka16), cat(pa_v, va16), cat(pb_k, kb16),
                                          cat(pb_v, vb16), cat(p_kidx, kidx16))
    lk = ka16.shape[1]
    topk = min(TOPK_MAX, lk // 4)
    lkp = -(-lk // tk) * tk
    ka16, va16, kb16, vb16, kidx16 = [_pad_axis(t, lkp, 1)
                                      for t in (ka16, va16, kb16, vb16, kidx16)]
    lqp = -(-lq // tq) * tq
    to_lanes = lambda t: jnp.swapaxes(_pad_axis(t, lqp, 1), 1, 2)
    qa_t, qb_t, qi_t = to_lanes(qa), to_lanes(qb), to_lanes(qi)
    w_t = to_lanes(kw[..., IDX_DH:IDX_DH + IDX_HEADS])
    va_t, vb_t = _values_on_lanes(va16), _values_on_lanes(vb16)

    oa = _attn_a(qa_t, ka16, va_t, lam_params, subln_g, tq=tq, tk=tk, q_pos0=q_pos0,
                 lam_init=lam_init)[:, :lq]
    ob = _attn_b(qb_t, qi_t, w_t, kidx16, kb16, vb_t, tq=tq, tk=tk, q_pos0=q_pos0,
                 topk=topk)[:, :lq]
    y = _post(x, oa, ob, qm, ga, gb, gm, gates, mk16, mv16, b_gate, wa, wb, wm, wo, final_g, tm)
    return y, new_rows


def kernel(x_prompt, x_sample, mem_prompt, cache_a_k, cache_a_v, cache_b_k, cache_b_v, cache_b_kidx, cache_mem_k, cache_mem_v, norm_g, w_in, b_gate, lam_q1, lam_k1, lam_q2, lam_k2, subln_g, mem_norm_g, w_mem_kv, w_br_a, w_br_b, w_br_m, w_out, final_g):
    depth, d, _ = w_in.shape
    assert depth == 1, "single-layer step only"
    l = 0
    lam_init = 0.8 - 0.6 * math.exp(-0.3 * l)
    w16 = jnp.concatenate([w_in[l][:, :IN_KW_END], jnp.zeros((d, KW_PAD), w_in.dtype),
                           w_in[l][:, IN_KW_END:]], axis=1).astype(BF16)
    lam_params = [p[l].reshape(1, A_DH) for p in (lam_q1, lam_k1, lam_q2, lam_k2)]
    shared = (norm_g[l], w16, b_gate[l], lam_params, lam_init, subln_g[l],
              w_br_a[l].astype(BF16), w_br_b[l].astype(BF16), w_br_m[l].astype(BF16),
              w_out[l].astype(BF16), final_g)

    bp, n_mem, _ = mem_prompt.shape
    mk32, mv32, mk16, mv16 = _memkv(mem_prompt.reshape(bp * n_mem, d), mem_norm_g[l],
                                    w_mem_kv[l].astype(BF16), tm=256)
    y_p, rows_p = _layer(x_prompt, None, mk16.reshape(bp, n_mem, WIDTH),
                         mv16.reshape(bp, n_mem, WIDTH), 0, *shared, tm=256, tq=128, tk=1024)

    bs, n_past = cache_a_k.shape[1], cache_a_k.shape[2]
    past = (cache_a_k[l], cache_a_v[l], cache_b_k[l], cache_b_v[l], cache_b_kidx[l])
    ls = x_sample.shape[1]
    y_s, rows_s = _layer(x_sample, past, cache_mem_k[l].reshape(bs, n_mem, WIDTH).astype(BF16),
                         cache_mem_v[l].reshape(bs, n_mem, WIDTH).astype(BF16), n_past, *shared,
                         tm=ls, tq=128, tk=384)

    mem_shape = (1, bp, n_mem, M_HEADS, M_DH)
    return (y_p, y_s, *[r[None] for r in rows_p], mk32.reshape(mem_shape),
            mv32.reshape(mem_shape), *[r[None] for r in rows_s])
```

```python
import functools
import math

import jax
import jax.numpy as jnp
import numpy as np
from jax import lax
from jax.experimental import pallas as pl
from jax.experimental.pallas import tpu as pltpu

F32 = jnp.float32
BF16 = jnp.bfloat16
I32 = jnp.int32

EPS = 1e-6
CHUNK = 64
CHUNK_SHIFT = 6
A_HEADS, A_DH = 4, 64
B_HEADS, B_DH = 8, 64
IDX_HEADS, IDX_DH = 8, 64
M_HEADS, M_DH = 4, 128
N_BRANCH = 3
TOPK_MAX = 256
WIDTH = 512
LANES = 128
PAIR = 128
ONES_ROWS = 16
V_ROWS = PAIR + ONES_ROWS
LOG2E = 1.4426950408889634
NEG = -1e30
INT_MAX = 2147483647
KEY_NEG_INF = -2139095041
VMEM_LIMIT = 60 * 1024 * 1024
BISECT_EVERY = 8
MAX_SEARCH_STEPS = BISECT_EVERY * 34
SUB_KEYS = 256
MAX_EXP2 = 64.0

C_QA, C_KA, C_VA, C_GA = 0, 512, 1024, 1536
C_QB, C_KB, C_VB, C_GB = 2048, 2560, 3072, 3584
C_QI, C_KW, C_QM, C_GM, C_GATES = 4096, 4608, 4736, 5248, 5760
KW_PAD = LANES - IDX_DH - IDX_HEADS
W_COLS = C_GATES + N_BRANCH * 1024
IN_KW_END = 4680


def _bf16_pieces(x, n):
    out, rest = [], float(x)
    for _ in range(n):
        p = float(np.asarray(rest, np.float32).astype(BF16).astype(np.float32))
        out.append(p)
        rest -= p
    return out


LOG2E_PIECES = _bf16_pieces(LOG2E, 3)
POS_SPLIT = 256


def _resident(block_shape, index_map):
    return pl.BlockSpec(block_shape, index_map, pipeline_mode=pl.Buffered(1))


def _alibi_slope(h, n):
    return 2.0 ** (-8.0 * (h + 1) / n)


def _proj_kernel(x_ref, g_ref, w_ref, qa, ka16, va16, ga, qb, kb16, vb16, gb, qi, qm, gm,
                 gates, ka, va, kb, vb, kw):
    x = x_ref[...]
    hn = (x * lax.rsqrt(jnp.mean(x * x, axis=-1, keepdims=True) + EPS) * g_ref[...]).astype(BF16)

    def mm(c0, width):
        return jnp.dot(hn, w_ref[:, c0:c0 + width], preferred_element_type=F32)

    qa[...] = (mm(C_QA, WIDTH) * (A_DH ** -0.5 * LOG2E)).astype(BF16)
    qb[...] = (mm(C_QB, WIDTH) * (B_DH ** -0.5 * LOG2E)).astype(BF16)
    qi[...] = (mm(C_QI, WIDTH) * (IDX_DH ** -0.5)).astype(BF16)
    qm[...] = (mm(C_QM, WIDTH) * (M_DH ** -0.5 * LOG2E)).astype(BF16)
    for c0, o32, o16 in ((C_KA, ka, ka16), (C_VA, va, va16), (C_KB, kb, kb16), (C_VB, vb, vb16)):
        z = mm(c0, WIDTH)
        o32[...] = z
        o16[...] = z.astype(BF16)
    for c0, o16 in ((C_GA, ga), (C_GB, gb), (C_GM, gm)):
        o16[...] = mm(c0, WIDTH).astype(BF16)
    kw[...] = mm(C_KW, LANES)
    for c in range(N_BRANCH * 1024 // WIDTH):
        gates[:, c * WIDTH:(c + 1) * WIDTH] = mm(C_GATES + c * WIDTH, WIDTH).astype(BF16)


def _proj(x2d, norm_g, w16, tm):
    n, d = x2d.shape
    row = lambda i: (i, 0)
    fixed = lambda i: (0, 0)
    widths16 = [WIDTH] * 11 + [N_BRANCH * 1024]
    widths32 = [WIDTH] * 4 + [LANES]
    out_shape = ([jax.ShapeDtypeStruct((n, w), BF16) for w in widths16]
                 + [jax.ShapeDtypeStruct((n, w), F32) for w in widths32])
    out_specs = [pl.BlockSpec((tm, w), row) for w in widths16 + widths32]
    return pl.pallas_call(
        _proj_kernel,
        grid=(n // tm,),
        in_specs=[pl.BlockSpec((tm, d), row), pl.BlockSpec((1, d), fixed),
                  _resident((d, W_COLS), fixed)],
        out_specs=out_specs,
        out_shape=out_shape,
        compiler_params=pltpu.CompilerParams(dimension_semantics=("arbitrary",),
                                             vmem_limit_bytes=VMEM_LIMIT),
        name="proj",
    )(x2d, norm_g.reshape(1, d), w16)


def _memkv_kernel(x_ref, g_ref, w_ref, k32, v32, k16, v16):
    x = x_ref[...]
    hn = (x * lax.rsqrt(jnp.mean(x * x, axis=-1, keepdims=True) + EPS) * g_ref[...]).astype(BF16)
    zk = jnp.dot(hn, w_ref[:, :WIDTH], preferred_element_type=F32)
    zv = jnp.dot(hn, w_ref[:, WIDTH:], preferred_element_type=F32)
    k32[...] = zk
    v32[...] = zv
    k16[...] = zk.astype(BF16)
    v16[...] = zv.astype(BF16)


def _memkv(mem2d, g, w16, tm):
    n, d = mem2d.shape
    row = lambda i: (i, 0)
    fixed = lambda i: (0, 0)
    return pl.pallas_call(
        _memkv_kernel,
        grid=(n // tm,),
        in_specs=[pl.BlockSpec((tm, d), row), pl.BlockSpec((1, d), fixed),
                  pl.BlockSpec((d, 2 * WIDTH), fixed)],
        out_specs=[pl.BlockSpec((tm, WIDTH), row)] * 4,
        out_shape=[jax.ShapeDtypeStruct((n, WIDTH), F32)] * 2
        + [jax.ShapeDtypeStruct((n, WIDTH), BF16)] * 2,
        compiler_params=pltpu.CompilerParams(dimension_semantics=("arbitrary",)),
        name="memkv",
    )(mem2d, g.reshape(1, d), w16)


def _visible_chunks(q0, tq, tk):
    n_vis = (lax.shift_right_logical(q0 + tq - 1, CHUNK_SHIFT) + 1) * CHUNK
    return (n_vis + tk - 1) // tk


def _alibi_key_tile(tk):
    pos = np.arange(tk)
    lo = pos % POS_SPLIT
    tile = np.zeros((tk, LANES), np.float32)
    tile[:, 0:3] = lo[:, None]
    tile[:, 3:6] = (pos - lo)[:, None]
    return jnp.asarray(tile, BF16)


def _alibi_query_rows(slopes_row, width):
    row = lax.broadcasted_iota(I32, (LANES, 1), 0)
    piece = jnp.zeros((LANES, 1), F32)
    for j, p in enumerate(LOG2E_PIECES):
        piece = jnp.where(jnp.logical_or(row == j, row == j + 3), p, piece)
    return (piece * slopes_row).astype(BF16) + jnp.zeros((LANES, width), BF16)


def _split_halves(q_t, tq):
    row = lax.broadcasted_iota(I32, (PAIR, 1), 0)
    zero = jnp.zeros_like(q_t)
    return jnp.concatenate([jnp.where(row < PAIR // 2, q_t, zero),
                            jnp.where(row >= PAIR // 2, q_t, zero)], axis=1)


def _diag_distance(off, q0, tk, tq):
    kpos = off + lax.broadcasted_iota(I32, (tk, 1), 0)
    qpos = q0 + lax.broadcasted_iota(I32, (1, tq), 1)
    return jnp.maximum(kpos - qpos, 0).astype(F32), kpos, qpos


def _online_update(s, v_aug, chunk_base, m_ref, acc_ref, idx):
    m_old = m_ref[idx]
    m_new = jnp.maximum(m_old, jnp.max(s, axis=0, keepdims=True) + chunk_base)
    p = jnp.exp2(s - (m_new - chunk_base)).astype(BF16)
    pv = jnp.dot(v_aug, p, preferred_element_type=F32)
    acc_ref[idx] = jnp.exp2(m_old - m_new) * acc_ref[idx] + pv
    m_ref[idx] = m_new


def _attend(n_groups, n_chunks, tk, width, scores, values, base, frame, m_ref, acc_ref, p_ref):
    last = n_chunks - 1
    sub = SUB_KEYS if tk % SUB_KEYS == 0 else LANES

    def chunk_off(c):
        return pl.multiple_of(c * tk, tk)

    acc_ref[...] = jnp.zeros(acc_ref.shape, F32)

    def weigh(c, diagonal, slot, tops):
        off = chunk_off(c)
        new_tops = []
        for g in range(n_groups):
            shift = frame(g) - base(g, off)
            top = tops[g]
            for j in range(tk // sub):
                s = scores(g, off, j, sub, diagonal)
                top = jnp.maximum(top, jnp.max(s, axis=0, keepdims=True) - shift)
                p_ref[slot, g, j * sub:(j + 1) * sub, :] = jnp.exp2(s - shift).astype(BF16)
            new_tops.append(top)
        return tuple(new_tops)

    def gather(c, slot):
        off = chunk_off(c)
        for g in range(n_groups):
            acc_ref[g] += jnp.dot(values(g, off, tk), p_ref[slot, g],
                                  preferred_element_type=F32)

    def step(i, slot, tops):
        gather(jnp.where(i == 0, last, i - 1), slot)
        return weigh(i, False, 1 - slot, tops)

    def two_steps(i2, tops):
        return step(2 * i2 + 1, 1, step(2 * i2, 0, tops))

    tops = tuple(jnp.full((1, width), -jnp.inf, F32) for _ in range(n_groups))
    tops = weigh(last, True, 0, tops)
    tops = lax.fori_loop(0, last // 2, two_steps, tops)
    tops = lax.cond(lax.rem(last, 2) == 1, lambda t: step(last - 1, 0, t), lambda t: t, tops)
    gather(jnp.where(last == 0, 0, last - 1), lax.rem(last, 2))
    off_scale = jnp.abs(tops[0])
    for t in tops[1:]:
        off_scale = jnp.maximum(off_scale, jnp.abs(t))

    @pl.when(jnp.logical_not(jnp.max(off_scale) <= MAX_EXP2))
    def _():
        m_ref[...] = jnp.full(m_ref.shape, NEG, F32)
        acc_ref[...] = jnp.zeros(acc_ref.shape, F32)

        def chunk(c, diagonal):
            off = chunk_off(c)
            for g in range(n_groups):
                _online_update(scores(g, off, 0, tk, diagonal), values(g, off, tk),
                               base(g, off), m_ref, acc_ref, g)

        def body(c, carry):
            chunk(c, False)
            return carry

        lax.fori_loop(0, last, body, 0)
        chunk(last, True)


def _attn_a_kernel(lq1, lk1, lq2, lk2, subg_ref, e_ref, qt_ref, k_ref, vt_ref, o_ref,
                   m_sc, acc_sc, p_sc, *, tq, tk, q_pos0, lam_init):
    q0 = q_pos0 + pl.program_id(1) * tq
    n_chunks = _visible_chunks(q0, tq, tk)
    lam = (jnp.exp(jnp.sum(lq1[...] * lk1[...], axis=-1, keepdims=True))
           - jnp.exp(jnp.sum(lq2[...] * lk2[...], axis=-1, keepdims=True)) + lam_init)
    slopes = [_alibi_slope(h, A_HEADS) for h in range(A_HEADS)]
    q_aug = []
    for h in range(A_HEADS):
        q_t = qt_ref[0, h * PAIR:(h + 1) * PAIR, :]
        q_aug.append(jnp.concatenate([_split_halves(q_t, tq),
                                      _alibi_query_rows(slopes[h], 2 * tq)], axis=0))
    ahead, kpos, qpos = _diag_distance((n_chunks - 1) * tk, q0, tk, tq)
    ahead = jnp.where(lax.shift_right_logical(kpos, CHUNK_SHIFT)
                      <= lax.shift_right_logical(qpos, CHUNK_SHIFT), ahead * (-2.0 * LOG2E), NEG)

    def scores(h, off, j, size, diagonal):
        hs = slice(h * PAIR, (h + 1) * PAIR)
        start = pl.multiple_of(off + j * size, size)
        k_aug = jnp.concatenate([k_ref[0, pl.ds(start, size), hs],
                                 e_ref[j * size:(j + 1) * size, :]], axis=1)
        s = jnp.dot(k_aug, q_aug[h], preferred_element_type=F32)
        if diagonal:
            fix = ahead[j * size:(j + 1) * size, :] * slopes[h]
            s = s + jnp.concatenate([fix, fix], axis=1)
        return s

    def values(h, start, size):
        start = pl.multiple_of(start, size)
        return vt_ref[0, h * V_ROWS:(h + 1) * V_ROWS, pl.ds(start, size)]

    def base(h, off):
        return off.astype(F32) * (slopes[h] * LOG2E)

    qpos2 = jnp.concatenate([qpos, qpos], axis=1).astype(F32)

    def frame(h):
        return qpos2 * (slopes[h] * LOG2E)

    _attend(A_HEADS, n_chunks, tk, 2 * tq, scores, values, base, frame, m_sc, acc_sc, p_sc)

    for h in range(A_HEADS):
        acc = acc_sc[h]
        o = acc[:PAIR] / acc[PAIR:PAIR + 1]
        o = o[:, :tq] - lam * o[:, tq:]
        o = o * lax.rsqrt(jnp.mean(o * o, axis=0, keepdims=True) + EPS) * subg_ref[...]
        o_ref[0, :, h * PAIR:(h + 1) * PAIR] = (o * (1.0 - lam_init)).T.astype(BF16)


def _attn_a(qa_t, ka16, va_t, lam_params, subln_g, *, tq, tk, q_pos0, lam_init):
    b, _, lq = qa_t.shape
    lkp = ka16.shape[1]
    small = lambda bi, i: (0, 0)
    return pl.pallas_call(
        functools.partial(_attn_a_kernel, tq=tq, tk=tk, q_pos0=q_pos0, lam_init=lam_init),
        grid=(b, lq // tq),
        in_specs=[pl.BlockSpec((1, A_DH), small)] * 4 + [
            pl.BlockSpec((2 * A_DH, 1), small), pl.BlockSpec((tk, LANES), small),
            pl.BlockSpec((1, WIDTH, tq), lambda bi, i: (bi, 0, i)),
            _resident((1, lkp, WIDTH), lambda bi, i: (bi, 0, 0)),
            _resident((1, A_HEADS * V_ROWS, lkp), lambda bi, i: (bi, 0, 0))],
        out_specs=pl.BlockSpec((1, tq, WIDTH), lambda bi, i: (bi, i, 0)),
        out_shape=jax.ShapeDtypeStruct((b, lq, WIDTH), BF16),
        scratch_shapes=[pltpu.VMEM((A_HEADS, 1, 2 * tq), F32),
                        pltpu.VMEM((A_HEADS, PAIR + ONES_ROWS, 2 * tq), F32),
                        pltpu.VMEM((2, A_HEADS, tk, 2 * tq), BF16)],
        compiler_params=pltpu.CompilerParams(dimension_semantics=("arbitrary", "arbitrary"),
                                             vmem_limit_bytes=VMEM_LIMIT),
        name="attn_a",
    )(*lam_params, subln_g.reshape(2 * A_DH, 1), _alibi_key_tile(tk), qa_t, ka16, va_t)


def _sum_keys(x):
    rows, tq = x.shape
    lanes_of_adds = 64
    if rows % lanes_of_adds == 0 and rows > lanes_of_adds:
        x = jnp.sum(x.reshape(rows // lanes_of_adds, lanes_of_adds, tq), axis=0)
    return jnp.sum(x, axis=0, keepdims=True)


def _count_true(hit):
    return _sum_keys(jnp.where(hit, 1.0, 0.0))


def _ordered_key(x):
    bits = pltpu.bitcast(x, I32)
    return bits ^ (lax.shift_right_arithmetic(bits, 31) & INT_MAX)


def _ordered_key_inv(k):
    return pltpu.bitcast(k ^ (lax.shift_right_arithmetic(k, 31) & INT_MAX), F32)


def _attn_b_kernel(e_ref, qbt_ref, qit_ref, wt_ref, kidx_ref, kb_ref, vbt_ref, o_ref,
                   sc, thr_sc, m_sc, acc_sc, p_sc, *, tq, tk, q_pos0, topk, idx_bits):
    q0 = q_pos0 + pl.program_id(1) * tq
    n_chunks = _visible_chunks(q0, tq, tk)
    last = n_chunks - 1
    topk_f = float(topk)
    qpos = q0 + lax.broadcasted_iota(I32, (1, tq), 1)
    qchunk = lax.shift_right_logical(qpos, CHUNK_SHIFT)
    n_valid = ((qchunk + 1) * CHUNK).astype(F32)

    def chunk_off(c):
        return pl.multiple_of(c * tk, tk)

    def key_pos(off):
        return off + lax.broadcasted_iota(I32, (tk, 1), 0)

    def visible(off):
        return lax.shift_right_logical(key_pos(off), CHUNK_SHIFT) <= qchunk

    qi_all = jnp.concatenate([qit_ref[0, h * IDX_DH:(h + 1) * IDX_DH, :]
                              for h in range(IDX_HEADS)], axis=1)
    w_rows = [wt_ref[0, h:h + 1, :] for h in range(IDX_HEADS)]

    sub = SUB_KEYS if tk % SUB_KEYS == 0 else LANES

    def merge_stats(a, b):
        return (jnp.maximum(a[0], b[0]), jnp.minimum(a[1], b[1]), a[2] + b[2], a[3] + b[3])

    def score_chunk(c, masked):
        stats = None
        for j in range(tk // sub):
            start = pl.multiple_of(c * tk + j * sub, sub)
            logits = jnp.dot(kidx_ref[0, pl.ds(start, sub), :], qi_all,
                             preferred_element_type=F32)
            score = jnp.zeros((sub, tq), F32)
            for h in range(IDX_HEADS):
                score = score + jnp.maximum(logits[:, h * tq:(h + 1) * tq], 0.0) * w_rows[h]
            lowest = score
            if masked:
                vis = (lax.shift_right_logical(start + lax.broadcasted_iota(I32, (sub, 1), 0),
                                               CHUNK_SHIFT) <= qchunk)
                score = jnp.where(vis, score, -jnp.inf)
                lowest = jnp.where(vis, score, jnp.inf)
            sc[pl.ds(start, sub), :] = score
            part = (jnp.max(score, axis=0, keepdims=True), jnp.min(lowest, axis=0, keepdims=True),
                    _count_true(score >= 0.0), _count_true(score > 0.0))
            stats = part if stats is None else merge_stats(stats, part)
        return stats

    stats = lax.fori_loop(
        0, last, lambda c, st: merge_stats(st, score_chunk(c, False)),
        (jnp.full((1, tq), -jnp.inf, F32), jnp.full((1, tq), jnp.inf, F32),
         jnp.zeros((1, tq), F32), jnp.zeros((1, tq), F32)))
    row_max, row_min, n_ge0, n_pos = merge_stats(stats, score_chunk(last, True))

    def count(pred):
        def one(c):
            off = chunk_off(c)
            return _count_true(pred(sc[pl.ds(off, tk), :], off))

        def body(i, acc):
            second = jnp.minimum(2 * i + 1, last)
            keep = jnp.where(2 * i + 1 <= last, 1.0, 0.0)
            return acc + one(2 * i) + keep * one(second)
        return lax.fori_loop(0, (n_chunks + 1) // 2, body, jnp.zeros((1, tq), F32))

    def count_ge(t):
        return count(lambda s, off: s >= t)

    def spread(cnt):
        c = jnp.clip(cnt, 0.5, n_valid - 0.5)
        return jnp.log2(c / (n_valid - c))

    target = spread(jnp.full((1, tq), topk_f, F32))
    key_lo, key_hi = _ordered_key(row_min), _ordered_key(row_max) + 1
    cnt_lo = n_valid
    t_a, f_a = row_min, spread(n_valid) - target
    t_b, f_b = row_max, spread(jnp.zeros((1, tq), F32)) - target
    zero_up = n_ge0 >= topk_f
    key_lo = jnp.where(zero_up, jnp.maximum(key_lo, 0), key_lo)
    cnt_lo = jnp.where(zero_up, n_ge0, cnt_lo)
    t_a = jnp.where(zero_up, 0.0, t_a)
    f_a = jnp.where(zero_up, spread(n_ge0) - target, f_a)
    zero_down = n_pos < topk_f
    key_hi = jnp.where(zero_down, jnp.minimum(key_hi, 1), key_hi)
    t_b = jnp.where(zero_down, 0.0, t_b)
    f_b = jnp.where(zero_down, spread(n_pos) - target, f_b)
    state0 = (jnp.int32(0), key_lo, key_hi, cnt_lo, t_a, f_a, t_b, f_b, jnp.zeros((1, tq), F32))

    def span(klo, khi):
        return khi - klo

    def finished(klo, khi, cnt_lo):
        d = span(klo, khi)
        tight = jnp.logical_or(d == 0, d == 1)
        return jnp.logical_or(jnp.logical_or(cnt_lo == topk_f, tight), n_valid <= topk_f)

    def search_cond(st):
        it, klo, khi, cnt_lo = st[:4]
        todo = jnp.where(finished(klo, khi, cnt_lo), 0.0, 1.0)
        return jnp.logical_and(it < MAX_SEARCH_STEPS, jnp.max(todo) > 0.0)

    def search_body(st):
        it, klo, khi, cnt_lo, ta, fa, tb, fb, side = st
        done = finished(klo, khi, cnt_lo)
        guess = ta + (tb - ta) * (fa / (fa - fb))
        guess = jnp.where(guess == guess, guess, ta)
        guess = jnp.clip(guess, -3e38, 3e38)
        mid = klo + lax.shift_right_logical(span(klo, khi), 1)
        cand = jnp.where(it % BISECT_EVERY == BISECT_EVERY - 1, mid, _ordered_key(guess))
        cand = jnp.minimum(jnp.maximum(cand, klo + 1), khi - 1)
        cand = jnp.where(done, klo, cand)
        t = _ordered_key_inv(cand)
        cnt = count_ge(t)
        up = jnp.logical_and(cnt >= topk_f, jnp.logical_not(done))
        down = jnp.logical_and(cnt < topk_f, jnp.logical_not(done))
        f = spread(cnt) - target
        fb = jnp.where(jnp.logical_and(up, side > 0.0), 0.5 * fb, fb)
        fa = jnp.where(jnp.logical_and(down, side < 0.0), 0.5 * fa, fa)
        return (it + 1, jnp.where(up, cand, klo), jnp.where(down, cand, khi),
                jnp.where(up, cnt, cnt_lo),
                jnp.where(up, t, ta), jnp.where(up, f, fa),
                jnp.where(down, t, tb), jnp.where(down, f, fb),
                jnp.where(up, 1.0, jnp.where(down, -1.0, side)))

    st = lax.while_loop(search_cond, lambda st: search_body(search_body(st)), state0)
    key_lo, cnt_lo = st[1], st[3]
    thr = _ordered_key_inv(key_lo)

    thr_sc[...] = jnp.full((1, tq), INT_MAX, I32)
    tied = jnp.where(jnp.logical_and(cnt_lo != topk_f, n_valid > topk_f), 1.0, 0.0)

    @pl.when(jnp.max(tied) > 0.0)
    def _():
        room = topk_f - count(lambda s, off: s > thr)
        jmax = jnp.zeros((1, tq), I32)
        for bit in range(idx_bits - 1, -1, -1):
            cand = jmax | (1 << bit)
            g = count(lambda s, off: jnp.logical_and(s == thr, key_pos(off) <= cand))
            jmax = jnp.where(g <= room, cand, jmax)
        thr_sc[...] = jmax

    jmax = thr_sc[...]

    ahead, _, _ = _diag_distance(chunk_off(last), q0, tk, tq)

    def mask_chunk(c, masked):
        off = chunk_off(c)
        s = sc[pl.ds(off, tk), :]
        sel = jnp.logical_or(s > thr, jnp.logical_and(s == thr, key_pos(off) <= jmax))
        eff = key_pos(off).astype(F32)
        if masked:
            sel = jnp.logical_and(sel, visible(off))
            eff = eff - 2.0 * ahead
        sc[pl.ds(off, tk), :] = jnp.where(sel, 0.0, NEG)
        return jnp.max(jnp.where(sel, eff, -jnp.inf), axis=0, keepdims=True)

    nearest = lax.fori_loop(0, last, lambda c, n: jnp.maximum(n, mask_chunk(c, False)),
                            jnp.full((1, tq), -jnp.inf, F32))
    nearest = jnp.maximum(nearest, mask_chunk(last, True))

    lane2 = lax.broadcasted_iota(I32, (1, 2 * tq), 1)
    n_pairs = B_HEADS // 2
    slope_rows, q_aug = [], []
    for pi in range(n_pairs):
        slope_row = jnp.where(lane2 < tq, _alibi_slope(2 * pi, B_HEADS),
                              _alibi_slope(2 * pi + 1, B_HEADS)).astype(F32)
        slope_rows.append(slope_row)
        q_t = qbt_ref[0, pi * PAIR:(pi + 1) * PAIR, :]
        q_aug.append(jnp.concatenate([_split_halves(q_t, tq),
                                      _alibi_query_rows(slope_row, 2 * tq)], axis=0))
    ahead2 = jnp.concatenate([ahead, ahead], axis=1) * (-2.0 * LOG2E)
    nearest2 = jnp.concatenate([nearest, nearest], axis=1)

    def scores(pi, off, j, size, diagonal):
        start = pl.multiple_of(off + j * size, size)
        k_aug = jnp.concatenate([kb_ref[0, pl.ds(start, size), pi * PAIR:(pi + 1) * PAIR],
                                 e_ref[j * size:(j + 1) * size, :]], axis=1)
        mask = sc[pl.ds(start, size), :]
        s = (jnp.dot(k_aug, q_aug[pi], preferred_element_type=F32)
             + jnp.concatenate([mask, mask], axis=1))
        if diagonal:
            s = s + ahead2[j * size:(j + 1) * size, :] * slope_rows[pi]
        return s

    def frame(pi):
        return nearest2 * (slope_rows[pi] * LOG2E)

    def values(pi, start, size):
        start = pl.multiple_of(start, size)
        return vbt_ref[0, pi * V_ROWS:(pi + 1) * V_ROWS, pl.ds(start, size)]

    def base(pi, off):
        return off.astype(F32) * (slope_rows[pi] * LOG2E)

    _attend(n_pairs, n_chunks, tk, 2 * tq, scores, values, base, frame, m_sc, acc_sc, p_sc)

    for pi in range(n_pairs):
        acc = acc_sc[pi]
        o = acc[:PAIR] / acc[PAIR:PAIR + 1]
        o = jnp.concatenate([o[:B_DH, :tq], o[B_DH:, tq:]], axis=0)
        o_ref[0, :, pi * PAIR:(pi + 1) * PAIR] = o.T.astype(BF16)


def _attn_b(qb_t, qi_t, w_t, kidx16, kb16, vb_t, *, tq, tk, q_pos0, topk):
    b, _, lq = qb_t.shape
    lkp = kb16.shape[1]
    q_blk = lambda rows: pl.BlockSpec((1, rows, tq), lambda bi, i: (bi, 0, i))
    per_batch = lambda bi, i: (bi, 0, 0)
    return pl.pallas_call(
        functools.partial(_attn_b_kernel, tq=tq, tk=tk, q_pos0=q_pos0, topk=topk,
                          idx_bits=max(1, (lkp - 1).bit_length())),
        grid=(b, lq // tq),
        in_specs=[pl.BlockSpec((tk, LANES), lambda bi, i: (0, 0)),
                  q_blk(WIDTH), q_blk(WIDTH), q_blk(IDX_HEADS),
                  _resident((1, lkp, IDX_DH), per_batch),
                  _resident((1, lkp, WIDTH), per_batch),
                  _resident((1, B_HEADS // 2 * V_ROWS, lkp), per_batch)],
        out_specs=pl.BlockSpec((1, tq, WIDTH), lambda bi, i: (bi, i, 0)),
        out_shape=jax.ShapeDtypeStruct((b, lq, WIDTH), BF16),
        scratch_shapes=[pltpu.VMEM((lkp, tq), F32), pltpu.VMEM((1, tq), I32),
                        pltpu.VMEM((B_HEADS // 2, 1, 2 * tq), F32),
                        pltpu.VMEM((B_HEADS // 2, PAIR + ONES_ROWS, 2 * tq), F32),
                        pltpu.VMEM((2, B_HEADS // 2, tk, 2 * tq), BF16)],
        compiler_params=pltpu.CompilerParams(dimension_semantics=("arbitrary", "arbitrary"),
                                             vmem_limit_bytes=VMEM_LIMIT),
        name="attn_b",
    )(_alibi_key_tile(tk), qb_t, qi_t, w_t, kidx16, kb16, vb_t)


def _nt_dot(a, b):
    return lax.dot_general(a, b, (((1,), (1,)), ((), ())), preferred_element_type=F32)


def _post_kernel(x_ref, oa_ref, ob_ref, qm_ref, ga_ref, gb_ref, gm_ref, gates_ref, mk_ref,
                 mv_ref, bg_ref, wa_ref, wb_ref, wm_ref, wo_ref, fg_ref, y_ref):
    d = x_ref.shape[-1]
    om = []
    for h in range(M_HEADS):
        hs = slice(h * M_DH, (h + 1) * M_DH)
        s = _nt_dot(qm_ref[0, :, hs], mk_ref[0, :, hs])
        p = jnp.exp2(s - jnp.max(s, axis=-1, keepdims=True))
        o = jnp.dot(p.astype(BF16), mv_ref[0, :, hs], preferred_element_type=F32)
        om.append(o / jnp.sum(p, axis=-1, keepdims=True))
    om = jnp.concatenate(om, axis=1)

    def branch(gate_ref, o, w_ref):
        g = gate_ref[0].astype(F32)
        return jnp.dot((g * jax.nn.sigmoid(g) * o).astype(BF16), w_ref[...],
                       preferred_element_type=F32)

    pa = branch(ga_ref, oa_ref[0].astype(F32), wa_ref)
    pb = branch(gb_ref, ob_ref[0].astype(F32), wb_ref)
    pm = branch(gm_ref, om, wm_ref)
    g = jax.nn.sigmoid(gates_ref[0].astype(F32) + bg_ref[...])
    merged = g[:, :d] * pa + g[:, d:2 * d] * pb + g[:, 2 * d:] * pm
    y = x_ref[0] + jnp.dot(merged.astype(BF16), wo_ref[...], preferred_element_type=F32)
    y_ref[0] = y * lax.rsqrt(jnp.mean(y * y, axis=-1, keepdims=True) + EPS) * fg_ref[...]


def _post(x, oa, ob, qm, ga, gb, gm, gates, mk16, mv16, b_gate, wa, wb, wm, wo, final_g, tm):
    b, lq, d = x.shape
    n_mem = mk16.shape[1]
    blk = lambda w: pl.BlockSpec((1, tm, w), lambda bi, i: (bi, i, 0))
    per_batch = pl.BlockSpec((1, n_mem, WIDTH), lambda bi, i: (bi, 0, 0))
    full = lambda r, c: pl.BlockSpec((r, c), lambda bi, i: (0, 0))
    return pl.pallas_call(
        _post_kernel,
        grid=(b, lq // tm),
        in_specs=[blk(d)] + [blk(WIDTH)] * 6 + [blk(N_BRANCH * d), per_batch, per_batch,
                  full(1, N_BRANCH * d), full(WIDTH, d), full(WIDTH, d), full(WIDTH, d),
                  full(d, d), full(1, d)],
        out_specs=blk(d),
        out_shape=jax.ShapeDtypeStruct((b, lq, d), F32),
        compiler_params=pltpu.CompilerParams(dimension_semantics=("arbitrary", "arbitrary"),
                                             vmem_limit_bytes=VMEM_LIMIT),
        name="post",
    )(x, oa, ob, qm, ga, gb, gm, gates, mk16, mv16, b_gate.reshape(1, -1), wa, wb, wm, wo,
      final_g.reshape(1, d))


def _values_on_lanes(v16):
    b, lk, _ = v16.shape
    v_t = jnp.swapaxes(v16, 1, 2).reshape(b, WIDTH // PAIR, PAIR, lk)
    ones = jnp.ones((b, WIDTH // PAIR, ONES_ROWS, lk), v16.dtype)
    return jnp.concatenate([v_t, ones], axis=2).reshape(b, WIDTH // PAIR * V_ROWS, lk)


def _pad_axis(x, size, axis):
    pad = size - x.shape[axis]
    if pad == 0:
        return x
    widths = [(0, 0)] * x.ndim
    widths[axis] = (0, pad)
    return jnp.pad(x, widths)


def _layer(x, past, mk16, mv16, q_pos0, norm_g, w16, b_gate, lam_params, lam_init, subln_g,
           wa, wb, wm, wo, final_g, *, tm, tq_a, tq, tk):
    b, lq, d = x.shape
    (qa, ka16, va16, ga, qb, kb16, vb16, gb, qi, qm, gm, gates, ka, va, kb, vb, kw) = [
        t.reshape(b, lq, -1) for t in _proj(x.reshape(b * lq, d), norm_g, w16, tm)]
    kidx = kw[..., :IDX_DH]
    new_rows = (ka.reshape(b, lq, A_HEADS, 2, A_DH), va.reshape(b, lq, A_HEADS, 2 * A_DH),
                kb.reshape(b, lq, B_HEADS, B_DH), vb.reshape(b, lq, B_HEADS, B_DH), kidx)
    kidx16 = kidx.astype(BF16)
    if past is not None:
        pa_k, pa_v, pb_k, pb_v, p_kidx = past
        n_past = pa_k.shape[1]
        cat = lambda p, n: jnp.concatenate([p.reshape(b, n_past, -1).astype(BF16), n], axis=1)
        ka16, va16, kb16, vb16, kidx16 = (cat(pa_k, ka16), cat(pa_v, va16), cat(pb_k, kb16),
                                          cat(pb_v, vb16), cat(p_kidx, kidx16))
    lk = ka16.shape[1]
    topk = min(TOPK_MAX, lk // 4)
    lkp = -(-lk // tk) * tk
    ka16, va16, kb16, vb16, kidx16 = [_pad_axis(t, lkp, 1)
                                      for t in (ka16, va16, kb16, vb16, kidx16)]
    lqp = -(-lq // max(tq_a, tq)) * max(tq_a, tq)
    to_lanes = lambda t: jnp.swapaxes(_pad_axis(t, lqp, 1), 1, 2)
    qa_t, qb_t, qi_t = to_lanes(qa), to_lanes(qb), to_lanes(qi)
    w_t = to_lanes(kw[..., IDX_DH:IDX_DH + IDX_HEADS])
    va_t, vb_t = _values_on_lanes(va16), _values_on_lanes(vb16)

    oa = _attn_a(qa_t, ka16, va_t, lam_params, subln_g, tq=tq_a, tk=tk, q_pos0=q_pos0,
                 lam_init=lam_init)[:, :lq]
    ob = _attn_b(qb_t, qi_t, w_t, kidx16, kb16, vb_t, tq=tq, tk=tk, q_pos0=q_pos0,
                 topk=topk)[:, :lq]
    y = _post(x, oa, ob, qm, ga, gb, gm, gates, mk16, mv16, b_gate, wa, wb, wm, wo, final_g, tm)
    return y, new_rows


def kernel(x_prompt, x_sample, mem_prompt, cache_a_k, cache_a_v, cache_b_k, cache_b_v, cache_b_kidx, cache_mem_k, cache_mem_v, norm_g, w_in, b_gate, lam_q1, lam_k1, lam_q2, lam_k2, subln_g, mem_norm_g, w_mem_kv, w_br_a, w_br_b, w_br_m, w_out, final_g):
    depth, d, _ = w_in.shape
    assert depth == 1, "single-layer step only"
    l = 0
    lam_init = 0.8 - 0.6 * math.exp(-0.3 * l)
    w16 = jnp.concatenate([w_in[l][:, :IN_KW_END], jnp.zeros((d, KW_PAD), w_in.dtype),
                           w_in[l][:, IN_KW_END:]], axis=1).astype(BF16)
    lam_params = [p[l].reshape(1, A_DH) for p in (lam_q1, lam_k1, lam_q2, lam_k2)]
    shared = (norm_g[l], w16, b_gate[l], lam_params, lam_init, subln_g[l],
              w_br_a[l].astype(BF16), w_br_b[l].astype(BF16), w_br_m[l].astype(BF16),
              w_out[l].astype(BF16), final_g)

    bp, n_mem, _ = mem_prompt.shape
    mk32, mv32, mk16, mv16 = _memkv(mem_prompt.reshape(bp * n_mem, d), mem_norm_g[l],
                                    w_mem_kv[l].astype(BF16), tm=256)
    y_p, rows_p = _layer(x_prompt, None, mk16.reshape(bp, n_mem, WIDTH),
                         mv16.reshape(bp, n_mem, WIDTH), 0, *shared, tm=256, tq_a=256, tq=128,
                         tk=1024)

    bs, n_past = cache_a_k.shape[1], cache_a_k.shape[2]
    past = (cache_a_k[l], cache_a_v[l], cache_b_k[l], cache_b_v[l], cache_b_kidx[l])
    ls = x_sample.shape[1]
    y_s, rows_s = _layer(x_sample, past, cache_mem_k[l].reshape(bs, n_mem, WIDTH).astype(BF16),
                         cache_mem_v[l].reshape(bs, n_mem, WIDTH).astype(BF16), n_past, *shared,
                         tm=ls, tq_a=128, tq=128, tk=384)

    mem_shape = (1, bp, n_mem, M_HEADS, M_DH)
    return (y_p, y_s, *[r[None] for r in rows_p], mk32.reshape(mem_shape),
            mv32.reshape(mem_shape), *[r[None] for r in rows_s])
```

```python
import functools
import math

import jax
import jax.numpy as jnp
import numpy as np
from jax import lax
from jax.experimental import pallas as pl
from jax.experimental.pallas import tpu as pltpu

F32 = jnp.float32
BF16 = jnp.bfloat16
I32 = jnp.int32

EPS = 1e-6
CHUNK = 64
CHUNK_SHIFT = 6
A_HEADS, A_DH = 4, 64
B_HEADS, B_DH = 8, 64
IDX_HEADS, IDX_DH = 8, 64
M_HEADS, M_DH = 4, 128
N_BRANCH = 3
TOPK_MAX = 256
WIDTH = 512
LANES = 128
PAIR = 128
ONES_ROWS = 16
V_ROWS = PAIR + ONES_ROWS
LOG2E = 1.4426950408889634
NEG = -1e30
INT_MAX = 2147483647
KEY_NEG_INF = -2139095041
VMEM_LIMIT = 60 * 1024 * 1024
BISECT_EVERY = 8
MAX_SEARCH_STEPS = BISECT_EVERY * 34
SCAN_CHUNKS = 2
SUB_KEYS = 256
MAX_EXP2 = 64.0

C_QA, C_KA, C_VA, C_GA = 0, 512, 1024, 1536
C_QB, C_KB, C_VB, C_GB = 2048, 2560, 3072, 3584
C_QI, C_KW, C_QM, C_GM, C_GATES = 4096, 4608, 4736, 5248, 5760
KW_PAD = LANES - IDX_DH - IDX_HEADS
W_COLS = C_GATES + N_BRANCH * 1024
IN_KW_END = 4680


def _bf16_pieces(x, n):
    out, rest = [], float(x)
    for _ in range(n):
        p = float(np.asarray(rest, np.float32).astype(BF16).astype(np.float32))
        out.append(p)
        rest -= p
    return out


LOG2E_PIECES = _bf16_pieces(LOG2E, 3)
POS_SPLIT = 256


def _resident(block_shape, index_map):
    return pl.BlockSpec(block_shape, index_map, pipeline_mode=pl.Buffered(1))


def _alibi_slope(h, n):
    return 2.0 ** (-8.0 * (h + 1) / n)


def _proj_kernel(x_ref, g_ref, w_ref, qa, ka16, va16, ga, qb, kb16, vb16, gb, qi, qm, gm,
                 gates, ka, va, kb, vb, kw):
    x = x_ref[...]
    hn = (x * lax.rsqrt(jnp.mean(x * x, axis=-1, keepdims=True) + EPS) * g_ref[...]).astype(BF16)

    def mm(c0, width):
        return jnp.dot(hn, w_ref[:, c0:c0 + width], preferred_element_type=F32)

    qa[...] = (mm(C_QA, WIDTH) * (A_DH ** -0.5 * LOG2E)).astype(BF16)
    qb[...] = (mm(C_QB, WIDTH) * (B_DH ** -0.5 * LOG2E)).astype(BF16)
    qi[...] = (mm(C_QI, WIDTH) * (IDX_DH ** -0.5)).astype(BF16)
    qm[...] = (mm(C_QM, WIDTH) * (M_DH ** -0.5 * LOG2E)).astype(BF16)
    for c0, o32, o16 in ((C_KA, ka, ka16), (C_VA, va, va16), (C_KB, kb, kb16), (C_VB, vb, vb16)):
        z = mm(c0, WIDTH)
        o32[...] = z
        o16[...] = z.astype(BF16)
    for c0, o16 in ((C_GA, ga), (C_GB, gb), (C_GM, gm)):
        o16[...] = mm(c0, WIDTH).astype(BF16)
    kw[...] = mm(C_KW, LANES)
    for c in range(N_BRANCH * 1024 // WIDTH):
        gates[:, c * WIDTH:(c + 1) * WIDTH] = mm(C_GATES + c * WIDTH, WIDTH).astype(BF16)


def _proj(x2d, norm_g, w16, tm):
    n, d = x2d.shape
    row = lambda i: (i, 0)
    fixed = lambda i: (0, 0)
    widths16 = [WIDTH] * 11 + [N_BRANCH * 1024]
    widths32 = [WIDTH] * 4 + [LANES]
    out_shape = ([jax.ShapeDtypeStruct((n, w), BF16) for w in widths16]
                 + [jax.ShapeDtypeStruct((n, w), F32) for w in widths32])
    out_specs = [pl.BlockSpec((tm, w), row) for w in widths16 + widths32]
    return pl.pallas_call(
        _proj_kernel,
        grid=(n // tm,),
        in_specs=[pl.BlockSpec((tm, d), row), pl.BlockSpec((1, d), fixed),
                  _resident((d, W_COLS), fixed)],
        out_specs=out_specs,
        out_shape=out_shape,
        compiler_params=pltpu.CompilerParams(dimension_semantics=("arbitrary",),
                                             vmem_limit_bytes=VMEM_LIMIT),
        name="proj",
    )(x2d, norm_g.reshape(1, d), w16)


def _memkv_kernel(x_ref, g_ref, w_ref, k32, v32, k16, v16):
    x = x_ref[...]
    hn = (x * lax.rsqrt(jnp.mean(x * x, axis=-1, keepdims=True) + EPS) * g_ref[...]).astype(BF16)
    zk = jnp.dot(hn, w_ref[:, :WIDTH], preferred_element_type=F32)
    zv = jnp.dot(hn, w_ref[:, WIDTH:], preferred_element_type=F32)
    k32[...] = zk
    v32[...] = zv
    k16[...] = zk.astype(BF16)
    v16[...] = zv.astype(BF16)


def _memkv(mem2d, g, w16, tm):
    n, d = mem2d.shape
    row = lambda i: (i, 0)
    fixed = lambda i: (0, 0)
    return pl.pallas_call(
        _memkv_kernel,
        grid=(n // tm,),
        in_specs=[pl.BlockSpec((tm, d), row), pl.BlockSpec((1, d), fixed),
                  pl.BlockSpec((d, 2 * WIDTH), fixed)],
        out_specs=[pl.BlockSpec((tm, WIDTH), row)] * 4,
        out_shape=[jax.ShapeDtypeStruct((n, WIDTH), F32)] * 2
        + [jax.ShapeDtypeStruct((n, WIDTH), BF16)] * 2,
        compiler_params=pltpu.CompilerParams(dimension_semantics=("arbitrary",)),
        name="memkv",
    )(mem2d, g.reshape(1, d), w16)


def _visible_chunks(q0, tq, tk):
    n_vis = (lax.shift_right_logical(q0 + tq - 1, CHUNK_SHIFT) + 1) * CHUNK
    return (n_vis + tk - 1) // tk


def _alibi_key_tile(tk):
    pos = np.arange(tk)
    lo = pos % POS_SPLIT
    tile = np.zeros((tk, LANES), np.float32)
    tile[:, 0:3] = lo[:, None]
    tile[:, 3:6] = (pos - lo)[:, None]
    return jnp.asarray(tile, BF16)


def _alibi_query_rows(slopes_row, width):
    row = lax.broadcasted_iota(I32, (LANES, 1), 0)
    piece = jnp.zeros((LANES, 1), F32)
    for j, p in enumerate(LOG2E_PIECES):
        piece = jnp.where(jnp.logical_or(row == j, row == j + 3), p, piece)
    return (piece * slopes_row).astype(BF16) + jnp.zeros((LANES, width), BF16)


def _split_halves(q_t, tq):
    row = lax.broadcasted_iota(I32, (PAIR, 1), 0)
    zero = jnp.zeros_like(q_t)
    return jnp.concatenate([jnp.where(row < PAIR // 2, q_t, zero),
                            jnp.where(row >= PAIR // 2, q_t, zero)], axis=1)


def _diag_distance(off, q0, tk, tq):
    kpos = off + lax.broadcasted_iota(I32, (tk, 1), 0)
    qpos = q0 + lax.broadcasted_iota(I32, (1, tq), 1)
    return jnp.maximum(kpos - qpos, 0).astype(F32), kpos, qpos


def _online_update(s, v_aug, chunk_base, m_ref, acc_ref, idx):
    m_old = m_ref[idx]
    m_new = jnp.maximum(m_old, jnp.max(s, axis=0, keepdims=True) + chunk_base)
    p = jnp.exp2(s - (m_new - chunk_base)).astype(BF16)
    pv = jnp.dot(v_aug, p, preferred_element_type=F32)
    acc_ref[idx] = jnp.exp2(m_old - m_new) * acc_ref[idx] + pv
    m_ref[idx] = m_new


def _attend(n_groups, n_chunks, tk, scores, values, base, frame, m_ref, acc_ref, p_ref):
    last = n_chunks - 1
    sub = SUB_KEYS if tk % SUB_KEYS == 0 else LANES

    def chunk_off(c):
        return pl.multiple_of(c * tk, tk)

    acc_ref[...] = jnp.zeros(acc_ref.shape, F32)

    def weigh(c, diagonal, slot):
        off = chunk_off(c)
        for g in range(n_groups):
            shift = frame(g) - base(g, off)
            for j in range(tk // sub):
                s = scores(g, off, j, sub, diagonal)
                p_ref[slot, g, j * sub:(j + 1) * sub, :] = jnp.exp2(s - shift).astype(BF16)

    def gather(c, slot):
        off = chunk_off(c)
        for g in range(n_groups):
            acc_ref[g] += jnp.dot(values(g, off, tk), p_ref[slot, g],
                                  preferred_element_type=F32)

    def step(i, slot):
        gather(jnp.where(i == 0, last, i - 1), slot)
        weigh(i, False, 1 - slot)

    def two_steps(i2, carry):
        step(2 * i2, 0)
        step(2 * i2 + 1, 1)
        return carry

    weigh(last, True, 0)
    lax.fori_loop(0, last // 2, two_steps, 0)

    @pl.when(lax.rem(last, 2) == 1)
    def _():
        step(last - 1, 0)

    gather(jnp.where(last == 0, 0, last - 1), lax.rem(last, 2))

    low, high, peak = None, None, None
    for g in range(n_groups):
        acc = acc_ref[g]
        sums = acc[acc.shape[0] - ONES_ROWS:acc.shape[0] - ONES_ROWS + 1]
        mag = jnp.max(jnp.abs(acc), axis=0, keepdims=True)
        low = sums if low is None else jnp.minimum(low, sums)
        high = sums if high is None else jnp.maximum(high, sums)
        peak = mag if peak is None else jnp.maximum(peak, mag)
    trusted = jnp.logical_and(jnp.min(low) >= 2.0 ** -MAX_EXP2,
                              jnp.logical_and(jnp.max(high) <= 2.0 ** MAX_EXP2,
                                              jnp.max(peak) <= 2.0 ** (2 * MAX_EXP2 - 1)))

    @pl.when(jnp.logical_not(trusted))
    def _():
        m_ref[...] = jnp.full(m_ref.shape, NEG, F32)
        acc_ref[...] = jnp.zeros(acc_ref.shape, F32)

        def chunk(c, diagonal):
            off = chunk_off(c)
            for g in range(n_groups):
                _online_update(scores(g, off, 0, tk, diagonal), values(g, off, tk),
                               base(g, off), m_ref, acc_ref, g)

        def body(c, carry):
            chunk(c, False)
            return carry

        lax.fori_loop(0, last, body, 0)
        chunk(last, True)


def _attn_a_kernel(lq1, lk1, lq2, lk2, subg_ref, e_ref, qt_ref, k_ref, vt_ref, o_ref,
                   m_sc, acc_sc, p_sc, *, tq, tk, q_pos0, lam_init):
    q0 = q_pos0 + pl.program_id(1) * tq
    n_chunks = _visible_chunks(q0, tq, tk)
    lam = (jnp.exp(jnp.sum(lq1[...] * lk1[...], axis=-1, keepdims=True))
           - jnp.exp(jnp.sum(lq2[...] * lk2[...], axis=-1, keepdims=True)) + lam_init)
    slopes = [_alibi_slope(h, A_HEADS) for h in range(A_HEADS)]
    q_aug = []
    for h in range(A_HEADS):
        q_t = qt_ref[0, h * PAIR:(h + 1) * PAIR, :]
        q_aug.append(jnp.concatenate([_split_halves(q_t, tq),
                                      _alibi_query_rows(slopes[h], 2 * tq)], axis=0))
    ahead, kpos, qpos = _diag_distance((n_chunks - 1) * tk, q0, tk, tq)
    ahead = jnp.where(lax.shift_right_logical(kpos, CHUNK_SHIFT)
                      <= lax.shift_right_logical(qpos, CHUNK_SHIFT), ahead * (-2.0 * LOG2E), NEG)

    def scores(h, off, j, size, diagonal):
        hs = slice(h * PAIR, (h + 1) * PAIR)
        start = pl.multiple_of(off + j * size, size)
        k_aug = jnp.concatenate([k_ref[0, pl.ds(start, size), hs],
                                 e_ref[j * size:(j + 1) * size, :]], axis=1)
        s = jnp.dot(k_aug, q_aug[h], preferred_element_type=F32)
        if diagonal:
            fix = ahead[j * size:(j + 1) * size, :] * slopes[h]
            s = s + jnp.concatenate([fix, fix], axis=1)
        return s

    def values(h, start, size):
        start = pl.multiple_of(start, size)
        return vt_ref[0, h * V_ROWS:(h + 1) * V_ROWS, pl.ds(start, size)]

    def base(h, off):
        return off.astype(F32) * (slopes[h] * LOG2E)

    qpos2 = jnp.concatenate([qpos, qpos], axis=1).astype(F32)

    def frame(h):
        return qpos2 * (slopes[h] * LOG2E)

    _attend(A_HEADS, n_chunks, tk, scores, values, base, frame, m_sc, acc_sc, p_sc)

    for h in range(A_HEADS):
        acc = acc_sc[h]
        o = acc[:PAIR] / acc[PAIR:PAIR + 1]
        o = o[:, :tq] - lam * o[:, tq:]
        o = o * lax.rsqrt(jnp.mean(o * o, axis=0, keepdims=True) + EPS) * subg_ref[...]
        o_ref[0, :, h * PAIR:(h + 1) * PAIR] = (o * (1.0 - lam_init)).T.astype(BF16)


def _attn_a(qa_t, ka16, va_t, lam_params, subln_g, *, tq, tk, q_pos0, lam_init):
    b, _, lq = qa_t.shape
    lkp = ka16.shape[1]
    small = lambda bi, i: (0, 0)
    return pl.pallas_call(
        functools.partial(_attn_a_kernel, tq=tq, tk=tk, q_pos0=q_pos0, lam_init=lam_init),
        grid=(b, lq // tq),
        in_specs=[pl.BlockSpec((1, A_DH), small)] * 4 + [
            pl.BlockSpec((2 * A_DH, 1), small), pl.BlockSpec((tk, LANES), small),
            pl.BlockSpec((1, WIDTH, tq), lambda bi, i: (bi, 0, i)),
            _resident((1, lkp, WIDTH), lambda bi, i: (bi, 0, 0)),
            _resident((1, A_HEADS * V_ROWS, lkp), lambda bi, i: (bi, 0, 0))],
        out_specs=pl.BlockSpec((1, tq, WIDTH), lambda bi, i: (bi, i, 0)),
        out_shape=jax.ShapeDtypeStruct((b, lq, WIDTH), BF16),
        scratch_shapes=[pltpu.VMEM((A_HEADS, 1, 2 * tq), F32),
                        pltpu.VMEM((A_HEADS, PAIR + ONES_ROWS, 2 * tq), F32),
                        pltpu.VMEM((2, A_HEADS, tk, 2 * tq), BF16)],
        compiler_params=pltpu.CompilerParams(dimension_semantics=("arbitrary", "arbitrary"),
                                             vmem_limit_bytes=VMEM_LIMIT),
        name="attn_a",
    )(*lam_params, subln_g.reshape(2 * A_DH, 1), _alibi_key_tile(tk), qa_t, ka16, va_t)


def _sum_keys(x):
    rows, tq = x.shape
    lanes_of_adds = 64
    if rows % lanes_of_adds == 0 and rows > lanes_of_adds:
        x = jnp.sum(x.reshape(rows // lanes_of_adds, lanes_of_adds, tq), axis=0)
    return jnp.sum(x, axis=0, keepdims=True)


def _count_true(hit):
    return _sum_keys(jnp.where(hit, 1.0, 0.0))


def _ordered_key(x):
    bits = pltpu.bitcast(x, I32)
    return bits ^ (lax.shift_right_arithmetic(bits, 31) & INT_MAX)


def _ordered_key_inv(k):
    return pltpu.bitcast(k ^ (lax.shift_right_arithmetic(k, 31) & INT_MAX), F32)


def _attn_b_kernel(e_ref, qbt_ref, qit_ref, wt_ref, kidx_ref, kb_ref, vbt_ref, o_ref,
                   sc, thr_sc, m_sc, acc_sc, p_sc, *, tq, tk, q_pos0, topk, idx_bits):
    q0 = q_pos0 + pl.program_id(1) * tq
    n_chunks = _visible_chunks(q0, tq, tk)
    last = n_chunks - 1
    topk_f = float(topk)
    qpos = q0 + lax.broadcasted_iota(I32, (1, tq), 1)
    qchunk = lax.shift_right_logical(qpos, CHUNK_SHIFT)
    n_valid = ((qchunk + 1) * CHUNK).astype(F32)

    def chunk_off(c):
        return pl.multiple_of(c * tk, tk)

    def key_pos(off):
        return off + lax.broadcasted_iota(I32, (tk, 1), 0)

    def visible(off):
        return lax.shift_right_logical(key_pos(off), CHUNK_SHIFT) <= qchunk

    qi_all = jnp.concatenate([qit_ref[0, h * IDX_DH:(h + 1) * IDX_DH, :]
                              for h in range(IDX_HEADS)], axis=1)
    w_rows = [wt_ref[0, h:h + 1, :] for h in range(IDX_HEADS)]

    sub = SUB_KEYS if tk % SUB_KEYS == 0 else LANES

    def merge_stats(a, b):
        return (jnp.maximum(a[0], b[0]), jnp.minimum(a[1], b[1]), a[2] + b[2], a[3] + b[3])

    def score_chunk(c, masked):
        stats = None
        for j in range(tk // sub):
            start = pl.multiple_of(c * tk + j * sub, sub)
            logits = jnp.dot(kidx_ref[0, pl.ds(start, sub), :], qi_all,
                             preferred_element_type=F32)
            score = jnp.zeros((sub, tq), F32)
            for h in range(IDX_HEADS):
                score = score + jnp.maximum(logits[:, h * tq:(h + 1) * tq], 0.0) * w_rows[h]
            lowest = score
            if masked:
                vis = (lax.shift_right_logical(start + lax.broadcasted_iota(I32, (sub, 1), 0),
                                               CHUNK_SHIFT) <= qchunk)
                score = jnp.where(vis, score, -jnp.inf)
                lowest = jnp.where(vis, score, jnp.inf)
            sc[pl.ds(start, sub), :] = score
            part = (jnp.max(score, axis=0, keepdims=True), jnp.min(lowest, axis=0, keepdims=True),
                    _count_true(score >= 0.0), _count_true(score > 0.0))
            stats = part if stats is None else merge_stats(stats, part)
        return stats

    def score_two(i2, st):
        st = merge_stats(st, score_chunk(2 * i2, False))
        return merge_stats(st, score_chunk(2 * i2 + 1, False))

    stats = lax.fori_loop(
        0, last // 2, score_two,
        (jnp.full((1, tq), -jnp.inf, F32), jnp.full((1, tq), jnp.inf, F32),
         jnp.zeros((1, tq), F32), jnp.zeros((1, tq), F32)))
    stats = lax.cond(lax.rem(last, 2) == 1,
                     lambda st: merge_stats(st, score_chunk(last - 1, False)),
                     lambda st: st, stats)
    row_max, row_min, n_ge0, n_pos = merge_stats(stats, score_chunk(last, True))

    def count(pred):
        def one(c):
            off = chunk_off(c)
            return _count_true(pred(sc[pl.ds(off, tk), :], off))

        def body(i, acc):
            for u in range(SCAN_CHUNKS):
                c = SCAN_CHUNKS * i + u
                part = one(jnp.minimum(c, last))
                acc = acc + (part if u == 0 else jnp.where(c <= last, 1.0, 0.0) * part)
            return acc
        return lax.fori_loop(0, (n_chunks + SCAN_CHUNKS - 1) // SCAN_CHUNKS, body,
                             jnp.zeros((1, tq), F32))

    def count_ge(t):
        return count(lambda s, off: s >= t)

    def spread(cnt):
        c = jnp.clip(cnt, 0.5, n_valid - 0.5)
        return jnp.log2(c / (n_valid - c))

    target = spread(jnp.full((1, tq), topk_f, F32))
    key_lo, key_hi = _ordered_key(row_min), _ordered_key(row_max) + 1
    cnt_lo = n_valid
    t_a, f_a = row_min, spread(n_valid) - target
    t_b, f_b = row_max, spread(jnp.zeros((1, tq), F32)) - target
    zero_up = n_ge0 >= topk_f
    key_lo = jnp.where(zero_up, jnp.maximum(key_lo, 0), key_lo)
    cnt_lo = jnp.where(zero_up, n_ge0, cnt_lo)
    t_a = jnp.where(zero_up, 0.0, t_a)
    f_a = jnp.where(zero_up, spread(n_ge0) - target, f_a)
    zero_down = n_pos < topk_f
    key_hi = jnp.where(zero_down, jnp.minimum(key_hi, 1), key_hi)
    t_b = jnp.where(zero_down, 0.0, t_b)
    f_b = jnp.where(zero_down, spread(n_pos) - target, f_b)
    state0 = (jnp.int32(0), key_lo, key_hi, cnt_lo, t_a, f_a, t_b, f_b, jnp.zeros((1, tq), F32))

    def span(klo, khi):
        return khi - klo

    def finished(klo, khi, cnt_lo):
        d = span(klo, khi)
        tight = jnp.logical_or(d == 0, d == 1)
        return jnp.logical_or(jnp.logical_or(cnt_lo == topk_f, tight), n_valid <= topk_f)

    def search_cond(st):
        it, klo, khi, cnt_lo = st[:4]
        todo = jnp.where(finished(klo, khi, cnt_lo), 0.0, 1.0)
        return jnp.logical_and(it < MAX_SEARCH_STEPS, jnp.max(todo) > 0.0)

    def search_body(st):
        it, klo, khi, cnt_lo, ta, fa, tb, fb, side = st
        done = finished(klo, khi, cnt_lo)
        guess = ta + (tb - ta) * (fa / (fa - fb))
        guess = jnp.where(guess == guess, guess, ta)
        guess = jnp.clip(guess, -3e38, 3e38)
        mid = klo + lax.shift_right_logical(span(klo, khi), 1)
        cand = jnp.where(it % BISECT_EVERY == BISECT_EVERY - 1, mid, _ordered_key(guess))
        cand = jnp.minimum(jnp.maximum(cand, klo + 1), khi - 1)
        cand = jnp.where(done, klo, cand)
        t = _ordered_key_inv(cand)
        cnt = count_ge(t)
        up = jnp.logical_and(cnt >= topk_f, jnp.logical_not(done))
        down = jnp.logical_and(cnt < topk_f, jnp.logical_not(done))
        f = spread(cnt) - target
        fb = jnp.where(jnp.logical_and(up, side > 0.0), 0.5 * fb, fb)
        fa = jnp.where(jnp.logical_and(down, side < 0.0), 0.5 * fa, fa)
        return (it + 1, jnp.where(up, cand, klo), jnp.where(down, cand, khi),
                jnp.where(up, cnt, cnt_lo),
                jnp.where(up, t, ta), jnp.where(up, f, fa),
                jnp.where(down, t, tb), jnp.where(down, f, fb),
                jnp.where(up, 1.0, jnp.where(down, -1.0, side)))

    st = lax.while_loop(search_cond, lambda st: search_body(search_body(st)), state0)
    key_lo, cnt_lo = st[1], st[3]
    thr = _ordered_key_inv(key_lo)

    thr_sc[...] = jnp.full((1, tq), INT_MAX, I32)
    tied = jnp.where(jnp.logical_and(cnt_lo != topk_f, n_valid > topk_f), 1.0, 0.0)

    @pl.when(jnp.max(tied) > 0.0)
    def _():
        room = topk_f - count(lambda s, off: s > thr)
        jmax = jnp.zeros((1, tq), I32)
        for bit in range(idx_bits - 1, -1, -1):
            cand = jmax | (1 << bit)
            g = count(lambda s, off: jnp.logical_and(s == thr, key_pos(off) <= cand))
            jmax = jnp.where(g <= room, cand, jmax)
        thr_sc[...] = jmax

    jmax = thr_sc[...]

    ahead, _, _ = _diag_distance(chunk_off(last), q0, tk, tq)

    def mask_chunk(c, masked):
        off = chunk_off(c)
        s = sc[pl.ds(off, tk), :]
        sel = jnp.logical_or(s > thr, jnp.logical_and(s == thr, key_pos(off) <= jmax))
        eff = key_pos(off).astype(F32)
        if masked:
            sel = jnp.logical_and(sel, visible(off))
            eff = eff - 2.0 * ahead
        sc[pl.ds(off, tk), :] = jnp.where(sel, 0.0, NEG)
        return jnp.max(jnp.where(sel, eff, -jnp.inf), axis=0, keepdims=True)

    nearest = lax.fori_loop(0, last, lambda c, n: jnp.maximum(n, mask_chunk(c, False)),
                            jnp.full((1, tq), -jnp.inf, F32))
    nearest = jnp.maximum(nearest, mask_chunk(last, True))

    lane2 = lax.broadcasted_iota(I32, (1, 2 * tq), 1)
    n_pairs = B_HEADS // 2
    slope_rows, q_aug = [], []
    for pi in range(n_pairs):
        slope_row = jnp.where(lane2 < tq, _alibi_slope(2 * pi, B_HEADS),
                              _alibi_slope(2 * pi + 1, B_HEADS)).astype(F32)
        slope_rows.append(slope_row)
        q_t = qbt_ref[0, pi * PAIR:(pi + 1) * PAIR, :]
        q_aug.append(jnp.concatenate([_split_halves(q_t, tq),
                                      _alibi_query_rows(slope_row, 2 * tq)], axis=0))
    ahead2 = jnp.concatenate([ahead, ahead], axis=1) * (-2.0 * LOG2E)
    nearest2 = jnp.concatenate([nearest, nearest], axis=1)

    def scores(pi, off, j, size, diagonal):
        start = pl.multiple_of(off + j * size, size)
        k_aug = jnp.concatenate([kb_ref[0, pl.ds(start, size), pi * PAIR:(pi + 1) * PAIR],
                                 e_ref[j * size:(j + 1) * size, :]], axis=1)
        mask = sc[pl.ds(start, size), :]
        s = (jnp.dot(k_aug, q_aug[pi], preferred_element_type=F32)
             + jnp.concatenate([mask, mask], axis=1))
        if diagonal:
            s = s + ahead2[j * size:(j + 1) * size, :] * slope_rows[pi]
        return s

    def frame(pi):
        return nearest2 * (slope_rows[pi] * LOG2E)

    def values(pi, start, size):
        start = pl.multiple_of(start, size)
        return vbt_ref[0, pi * V_ROWS:(pi + 1) * V_ROWS, pl.ds(start, size)]

    def base(pi, off):
        return off.astype(F32) * (slope_rows[pi] * LOG2E)

    _attend(n_pairs, n_chunks, tk, scores, values, base, frame, m_sc, acc_sc, p_sc)

    for pi in range(n_pairs):
        acc = acc_sc[pi]
        o = acc[:PAIR] / acc[PAIR:PAIR + 1]
        o = jnp.concatenate([o[:B_DH, :tq], o[B_DH:, tq:]], axis=0)
        o_ref[0, :, pi * PAIR:(pi + 1) * PAIR] = o.T.astype(BF16)


def _attn_b(qb_t, qi_t, w_t, kidx16, kb16, vb_t, *, tq, tk, q_pos0, topk):
    b, _, lq = qb_t.shape
    lkp = kb16.shape[1]
    q_blk = lambda rows: pl.BlockSpec((1, rows, tq), lambda bi, i: (bi, 0, i))
    per_batch = lambda bi, i: (bi, 0, 0)
    return pl.pallas_call(
        functools.partial(_attn_b_kernel, tq=tq, tk=tk, q_pos0=q_pos0, topk=topk,
                          idx_bits=max(1, (lkp - 1).bit_length())),
        grid=(b, lq // tq),
        in_specs=[pl.BlockSpec((tk, LANES), lambda bi, i: (0, 0)),
                  q_blk(WIDTH), q_blk(WIDTH), q_blk(IDX_HEADS),
                  _resident((1, lkp, IDX_DH), per_batch),
                  _resident((1, lkp, WIDTH), per_batch),
                  _resident((1, B_HEADS // 2 * V_ROWS, lkp), per_batch)],
        out_specs=pl.BlockSpec((1, tq, WIDTH), lambda bi, i: (bi, i, 0)),
        out_shape=jax.ShapeDtypeStruct((b, lq, WIDTH), BF16),
        scratch_shapes=[pltpu.VMEM((lkp, tq), F32), pltpu.VMEM((1, tq), I32),
                        pltpu.VMEM((B_HEADS // 2, 1, 2 * tq), F32),
                        pltpu.VMEM((B_HEADS // 2, PAIR + ONES_ROWS, 2 * tq), F32),
                        pltpu.VMEM((2, B_HEADS // 2, tk, 2 * tq), BF16)],
        compiler_params=pltpu.CompilerParams(dimension_semantics=("arbitrary", "arbitrary"),
                                             vmem_limit_bytes=VMEM_LIMIT),
        name="attn_b",
    )(_alibi_key_tile(tk), qb_t, qi_t, w_t, kidx16, kb16, vb_t)


def _nt_dot(a, b):
    return lax.dot_general(a, b, (((1,), (1,)), ((), ())), preferred_element_type=F32)


def _post_kernel(x_ref, oa_ref, ob_ref, qm_ref, ga_ref, gb_ref, gm_ref, gates_ref, mk_ref,
                 mv_ref, bg_ref, wa_ref, wb_ref, wm_ref, wo_ref, fg_ref, y_ref):
    d = x_ref.shape[-1]
    om = []
    for h in range(M_HEADS):
        hs = slice(h * M_DH, (h + 1) * M_DH)
        s = _nt_dot(qm_ref[0, :, hs], mk_ref[0, :, hs])
        p = jnp.exp2(s - jnp.max(s, axis=-1, keepdims=True))
        o = jnp.dot(p.astype(BF16), mv_ref[0, :, hs], preferred_element_type=F32)
        om.append(o / jnp.sum(p, axis=-1, keepdims=True))
    om = jnp.concatenate(om, axis=1)

    def branch(gate_ref, o, w_ref):
        g = gate_ref[0].astype(F32)
        return jnp.dot((g * jax.nn.sigmoid(g) * o).astype(BF16), w_ref[...],
                       preferred_element_type=F32)

    pa = branch(ga_ref, oa_ref[0].astype(F32), wa_ref)
    pb = branch(gb_ref, ob_ref[0].astype(F32), wb_ref)
    pm = branch(gm_ref, om, wm_ref)
    g = jax.nn.sigmoid(gates_ref[0].astype(F32) + bg_ref[...])
    merged = g[:, :d] * pa + g[:, d:2 * d] * pb + g[:, 2 * d:] * pm
    y = x_ref[0] + jnp.dot(merged.astype(BF16), wo_ref[...], preferred_element_type=F32)
    y_ref[0] = y * lax.rsqrt(jnp.mean(y * y, axis=-1, keepdims=True) + EPS) * fg_ref[...]


def _post(x, oa, ob, qm, ga, gb, gm, gates, mk16, mv16, b_gate, wa, wb, wm, wo, final_g, tm):
    b, lq, d = x.shape
    n_mem = mk16.shape[1]
    blk = lambda w: pl.BlockSpec((1, tm, w), lambda bi, i: (bi, i, 0))
    per_batch = pl.BlockSpec((1, n_mem, WIDTH), lambda bi, i: (bi, 0, 0))
    full = lambda r, c: pl.BlockSpec((r, c), lambda bi, i: (0, 0))
    return pl.pallas_call(
        _post_kernel,
        grid=(b, lq // tm),
        in_specs=[blk(d)] + [blk(WIDTH)] * 6 + [blk(N_BRANCH * d), per_batch, per_batch,
                  full(1, N_BRANCH * d), full(WIDTH, d), full(WIDTH, d), full(WIDTH, d),
                  full(d, d), full(1, d)],
        out_specs=blk(d),
        out_shape=jax.ShapeDtypeStruct((b, lq, d), F32),
        compiler_params=pltpu.CompilerParams(dimension_semantics=("arbitrary", "arbitrary"),
                                             vmem_limit_bytes=VMEM_LIMIT),
        name="post",
    )(x, oa, ob, qm, ga, gb, gm, gates, mk16, mv16, b_gate.reshape(1, -1), wa, wb, wm, wo,
      final_g.reshape(1, d))


def _values_on_lanes(v16):
    b, lk, _ = v16.shape
    v_t = jnp.swapaxes(v16, 1, 2).reshape(b, WIDTH // PAIR, PAIR, lk)
    ones = jnp.ones((b, WIDTH // PAIR, ONES_ROWS, lk), v16.dtype)
    return jnp.concatenate([v_t, ones], axis=2).reshape(b, WIDTH // PAIR * V_ROWS, lk)


def _pad_axis(x, size, axis):
    pad = size - x.shape[axis]
    if pad == 0:
        return x
    widths = [(0, 0)] * x.ndim
    widths[axis] = (0, pad)
    return jnp.pad(x, widths)


def _layer(x, past, mk16, mv16, q_pos0, norm_g, w16, b_gate, lam_params, lam_init, subln_g,
           wa, wb, wm, wo, final_g, *, tm, tq_a, tq, tk):
    b, lq, d = x.shape
    (qa, ka16, va16, ga, qb, kb16, vb16, gb, qi, qm, gm, gates, ka, va, kb, vb, kw) = [
        t.reshape(b, lq, -1) for t in _proj(x.reshape(b * lq, d), norm_g, w16, tm)]
    kidx = kw[..., :IDX_DH]
    new_rows = (ka.reshape(b, lq, A_HEADS, 2, A_DH), va.reshape(b, lq, A_HEADS, 2 * A_DH),
                kb.reshape(b, lq, B_HEADS, B_DH), vb.reshape(b, lq, B_HEADS, B_DH), kidx)
    kidx16 = kidx.astype(BF16)
    if past is not None:
        pa_k, pa_v, pb_k, pb_v, p_kidx = past
        n_past = pa_k.shape[1]
        cat = lambda p, n: jnp.concatenate([p.reshape(b, n_past, -1).astype(BF16), n], axis=1)
        ka16, va16, kb16, vb16, kidx16 = (cat(pa_k, ka16), cat(pa_v, va16), cat(pb_k, kb16),
                                          cat(pb_v, vb16), cat(p_kidx, kidx16))
    lk = ka16.shape[1]
    topk = min(TOPK_MAX, lk // 4)
    lkp = -(-lk // tk) * tk
    ka16, va16, kb16, vb16, kidx16 = [_pad_axis(t, lkp, 1)
                                      for t in (ka16, va16, kb16, vb16, kidx16)]
    lqp = -(-lq // max(tq_a, tq)) * max(tq_a, tq)
    to_lanes = lambda t: jnp.swapaxes(_pad_axis(t, lqp, 1), 1, 2)
    qa_t, qb_t, qi_t = to_lanes(qa), to_lanes(qb), to_lanes(qi)
    w_t = to_lanes(kw[..., IDX_DH:IDX_DH + IDX_HEADS])
    va_t, vb_t = _values_on_lanes(va16), _values_on_lanes(vb16)

    oa = _attn_a(qa_t, ka16, va_t, lam_params, subln_g, tq=tq_a, tk=tk, q_pos0=q_pos0,
                 lam_init=lam_init)[:, :lq]
    ob = _attn_b(qb_t, qi_t, w_t, kidx16, kb16, vb_t, tq=tq, tk=tk, q_pos0=q_pos0,
                 topk=topk)[:, :lq]
    y = _post(x, oa, ob, qm, ga, gb, gm, gates, mk16, mv16, b_gate, wa, wb, wm, wo, final_g, tm)
    return y, new_rows


def kernel(x_prompt, x_sample, mem_prompt, cache_a_k, cache_a_v, cache_b_k, cache_b_v, cache_b_kidx, cache_mem_k, cache_mem_v, norm_g, w_in, b_gate, lam_q1, lam_k1, lam_q2, lam_k2, subln_g, mem_norm_g, w_mem_kv, w_br_a, w_br_b, w_br_m, w_out, final_g):
    depth, d, _ = w_in.shape
    assert depth == 1, "single-layer step only"
    l = 0
    lam_init = 0.8 - 0.6 * math.exp(-0.3 * l)
    w16 = jnp.concatenate([w_in[l][:, :IN_KW_END], jnp.zeros((d, KW_PAD), w_in.dtype),
                           w_in[l][:, IN_KW_END:]], axis=1).astype(BF16)
    lam_params = [p[l].reshape(1, A_DH) for p in (lam_q1, lam_k1, lam_q2, lam_k2)]
    shared = (norm_g[l], w16, b_gate[l], lam_params, lam_init, subln_g[l],
              w_br_a[l].astype(BF16), w_br_b[l].astype(BF16), w_br_m[l].astype(BF16),
              w_out[l].astype(BF16), final_g)

    bp, n_mem, _ = mem_prompt.shape
    mk32, mv32, mk16, mv16 = _memkv(mem_prompt.reshape(bp * n_mem, d), mem_norm_g[l],
                                    w_mem_kv[l].astype(BF16), tm=256)
    y_p, rows_p = _layer(x_prompt, None, mk16.reshape(bp, n_mem, WIDTH),
                         mv16.reshape(bp, n_mem, WIDTH), 0, *shared, tm=256, tq_a=256, tq=128,
                         tk=1024)

    bs, n_past = cache_a_k.shape[1], cache_a_k.shape[2]
    past = (cache_a_k[l], cache_a_v[l], cache_b_k[l], cache_b_v[l], cache_b_kidx[l])
    ls = x_sample.shape[1]
    y_s, rows_s = _layer(x_sample, past, cache_mem_k[l].reshape(bs, n_mem, WIDTH).astype(BF16),
                         cache_mem_v[l].reshape(bs, n_mem, WIDTH).astype(BF16), n_past, *shared,
                         tm=ls, tq_a=128, tq=128, tk=384)

    mem_shape = (1, bp, n_mem, M_HEADS, M_DH)
    return (y_p, y_s, *[r[None] for r in rows_p], mk32.reshape(mem_shape),
            mv32.reshape(mem_shape), *[r[None] for r in rows_s])
```

```python
import functools
import math

import jax
import jax.numpy as jnp
import numpy as np
from jax import lax
from jax.experimental import pallas as pl
from jax.experimental.pallas import tpu as pltpu

F32 = jnp.float32
BF16 = jnp.bfloat16
I32 = jnp.int32

EPS = 1e-6
CHUNK = 64
CHUNK_SHIFT = 6
A_HEADS, A_DH = 4, 64
B_HEADS, B_DH = 8, 64
IDX_HEADS, IDX_DH = 8, 64
M_HEADS, M_DH = 4, 128
N_BRANCH = 3
TOPK_MAX = 256
WIDTH = 512
LANES = 128
PAIR = 128
ONES_ROWS = 16
V_ROWS = PAIR + ONES_ROWS
LOG2E = 1.4426950408889634
NEG = -1e30
INT_MAX = 2147483647
KEY_NEG_INF = -2139095041
VMEM_LIMIT = 60 * 1024 * 1024
BISECT_EVERY = 8
MAX_SEARCH_STEPS = BISECT_EVERY * 34
SCAN_CHUNKS = 2
SUB_KEYS = 256
MAX_EXP2 = 64.0

C_QA, C_KA, C_VA, C_GA = 0, 512, 1024, 1536
C_QB, C_KB, C_VB, C_GB = 2048, 2560, 3072, 3584
C_QI, C_KW, C_QM, C_GM, C_GATES = 4096, 4608, 4736, 5248, 5760
KW_PAD = LANES - IDX_DH - IDX_HEADS
W_COLS = C_GATES + N_BRANCH * 1024
IN_KW_END = 4680


def _bf16_pieces(x, n):
    out, rest = [], float(x)
    for _ in range(n):
        p = float(np.asarray(rest, np.float32).astype(BF16).astype(np.float32))
        out.append(p)
        rest -= p
    return out


LOG2E_PIECES = _bf16_pieces(LOG2E, 3)
POS_SPLIT = 256


def _resident(block_shape, index_map):
    return pl.BlockSpec(block_shape, index_map, pipeline_mode=pl.Buffered(1))


def _alibi_slope(h, n):
    return 2.0 ** (-8.0 * (h + 1) / n)


PROJ_OUTPUTS = (
    ("ka16", BF16, ("rows", WIDTH)), ("kb16", BF16, ("rows", WIDTH)),
    ("ga", BF16, ("rows", WIDTH)), ("gb", BF16, ("rows", WIDTH)), ("gm", BF16, ("rows", WIDTH)),
    ("qm", BF16, ("rows", WIDTH)), ("gates", BF16, ("rows", N_BRANCH * 1024)),
    ("kidx16", BF16, ("rows", IDX_DH)),
    ("qa_t", BF16, ("lanes", WIDTH)), ("qb_t", BF16, ("lanes", WIDTH)),
    ("qi_t", BF16, ("lanes", WIDTH)),
    ("va_t", BF16, ("lanes", A_HEADS * V_ROWS)), ("vb_t", BF16, ("lanes", B_HEADS // 2 * V_ROWS)),
    ("ka_t32", F32, ("lanes", WIDTH)), ("kb_t32", F32, ("lanes", WIDTH)),
    ("vb_t32", F32, ("lanes", WIDTH)), ("kw_t32", F32, ("lanes", LANES)),
    ("va4", F32, ("split", LANES)),
)


def _proj_kernel(x_ref, g_ref, w_ref, *out_refs):
    o = dict(zip([name for name, _, _ in PROJ_OUTPUTS], out_refs))
    tm = x_ref.shape[1]
    x = x_ref[0]
    hn = (x * lax.rsqrt(jnp.mean(x * x, axis=-1, keepdims=True) + EPS) * g_ref[...]).astype(BF16)

    def mm(c0, width):
        return jnp.dot(hn, w_ref[:, c0:c0 + width], preferred_element_type=F32)

    o["qa_t"][0] = (mm(C_QA, WIDTH) * (A_DH ** -0.5 * LOG2E)).T.astype(BF16)
    o["qb_t"][0] = (mm(C_QB, WIDTH) * (B_DH ** -0.5 * LOG2E)).T.astype(BF16)
    o["qi_t"][0] = (mm(C_QI, WIDTH) * (IDX_DH ** -0.5)).T.astype(BF16)
    o["qm"][0] = (mm(C_QM, WIDTH) * (M_DH ** -0.5 * LOG2E)).astype(BF16)
    for c0, k16, k_t32 in ((C_KA, "ka16", "ka_t32"), (C_KB, "kb16", "kb_t32")):
        z = mm(c0, WIDTH)
        o[k16][0] = z.astype(BF16)
        o[k_t32][0] = z.T
    ones = jnp.ones((ONES_ROWS, tm), BF16)
    for c0, v_t, v_t32 in ((C_VA, "va_t", None), (C_VB, "vb_t", "vb_t32")):
        z = mm(c0, WIDTH)
        for h in range(WIDTH // PAIR):
            z_h = z[:, h * PAIR:(h + 1) * PAIR]
            o[v_t][0, h * V_ROWS:h * V_ROWS + PAIR, :] = z_h.T.astype(BF16)
            o[v_t][0, h * V_ROWS + PAIR:(h + 1) * V_ROWS, :] = ones
            if v_t32 is None:
                o["va4"][0, pl.ds(h, tm, stride=A_HEADS), :] = z_h
        if v_t32 is not None:
            o[v_t32][0] = z.T
    for c0, name in ((C_GA, "ga"), (C_GB, "gb"), (C_GM, "gm")):
        o[name][0] = mm(c0, WIDTH).astype(BF16)
    kw = mm(C_KW, LANES)
    o["kidx16"][0] = kw[:, :IDX_DH].astype(BF16)
    o["kw_t32"][0] = kw.T
    for c in range(N_BRANCH * 1024 // WIDTH):
        o["gates"][0, :, c * WIDTH:(c + 1) * WIDTH] = mm(C_GATES + c * WIDTH, WIDTH).astype(BF16)


def _proj(x, norm_g, w16, tm):
    b, s, d = x.shape
    fixed = lambda bi, i: (0, 0)
    out_shape, out_specs = [], []
    for _, dtype, (kind, n) in PROJ_OUTPUTS:
        if kind == "rows":
            shape, block, index = (b, s, n), (1, tm, n), (lambda bi, i: (bi, i, 0))
        elif kind == "lanes":
            shape, block, index = (b, n, s), (1, n, tm), (lambda bi, i: (bi, 0, i))
        else:
            shape, block, index = ((b, s * A_HEADS, n), (1, tm * A_HEADS, n),
                                   (lambda bi, i: (bi, i, 0)))
        out_shape.append(jax.ShapeDtypeStruct(shape, dtype))
        out_specs.append(pl.BlockSpec(block, index))
    outs = pl.pallas_call(
        _proj_kernel,
        grid=(b, s // tm),
        in_specs=[pl.BlockSpec((1, tm, d), lambda bi, i: (bi, i, 0)), pl.BlockSpec((1, d), fixed),
                  _resident((d, W_COLS), fixed)],
        out_specs=out_specs,
        out_shape=out_shape,
        compiler_params=pltpu.CompilerParams(dimension_semantics=("arbitrary", "arbitrary"),
                                             vmem_limit_bytes=VMEM_LIMIT),
        name="proj",
    )(x, norm_g.reshape(1, d), w16)
    return dict(zip([name for name, _, _ in PROJ_OUTPUTS], outs))


def _memkv_kernel(x_ref, g_ref, w_ref, k32, v32, k16, v16):
    x = x_ref[...]
    hn = (x * lax.rsqrt(jnp.mean(x * x, axis=-1, keepdims=True) + EPS) * g_ref[...]).astype(BF16)
    zk = jnp.dot(hn, w_ref[:, :WIDTH], preferred_element_type=F32)
    zv = jnp.dot(hn, w_ref[:, WIDTH:], preferred_element_type=F32)
    k32[...] = zk
    v32[...] = zv
    k16[...] = zk.astype(BF16)
    v16[...] = zv.astype(BF16)


def _memkv(mem2d, g, w16, tm):
    n, d = mem2d.shape
    row = lambda i: (i, 0)
    fixed = lambda i: (0, 0)
    return pl.pallas_call(
        _memkv_kernel,
        grid=(n // tm,),
        in_specs=[pl.BlockSpec((tm, d), row), pl.BlockSpec((1, d), fixed),
                  pl.BlockSpec((d, 2 * WIDTH), fixed)],
        out_specs=[pl.BlockSpec((tm, WIDTH), row)] * 4,
        out_shape=[jax.ShapeDtypeStruct((n, WIDTH), F32)] * 2
        + [jax.ShapeDtypeStruct((n, WIDTH), BF16)] * 2,
        compiler_params=pltpu.CompilerParams(dimension_semantics=("arbitrary",)),
        name="memkv",
    )(mem2d, g.reshape(1, d), w16)


def _visible_chunks(q0, tq, tk):
    n_vis = (lax.shift_right_logical(q0 + tq - 1, CHUNK_SHIFT) + 1) * CHUNK
    return (n_vis + tk - 1) // tk


def _alibi_key_tile(tk):
    pos = np.arange(tk)
    lo = pos % POS_SPLIT
    tile = np.zeros((tk, LANES), np.float32)
    tile[:, 0:3] = lo[:, None]
    tile[:, 3:6] = (pos - lo)[:, None]
    return jnp.asarray(tile, BF16)


def _alibi_query_rows(slopes_row, width):
    row = lax.broadcasted_iota(I32, (LANES, 1), 0)
    piece = jnp.zeros((LANES, 1), F32)
    for j, p in enumerate(LOG2E_PIECES):
        piece = jnp.where(jnp.logical_or(row == j, row == j + 3), p, piece)
    return (piece * slopes_row).astype(BF16) + jnp.zeros((LANES, width), BF16)


def _split_halves(q_t, tq):
    row = lax.broadcasted_iota(I32, (PAIR, 1), 0)
    zero = jnp.zeros_like(q_t)
    return jnp.concatenate([jnp.where(row < PAIR // 2, q_t, zero),
                            jnp.where(row >= PAIR // 2, q_t, zero)], axis=1)


def _diag_distance(off, q0, tk, tq):
    kpos = off + lax.broadcasted_iota(I32, (tk, 1), 0)
    qpos = q0 + lax.broadcasted_iota(I32, (1, tq), 1)
    return jnp.maximum(kpos - qpos, 0).astype(F32), kpos, qpos


def _online_update(s, v_aug, chunk_base, m_ref, acc_ref, idx):
    m_old = m_ref[idx]
    m_new = jnp.maximum(m_old, jnp.max(s, axis=0, keepdims=True) + chunk_base)
    p = jnp.exp2(s - (m_new - chunk_base)).astype(BF16)
    pv = jnp.dot(v_aug, p, preferred_element_type=F32)
    acc_ref[idx] = jnp.exp2(m_old - m_new) * acc_ref[idx] + pv
    m_ref[idx] = m_new


def _attend(n_groups, n_chunks, tk, scores, values, base, frame, m_ref, acc_ref, p_ref):
    last = n_chunks - 1
    sub = SUB_KEYS if tk % SUB_KEYS == 0 else LANES

    def chunk_off(c):
        return pl.multiple_of(c * tk, tk)

    acc_ref[...] = jnp.zeros(acc_ref.shape, F32)

    def weigh(c, diagonal, slot):
        off = chunk_off(c)
        for g in range(n_groups):
            shift = frame(g) - base(g, off)
            for j in range(tk // sub):
                s = scores(g, off, j, sub, diagonal)
                p_ref[slot, g, j * sub:(j + 1) * sub, :] = jnp.exp2(s - shift).astype(BF16)

    def gather(c, slot):
        off = chunk_off(c)
        for g in range(n_groups):
            acc_ref[g] += jnp.dot(values(g, off, tk), p_ref[slot, g],
                                  preferred_element_type=F32)

    def step(i, slot):
        gather(jnp.where(i == 0, last, i - 1), slot)
        weigh(i, False, 1 - slot)

    def two_steps(i2, carry):
        step(2 * i2, 0)
        step(2 * i2 + 1, 1)
        return carry

    weigh(last, True, 0)
    lax.fori_loop(0, last // 2, two_steps, 0)

    @pl.when(lax.rem(last, 2) == 1)
    def _():
        step(last - 1, 0)

    gather(jnp.where(last == 0, 0, last - 1), lax.rem(last, 2))

    low, high, peak = None, None, None
    for g in range(n_groups):
        acc = acc_ref[g]
        sums = acc[acc.shape[0] - ONES_ROWS:acc.shape[0] - ONES_ROWS + 1]
        mag = jnp.max(jnp.abs(acc), axis=0, keepdims=True)
        low = sums if low is None else jnp.minimum(low, sums)
        high = sums if high is None else jnp.maximum(high, sums)
        peak = mag if peak is None else jnp.maximum(peak, mag)
    trusted = jnp.logical_and(jnp.min(low) >= 2.0 ** -MAX_EXP2,
                              jnp.logical_and(jnp.max(high) <= 2.0 ** MAX_EXP2,
                                              jnp.max(peak) <= 2.0 ** (2 * MAX_EXP2 - 1)))

    @pl.when(jnp.logical_not(trusted))
    def _():
        m_ref[...] = jnp.full(m_ref.shape, NEG, F32)
        acc_ref[...] = jnp.zeros(acc_ref.shape, F32)

        def chunk(c, diagonal):
            off = chunk_off(c)
            for g in range(n_groups):
                _online_update(scores(g, off, 0, tk, diagonal), values(g, off, tk),
                               base(g, off), m_ref, acc_ref, g)

        def body(c, carry):
            chunk(c, False)
            return carry

        lax.fori_loop(0, last, body, 0)
        chunk(last, True)


def _attn_a_kernel(lq1, lk1, lq2, lk2, subg_ref, e_ref, qt_ref, k_ref, vt_ref, o_ref,
                   m_sc, acc_sc, p_sc, *, tq, tk, q_pos0, lam_init):
    q0 = q_pos0 + pl.program_id(1) * tq
    n_chunks = _visible_chunks(q0, tq, tk)
    lam = (jnp.exp(jnp.sum(lq1[...] * lk1[...], axis=-1, keepdims=True))
           - jnp.exp(jnp.sum(lq2[...] * lk2[...], axis=-1, keepdims=True)) + lam_init)
    slopes = [_alibi_slope(h, A_HEADS) for h in range(A_HEADS)]
    q_aug = []
    for h in range(A_HEADS):
        q_t = qt_ref[0, h * PAIR:(h + 1) * PAIR, :]
        q_aug.append(jnp.concatenate([_split_halves(q_t, tq),
                                      _alibi_query_rows(slopes[h], 2 * tq)], axis=0))
    ahead, kpos, qpos = _diag_distance((n_chunks - 1) * tk, q0, tk, tq)
    ahead = jnp.where(lax.shift_right_logical(kpos, CHUNK_SHIFT)
                      <= lax.shift_right_logical(qpos, CHUNK_SHIFT), ahead * (-2.0 * LOG2E), NEG)

    def scores(h, off, j, size, diagonal):
        hs = slice(h * PAIR, (h + 1) * PAIR)
        start = pl.multiple_of(off + j * size, size)
        k_aug = jnp.concatenate([k_ref[0, pl.ds(start, size), hs],
                                 e_ref[j * size:(j + 1) * size, :]], axis=1)
        s = jnp.dot(k_aug, q_aug[h], preferred_element_type=F32)
        if diagonal:
            fix = ahead[j * size:(j + 1) * size, :] * slopes[h]
            s = s + jnp.concatenate([fix, fix], axis=1)
        return s

    def values(h, start, size):
        start = pl.multiple_of(start, size)
        return vt_ref[0, h * V_ROWS:(h + 1) * V_ROWS, pl.ds(start, size)]

    def base(h, off):
        return off.astype(F32) * (slopes[h] * LOG2E)

    qpos2 = jnp.concatenate([qpos, qpos], axis=1).astype(F32)

    def frame(h):
        return qpos2 * (slopes[h] * LOG2E)

    _attend(A_HEADS, n_chunks, tk, scores, values, base, frame, m_sc, acc_sc, p_sc)

    for h in range(A_HEADS):
        acc = acc_sc[h]
        o = acc[:PAIR] / acc[PAIR:PAIR + 1]
        o = o[:, :tq] - lam * o[:, tq:]
        o = o * lax.rsqrt(jnp.mean(o * o, axis=0, keepdims=True) + EPS) * subg_ref[...]
        o_ref[0, :, h * PAIR:(h + 1) * PAIR] = (o * (1.0 - lam_init)).T.astype(BF16)


def _attn_a(qa_t, ka16, va_t, lam_params, subln_g, *, tq, tk, q_pos0, lam_init):
    b, _, lq = qa_t.shape
    lkp = ka16.shape[1]
    small = lambda bi, i: (0, 0)
    return pl.pallas_call(
        functools.partial(_attn_a_kernel, tq=tq, tk=tk, q_pos0=q_pos0, lam_init=lam_init),
        grid=(b, lq // tq),
        in_specs=[pl.BlockSpec((1, A_DH), small)] * 4 + [
            pl.BlockSpec((2 * A_DH, 1), small), pl.BlockSpec((tk, LANES), small),
            pl.BlockSpec((1, WIDTH, tq), lambda bi, i: (bi, 0, i)),
            _resident((1, lkp, WIDTH), lambda bi, i: (bi, 0, 0)),
            _resident((1, A_HEADS * V_ROWS, lkp), lambda bi, i: (bi, 0, 0))],
        out_specs=pl.BlockSpec((1, tq, WIDTH), lambda bi, i: (bi, i, 0)),
        out_shape=jax.ShapeDtypeStruct((b, lq, WIDTH), BF16),
        scratch_shapes=[pltpu.VMEM((A_HEADS, 1, 2 * tq), F32),
                        pltpu.VMEM((A_HEADS, PAIR + ONES_ROWS, 2 * tq), F32),
                        pltpu.VMEM((2, A_HEADS, tk, 2 * tq), BF16)],
        compiler_params=pltpu.CompilerParams(dimension_semantics=("arbitrary", "arbitrary"),
                                             vmem_limit_bytes=VMEM_LIMIT),
        name="attn_a",
    )(*lam_params, subln_g.reshape(2 * A_DH, 1), _alibi_key_tile(tk), qa_t, ka16, va_t)


def _sum_keys(x):
    rows, tq = x.shape
    lanes_of_adds = 64
    if rows % lanes_of_adds == 0 and rows > lanes_of_adds:
        x = jnp.sum(x.reshape(rows // lanes_of_adds, lanes_of_adds, tq), axis=0)
    return jnp.sum(x, axis=0, keepdims=True)


def _count_true(hit):
    return _sum_keys(jnp.where(hit, 1.0, 0.0))


def _ordered_key(x):
    bits = pltpu.bitcast(x, I32)
    return bits ^ (lax.shift_right_arithmetic(bits, 31) & INT_MAX)


def _ordered_key_inv(k):
    return pltpu.bitcast(k ^ (lax.shift_right_arithmetic(k, 31) & INT_MAX), F32)


def _attn_b_kernel(e_ref, qbt_ref, qit_ref, wt_ref, kidx_ref, kb_ref, vbt_ref, o_ref,
                   sc, thr_sc, m_sc, acc_sc, p_sc, *, tq, tk, q_pos0, topk, idx_bits):
    q0 = q_pos0 + pl.program_id(1) * tq
    n_chunks = _visible_chunks(q0, tq, tk)
    last = n_chunks - 1
    topk_f = float(topk)
    qpos = q0 + lax.broadcasted_iota(I32, (1, tq), 1)
    qchunk = lax.shift_right_logical(qpos, CHUNK_SHIFT)
    n_valid = ((qchunk + 1) * CHUNK).astype(F32)

    def chunk_off(c):
        return pl.multiple_of(c * tk, tk)

    def key_pos(off):
        return off + lax.broadcasted_iota(I32, (tk, 1), 0)

    def visible(off):
        return lax.shift_right_logical(key_pos(off), CHUNK_SHIFT) <= qchunk

    qi_all = jnp.concatenate([qit_ref[0, h * IDX_DH:(h + 1) * IDX_DH, :]
                              for h in range(IDX_HEADS)], axis=1)
    w_rows = [wt_ref[0, h:h + 1, :] for h in range(IDX_HEADS)]

    sub = SUB_KEYS if tk % SUB_KEYS == 0 else LANES

    def merge_stats(a, b):
        return (jnp.maximum(a[0], b[0]), jnp.minimum(a[1], b[1]), a[2] + b[2], a[3] + b[3])

    def score_chunk(c, masked):
        stats = None
        for j in range(tk // sub):
            start = pl.multiple_of(c * tk + j * sub, sub)
            logits = jnp.dot(kidx_ref[0, pl.ds(start, sub), :], qi_all,
                             preferred_element_type=F32)
            score = jnp.zeros((sub, tq), F32)
            for h in range(IDX_HEADS):
                score = score + jnp.maximum(logits[:, h * tq:(h + 1) * tq], 0.0) * w_rows[h]
            lowest = score
            if masked:
                vis = (lax.shift_right_logical(start + lax.broadcasted_iota(I32, (sub, 1), 0),
                                               CHUNK_SHIFT) <= qchunk)
                score = jnp.where(vis, score, -jnp.inf)
                lowest = jnp.where(vis, score, jnp.inf)
            sc[pl.ds(start, sub), :] = score
            part = (jnp.max(score, axis=0, keepdims=True), jnp.min(lowest, axis=0, keepdims=True),
                    _count_true(score >= 0.0), _count_true(score > 0.0))
            stats = part if stats is None else merge_stats(stats, part)
        return stats

    def score_two(i2, st):
        st = merge_stats(st, score_chunk(2 * i2, False))
        return merge_stats(st, score_chunk(2 * i2 + 1, False))

    stats = lax.fori_loop(
        0, last // 2, score_two,
        (jnp.full((1, tq), -jnp.inf, F32), jnp.full((1, tq), jnp.inf, F32),
         jnp.zeros((1, tq), F32), jnp.zeros((1, tq), F32)))
    stats = lax.cond(lax.rem(last, 2) == 1,
                     lambda st: merge_stats(st, score_chunk(last - 1, False)),
                     lambda st: st, stats)
    row_max, row_min, n_ge0, n_pos = merge_stats(stats, score_chunk(last, True))

    def count(pred):
        def one(c):
            off = chunk_off(c)
            return _count_true(pred(sc[pl.ds(off, tk), :], off))

        def body(i, acc):
            for u in range(SCAN_CHUNKS):
                c = SCAN_CHUNKS * i + u
                part = one(jnp.minimum(c, last))
                acc = acc + (part if u == 0 else jnp.where(c <= last, 1.0, 0.0) * part)
            return acc
        return lax.fori_loop(0, (n_chunks + SCAN_CHUNKS - 1) // SCAN_CHUNKS, body,
                             jnp.zeros((1, tq), F32))

    def count_ge(t):
        return count(lambda s, off: s >= t)

    def spread(cnt):
        c = jnp.clip(cnt, 0.5, n_valid - 0.5)
        return jnp.log2(c / (n_valid - c))

    target = spread(jnp.full((1, tq), topk_f, F32))
    key_lo, key_hi = _ordered_key(row_min), _ordered_key(row_max) + 1
    cnt_lo = n_valid
    t_a, f_a = row_min, spread(n_valid) - target
    t_b, f_b = row_max, spread(jnp.zeros((1, tq), F32)) - target
    zero_up = n_ge0 >= topk_f
    key_lo = jnp.where(zero_up, jnp.maximum(key_lo, 0), key_lo)
    cnt_lo = jnp.where(zero_up, n_ge0, cnt_lo)
    t_a = jnp.where(zero_up, 0.0, t_a)
    f_a = jnp.where(zero_up, spread(n_ge0) - target, f_a)
    zero_down = n_pos < topk_f
    key_hi = jnp.where(zero_down, jnp.minimum(key_hi, 1), key_hi)
    t_b = jnp.where(zero_down, 0.0, t_b)
    f_b = jnp.where(zero_down, spread(n_pos) - target, f_b)
    state0 = (jnp.int32(0), key_lo, key_hi, cnt_lo, t_a, f_a, t_b, f_b, jnp.zeros((1, tq), F32))

    def span(klo, khi):
        return khi - klo

    def finished(klo, khi, cnt_lo):
        d = span(klo, khi)
        tight = jnp.logical_or(d == 0, d == 1)
        return jnp.logical_or(jnp.logical_or(cnt_lo == topk_f, tight), n_valid <= topk_f)

    def search_cond(st):
        it, klo, khi, cnt_lo = st[:4]
        todo = jnp.where(finished(klo, khi, cnt_lo), 0.0, 1.0)
        return jnp.logical_and(it < MAX_SEARCH_STEPS, jnp.max(todo) > 0.0)

    def search_body(st):
        it, klo, khi, cnt_lo, ta, fa, tb, fb, side = st
        done = finished(klo, khi, cnt_lo)
        guess = ta + (tb - ta) * (fa / (fa - fb))
        guess = jnp.where(guess == guess, guess, ta)
        guess = jnp.clip(guess, -3e38, 3e38)
        mid = klo + lax.shift_right_logical(span(klo, khi), 1)
        cand = jnp.where(it % BISECT_EVERY == BISECT_EVERY - 1, mid, _ordered_key(guess))
        cand = jnp.minimum(jnp.maximum(cand, klo + 1), khi - 1)
        cand = jnp.where(done, klo, cand)
        t = _ordered_key_inv(cand)
        cnt = count_ge(t)
        up = jnp.logical_and(cnt >= topk_f, jnp.logical_not(done))
        down = jnp.logical_and(cnt < topk_f, jnp.logical_not(done))
        f = spread(cnt) - target
        fb = jnp.where(jnp.logical_and(up, side > 0.0), 0.5 * fb, fb)
        fa = jnp.where(jnp.logical_and(down, side < 0.0), 0.5 * fa, fa)
        return (it + 1, jnp.where(up, cand, klo), jnp.where(down, cand, khi),
                jnp.where(up, cnt, cnt_lo),
                jnp.where(up, t, ta), jnp.where(up, f, fa),
                jnp.where(down, t, tb), jnp.where(down, f, fb),
                jnp.where(up, 1.0, jnp.where(down, -1.0, side)))

    st = lax.while_loop(search_cond, lambda st: search_body(search_body(st)), state0)
    key_lo, cnt_lo = st[1], st[3]
    thr = _ordered_key_inv(key_lo)

    thr_sc[...] = jnp.full((1, tq), INT_MAX, I32)
    tied = jnp.where(jnp.logical_and(cnt_lo != topk_f, n_valid > topk_f), 1.0, 0.0)

    @pl.when(jnp.max(tied) > 0.0)
    def _():
        room = topk_f - count(lambda s, off: s > thr)
        jmax = jnp.zeros((1, tq), I32)
        for bit in range(idx_bits - 1, -1, -1):
            cand = jmax | (1 << bit)
            g = count(lambda s, off: jnp.logical_and(s == thr, key_pos(off) <= cand))
            jmax = jnp.where(g <= room, cand, jmax)
        thr_sc[...] = jmax

    jmax = thr_sc[...]

    ahead, _, _ = _diag_distance(chunk_off(last), q0, tk, tq)

    def mask_chunk(c, masked):
        off = chunk_off(c)
        s = sc[pl.ds(off, tk), :]
        sel = jnp.logical_or(s > thr, jnp.logical_and(s == thr, key_pos(off) <= jmax))
        eff = key_pos(off).astype(F32)
        if masked:
            sel = jnp.logical_and(sel, visible(off))
            eff = eff - 2.0 * ahead
        sc[pl.ds(off, tk), :] = jnp.where(sel, 0.0, NEG)
        return jnp.max(jnp.where(sel, eff, -jnp.inf), axis=0, keepdims=True)

    nearest = lax.fori_loop(0, last, lambda c, n: jnp.maximum(n, mask_chunk(c, False)),
                            jnp.full((1, tq), -jnp.inf, F32))
    nearest = jnp.maximum(nearest, mask_chunk(last, True))

    lane2 = lax.broadcasted_iota(I32, (1, 2 * tq), 1)
    n_pairs = B_HEADS // 2
    slope_rows, q_aug = [], []
    for pi in range(n_pairs):
        slope_row = jnp.where(lane2 < tq, _alibi_slope(2 * pi, B_HEADS),
                              _alibi_slope(2 * pi + 1, B_HEADS)).astype(F32)
        slope_rows.append(slope_row)
        q_t = qbt_ref[0, pi * PAIR:(pi + 1) * PAIR, :]
        q_aug.append(jnp.concatenate([_split_halves(q_t, tq),
                                      _alibi_query_rows(slope_row, 2 * tq)], axis=0))
    ahead2 = jnp.concatenate([ahead, ahead], axis=1) * (-2.0 * LOG2E)
    nearest2 = jnp.concatenate([nearest, nearest], axis=1)

    def scores(pi, off, j, size, diagonal):
        start = pl.multiple_of(off + j * size, size)
        k_aug = jnp.concatenate([kb_ref[0, pl.ds(start, size), pi * PAIR:(pi + 1) * PAIR],
                                 e_ref[j * size:(j + 1) * size, :]], axis=1)
        mask = sc[pl.ds(start, size), :]
        s = (jnp.dot(k_aug, q_aug[pi], preferred_element_type=F32)
             + jnp.concatenate([mask, mask], axis=1))
        if diagonal:
            s = s + ahead2[j * size:(j + 1) * size, :] * slope_rows[pi]
        return s

    def frame(pi):
        return nearest2 * (slope_rows[pi] * LOG2E)

    def values(pi, start, size):
        start = pl.multiple_of(start, size)
        return vbt_ref[0, pi * V_ROWS:(pi + 1) * V_ROWS, pl.ds(start, size)]

    def base(pi, off):
        return off.astype(F32) * (slope_rows[pi] * LOG2E)

    _attend(n_pairs, n_chunks, tk, scores, values, base, frame, m_sc, acc_sc, p_sc)

    for pi in range(n_pairs):
        acc = acc_sc[pi]
        o = acc[:PAIR] / acc[PAIR:PAIR + 1]
        o = jnp.concatenate([o[:B_DH, :tq], o[B_DH:, tq:]], axis=0)
        o_ref[0, :, pi * PAIR:(pi + 1) * PAIR] = o.T.astype(BF16)


def _attn_b(qb_t, qi_t, w_t, kidx16, kb16, vb_t, *, tq, tk, q_pos0, topk):
    b, _, lq = qb_t.shape
    lkp = kb16.shape[1]
    q_blk = lambda rows: pl.BlockSpec((1, rows, tq), lambda bi, i: (bi, 0, i))
    per_batch = lambda bi, i: (bi, 0, 0)
    return pl.pallas_call(
        functools.partial(_attn_b_kernel, tq=tq, tk=tk, q_pos0=q_pos0, topk=topk,
                          idx_bits=max(1, (lkp - 1).bit_length())),
        grid=(b, lq // tq),
        in_specs=[pl.BlockSpec((tk, LANES), lambda bi, i: (0, 0)),
                  q_blk(WIDTH), q_blk(WIDTH), q_blk(IDX_HEADS),
                  _resident((1, lkp, IDX_DH), per_batch),
                  _resident((1, lkp, WIDTH), per_batch),
                  _resident((1, B_HEADS // 2 * V_ROWS, lkp), per_batch)],
        out_specs=pl.BlockSpec((1, tq, WIDTH), lambda bi, i: (bi, i, 0)),
        out_shape=jax.ShapeDtypeStruct((b, lq, WIDTH), BF16),
        scratch_shapes=[pltpu.VMEM((lkp, tq), F32), pltpu.VMEM((1, tq), I32),
                        pltpu.VMEM((B_HEADS // 2, 1, 2 * tq), F32),
                        pltpu.VMEM((B_HEADS // 2, PAIR + ONES_ROWS, 2 * tq), F32),
                        pltpu.VMEM((2, B_HEADS // 2, tk, 2 * tq), BF16)],
        compiler_params=pltpu.CompilerParams(dimension_semantics=("arbitrary", "arbitrary"),
                                             vmem_limit_bytes=VMEM_LIMIT),
        name="attn_b",
    )(_alibi_key_tile(tk), qb_t, qi_t, w_t, kidx16, kb16, vb_t)


def _nt_dot(a, b):
    return lax.dot_general(a, b, (((1,), (1,)), ((), ())), preferred_element_type=F32)


def _post_kernel(x_ref, oa_ref, ob_ref, qm_ref, ga_ref, gb_ref, gm_ref, gates_ref, mk_ref,
                 mv_ref, bg_ref, wa_ref, wb_ref, wm_ref, wo_ref, fg_ref, y_ref):
    d = x_ref.shape[-1]
    om = []
    for h in range(M_HEADS):
        hs = slice(h * M_DH, (h + 1) * M_DH)
        s = _nt_dot(qm_ref[0, :, hs], mk_ref[0, :, hs])
        p = jnp.exp2(s - jnp.max(s, axis=-1, keepdims=True))
        o = jnp.dot(p.astype(BF16), mv_ref[0, :, hs], preferred_element_type=F32)
        om.append(o / jnp.sum(p, axis=-1, keepdims=True))
    om = jnp.concatenate(om, axis=1)

    def branch(gate_ref, o, w_ref):
        g = gate_ref[0].astype(F32)
        return jnp.dot((g * jax.nn.sigmoid(g) * o).astype(BF16), w_ref[...],
                       preferred_element_type=F32)

    pa = branch(ga_ref, oa_ref[0].astype(F32), wa_ref)
    pb = branch(gb_ref, ob_ref[0].astype(F32), wb_ref)
    pm = branch(gm_ref, om, wm_ref)
    g = jax.nn.sigmoid(gates_ref[0].astype(F32) + bg_ref[...])
    merged = g[:, :d] * pa + g[:, d:2 * d] * pb + g[:, 2 * d:] * pm
    y = x_ref[0] + jnp.dot(merged.astype(BF16), wo_ref[...], preferred_element_type=F32)
    y_ref[0] = y * lax.rsqrt(jnp.mean(y * y, axis=-1, keepdims=True) + EPS) * fg_ref[...]


def _post(x, oa, ob, qm, ga, gb, gm, gates, mk16, mv16, b_gate, wa, wb, wm, wo, final_g, tm):
    b, lq, d = x.shape
    n_mem = mk16.shape[1]
    blk = lambda w: pl.BlockSpec((1, tm, w), lambda bi, i: (bi, i, 0))
    per_batch = pl.BlockSpec((1, n_mem, WIDTH), lambda bi, i: (bi, 0, 0))
    full = lambda r, c: pl.BlockSpec((r, c), lambda bi, i: (0, 0))
    return pl.pallas_call(
        _post_kernel,
        grid=(b, lq // tm),
        in_specs=[blk(d)] + [blk(WIDTH)] * 6 + [blk(N_BRANCH * d), per_batch, per_batch,
                  full(1, N_BRANCH * d), full(WIDTH, d), full(WIDTH, d), full(WIDTH, d),
                  full(d, d), full(1, d)],
        out_specs=blk(d),
        out_shape=jax.ShapeDtypeStruct((b, lq, d), F32),
        compiler_params=pltpu.CompilerParams(dimension_semantics=("arbitrary", "arbitrary"),
                                             vmem_limit_bytes=VMEM_LIMIT),
        name="post",
    )(x, oa, ob, qm, ga, gb, gm, gates, mk16, mv16, b_gate.reshape(1, -1), wa, wb, wm, wo,
      final_g.reshape(1, d))


def _values_on_lanes(v16):
    b, lk, _ = v16.shape
    v_t = jnp.swapaxes(v16, 1, 2).reshape(b, WIDTH // PAIR, PAIR, lk)
    ones = jnp.ones((b, WIDTH // PAIR, ONES_ROWS, lk), v16.dtype)
    return jnp.concatenate([v_t, ones], axis=2).reshape(b, WIDTH // PAIR * V_ROWS, lk)


def _pad_axis(x, size, axis):
    pad = size - x.shape[axis]
    if pad == 0:
        return x
    widths = [(0, 0)] * x.ndim
    widths[axis] = (0, pad)
    return jnp.pad(x, widths)


def _layer(x, past, mk16, mv16, q_pos0, norm_g, w16, b_gate, lam_params, lam_init, subln_g,
           wa, wb, wm, wo, final_g, *, tm, tq_a, tq, tk):
    b, lq, d = x.shape
    p = _proj(x, norm_g, w16, tm)
    new_rows = (
        jnp.transpose(p["ka_t32"].reshape(b, A_HEADS, 2, A_DH, lq), (0, 4, 1, 2, 3)),
        p["va4"].reshape(b, lq, A_HEADS, 2 * A_DH),
        jnp.transpose(p["kb_t32"].reshape(b, B_HEADS, B_DH, lq), (0, 3, 1, 2)),
        jnp.transpose(p["vb_t32"].reshape(b, B_HEADS, B_DH, lq), (0, 3, 1, 2)),
        jnp.swapaxes(p["kw_t32"][:, :IDX_DH], 1, 2))
    p["w_t"] = p["kw_t32"][:, IDX_DH:IDX_DH + IDX_HEADS]
    ka16, kb16, kidx16, va_t, vb_t = p["ka16"], p["kb16"], p["kidx16"], p["va_t"], p["vb_t"]
    if past is not None:
        pa_k, pa_v, pb_k, pb_v, p_kidx = past
        n_past = pa_k.shape[1]
        flat16 = lambda t: t.reshape(b, n_past, -1).astype(BF16)
        rows = lambda old, new: jnp.concatenate([flat16(old), new], axis=1)
        lanes = lambda old, new: jnp.concatenate([_values_on_lanes(flat16(old)), new], axis=2)
        ka16, kb16, kidx16 = rows(pa_k, ka16), rows(pb_k, kb16), rows(p_kidx, kidx16)
        va_t, vb_t = lanes(pa_v, va_t), lanes(pb_v, vb_t)
    lk = ka16.shape[1]
    topk = min(TOPK_MAX, lk // 4)
    lkp = -(-lk // tk) * tk
    ka16, kb16, kidx16 = [_pad_axis(t, lkp, 1) for t in (ka16, kb16, kidx16)]
    va_t, vb_t = _pad_axis(va_t, lkp, 2), _pad_axis(vb_t, lkp, 2)
    lqp = -(-lq // max(tq_a, tq)) * max(tq_a, tq)
    qa_t, qb_t, qi_t, w_t = [_pad_axis(p[n], lqp, 2) for n in ("qa_t", "qb_t", "qi_t", "w_t")]

    oa = _attn_a(qa_t, ka16, va_t, lam_params, subln_g, tq=tq_a, tk=tk, q_pos0=q_pos0,
                 lam_init=lam_init)[:, :lq]
    ob = _attn_b(qb_t, qi_t, w_t, kidx16, kb16, vb_t, tq=tq, tk=tk, q_pos0=q_pos0,
                 topk=topk)[:, :lq]
    y = _post(x, oa, ob, p["qm"], p["ga"], p["gb"], p["gm"], p["gates"], mk16, mv16, b_gate,
              wa, wb, wm, wo, final_g, tm)
    return y, new_rows


def kernel(x_prompt, x_sample, mem_prompt, cache_a_k, cache_a_v, cache_b_k, cache_b_v, cache_b_kidx, cache_mem_k, cache_mem_v, norm_g, w_in, b_gate, lam_q1, lam_k1, lam_q2, lam_k2, subln_g, mem_norm_g, w_mem_kv, w_br_a, w_br_b, w_br_m, w_out, final_g):
    depth, d, _ = w_in.shape
    assert depth == 1, "single-layer step only"
    l = 0
    lam_init = 0.8 - 0.6 * math.exp(-0.3 * l)
    w16 = jnp.concatenate([w_in[l][:, :IN_KW_END], jnp.zeros((d, KW_PAD), w_in.dtype),
                           w_in[l][:, IN_KW_END:]], axis=1).astype(BF16)
    lam_params = [p[l].reshape(1, A_DH) for p in (lam_q1, lam_k1, lam_q2, lam_k2)]
    shared = (norm_g[l], w16, b_gate[l], lam_params, lam_init, subln_g[l],
              w_br_a[l].astype(BF16), w_br_b[l].astype(BF16), w_br_m[l].astype(BF16),
              w_out[l].astype(BF16), final_g)

    bp, n_mem, _ = mem_prompt.shape
    mk32, mv32, mk16, mv16 = _memkv(mem_prompt.reshape(bp * n_mem, d), mem_norm_g[l],
                                    w_mem_kv[l].astype(BF16), tm=256)
    y_p, rows_p = _layer(x_prompt, None, mk16.reshape(bp, n_mem, WIDTH),
                         mv16.reshape(bp, n_mem, WIDTH), 0, *shared, tm=256, tq_a=256, tq=128,
                         tk=1024)

    bs, n_past = cache_a_k.shape[1], cache_a_k.shape[2]
    past = (cache_a_k[l], cache_a_v[l], cache_b_k[l], cache_b_v[l], cache_b_kidx[l])
    ls = x_sample.shape[1]
    y_s, rows_s = _layer(x_sample, past, cache_mem_k[l].reshape(bs, n_mem, WIDTH).astype(BF16),
                         cache_mem_v[l].reshape(bs, n_mem, WIDTH).astype(BF16), n_past, *shared,
                         tm=ls, tq_a=128, tq=128, tk=384)

    mem_shape = (1, bp, n_mem, M_HEADS, M_DH)
    return (y_p, y_s, *[r[None] for r in rows_p], mk32.reshape(mem_shape),
            mv32.reshape(mem_shape), *[r[None] for r in rows_s])
```

```python
import functools
import math

import jax
import jax.numpy as jnp
import numpy as np
from jax import lax
from jax.experimental import pallas as pl
from jax.experimental.pallas import tpu as pltpu

F32 = jnp.float32
BF16 = jnp.bfloat16
I32 = jnp.int32

EPS = 1e-6
CHUNK = 64
CHUNK_SHIFT = 6
A_HEADS, A_DH = 4, 64
B_HEADS, B_DH = 8, 64
IDX_HEADS, IDX_DH = 8, 64
M_HEADS, M_DH = 4, 128
N_BRANCH = 3
TOPK_MAX = 256
WIDTH = 512
LANES = 128
PAIR = 128
ONES_ROWS = 16
V_ROWS = PAIR + ONES_ROWS
LOG2E = 1.4426950408889634
NEG = -1e30
INT_MAX = 2147483647
KEY_NEG_INF = -2139095041
VMEM_LIMIT = 60 * 1024 * 1024
BISECT_EVERY = 8
MAX_SEARCH_STEPS = BISECT_EVERY * 34
SCAN_CHUNKS = 2
SUB_KEYS = 256
MAX_EXP2 = 64.0
SKIP_EXP2 = 160.0
NORM_SLACK = 1.01

C_QA, C_KA, C_VA, C_GA = 0, 512, 1024, 1536
C_QB, C_KB, C_VB, C_GB = 2048, 2560, 3072, 3584
C_QI, C_KW, C_QM, C_GM, C_GATES = 4096, 4608, 4736, 5248, 5760
KW_PAD = LANES - IDX_DH - IDX_HEADS
W_COLS = C_GATES + N_BRANCH * 1024
IN_KW_END = 4680


def _bf16_pieces(x, n):
    out, rest = [], float(x)
    for _ in range(n):
        p = float(np.asarray(rest, np.float32).astype(BF16).astype(np.float32))
        out.append(p)
        rest -= p
    return out


LOG2E_PIECES = _bf16_pieces(LOG2E, 3)
POS_SPLIT = 256


def _resident(block_shape, index_map):
    return pl.BlockSpec(block_shape, index_map, pipeline_mode=pl.Buffered(1))


def _alibi_slope(h, n):
    return 2.0 ** (-8.0 * (h + 1) / n)


PROJ_OUTPUTS = (
    ("ka16", BF16, ("rows", WIDTH)), ("kb16", BF16, ("rows", WIDTH)),
    ("ga", BF16, ("rows", WIDTH)), ("gb", BF16, ("rows", WIDTH)), ("gm", BF16, ("rows", WIDTH)),
    ("qm", BF16, ("rows", WIDTH)), ("gates", BF16, ("rows", N_BRANCH * 1024)),
    ("kidx16", BF16, ("rows", IDX_DH)),
    ("qa_t", BF16, ("lanes", WIDTH)), ("qb_t", BF16, ("lanes", WIDTH)),
    ("qi_t", BF16, ("lanes", WIDTH)),
    ("va_t", BF16, ("lanes", A_HEADS * V_ROWS)), ("vb_t", BF16, ("lanes", B_HEADS // 2 * V_ROWS)),
    ("ka_t32", F32, ("lanes", WIDTH)), ("kb_t32", F32, ("lanes", WIDTH)),
    ("vb_t32", F32, ("lanes", WIDTH)), ("kw_t32", F32, ("lanes", LANES)),
    ("va4", F32, ("split", LANES)),
)


def _proj_kernel(x_ref, g_ref, w_ref, *out_refs):
    o = dict(zip([name for name, _, _ in PROJ_OUTPUTS], out_refs))
    tm = x_ref.shape[1]
    x = x_ref[0]
    hn = (x * lax.rsqrt(jnp.mean(x * x, axis=-1, keepdims=True) + EPS) * g_ref[...]).astype(BF16)

    def mm(c0, width):
        return jnp.dot(hn, w_ref[:, c0:c0 + width], preferred_element_type=F32)

    o["qa_t"][0] = (mm(C_QA, WIDTH) * (A_DH ** -0.5 * LOG2E)).T.astype(BF16)
    o["qb_t"][0] = (mm(C_QB, WIDTH) * (B_DH ** -0.5 * LOG2E)).T.astype(BF16)
    o["qi_t"][0] = (mm(C_QI, WIDTH) * (IDX_DH ** -0.5)).T.astype(BF16)
    o["qm"][0] = (mm(C_QM, WIDTH) * (M_DH ** -0.5 * LOG2E)).astype(BF16)
    for c0, k16, k_t32 in ((C_KA, "ka16", "ka_t32"), (C_KB, "kb16", "kb_t32")):
        z = mm(c0, WIDTH)
        o[k16][0] = z.astype(BF16)
        o[k_t32][0] = z.T
    ones = jnp.ones((ONES_ROWS, tm), BF16)
    for c0, v_t, v_t32 in ((C_VA, "va_t", None), (C_VB, "vb_t", "vb_t32")):
        z = mm(c0, WIDTH)
        for h in range(WIDTH // PAIR):
            z_h = z[:, h * PAIR:(h + 1) * PAIR]
            o[v_t][0, h * V_ROWS:h * V_ROWS + PAIR, :] = z_h.T.astype(BF16)
            o[v_t][0, h * V_ROWS + PAIR:(h + 1) * V_ROWS, :] = ones
            if v_t32 is None:
                o["va4"][0, pl.ds(h, tm, stride=A_HEADS), :] = z_h
        if v_t32 is not None:
            o[v_t32][0] = z.T
    for c0, name in ((C_GA, "ga"), (C_GB, "gb"), (C_GM, "gm")):
        o[name][0] = mm(c0, WIDTH).astype(BF16)
    kw = mm(C_KW, LANES)
    o["kidx16"][0] = kw[:, :IDX_DH].astype(BF16)
    o["kw_t32"][0] = kw.T
    for c in range(N_BRANCH * 1024 // WIDTH):
        o["gates"][0, :, c * WIDTH:(c + 1) * WIDTH] = mm(C_GATES + c * WIDTH, WIDTH).astype(BF16)


def _proj(x, norm_g, w16, tm):
    b, s, d = x.shape
    fixed = lambda bi, i: (0, 0)
    out_shape, out_specs = [], []
    for _, dtype, (kind, n) in PROJ_OUTPUTS:
        if kind == "rows":
            shape, block, index = (b, s, n), (1, tm, n), (lambda bi, i: (bi, i, 0))
        elif kind == "lanes":
            shape, block, index = (b, n, s), (1, n, tm), (lambda bi, i: (bi, 0, i))
        else:
            shape, block, index = ((b, s * A_HEADS, n), (1, tm * A_HEADS, n),
                                   (lambda bi, i: (bi, i, 0)))
        out_shape.append(jax.ShapeDtypeStruct(shape, dtype))
        out_specs.append(pl.BlockSpec(block, index))
    outs = pl.pallas_call(
        _proj_kernel,
        grid=(b, s // tm),
        in_specs=[pl.BlockSpec((1, tm, d), lambda bi, i: (bi, i, 0)), pl.BlockSpec((1, d), fixed),
                  _resident((d, W_COLS), fixed)],
        out_specs=out_specs,
        out_shape=out_shape,
        compiler_params=pltpu.CompilerParams(dimension_semantics=("arbitrary", "arbitrary"),
                                             vmem_limit_bytes=VMEM_LIMIT),
        name="proj",
    )(x, norm_g.reshape(1, d), w16)
    return dict(zip([name for name, _, _ in PROJ_OUTPUTS], outs))


def _memkv_kernel(x_ref, g_ref, w_ref, k32, v32, k16, v16):
    x = x_ref[...]
    hn = (x * lax.rsqrt(jnp.mean(x * x, axis=-1, keepdims=True) + EPS) * g_ref[...]).astype(BF16)
    zk = jnp.dot(hn, w_ref[:, :WIDTH], preferred_element_type=F32)
    zv = jnp.dot(hn, w_ref[:, WIDTH:], preferred_element_type=F32)
    k32[...] = zk
    v32[...] = zv
    k16[...] = zk.astype(BF16)
    v16[...] = zv.astype(BF16)


def _memkv(mem2d, g, w16, tm):
    n, d = mem2d.shape
    row = lambda i: (i, 0)
    fixed = lambda i: (0, 0)
    return pl.pallas_call(
        _memkv_kernel,
        grid=(n // tm,),
        in_specs=[pl.BlockSpec((tm, d), row), pl.BlockSpec((1, d), fixed),
                  pl.BlockSpec((d, 2 * WIDTH), fixed)],
        out_specs=[pl.BlockSpec((tm, WIDTH), row)] * 4,
        out_shape=[jax.ShapeDtypeStruct((n, WIDTH), F32)] * 2
        + [jax.ShapeDtypeStruct((n, WIDTH), BF16)] * 2,
        compiler_params=pltpu.CompilerParams(dimension_semantics=("arbitrary",)),
        name="memkv",
    )(mem2d, g.reshape(1, d), w16)


def _visible_chunks(q0, tq, tk):
    n_vis = (lax.shift_right_logical(q0 + tq - 1, CHUNK_SHIFT) + 1) * CHUNK
    return (n_vis + tk - 1) // tk


def _alibi_key_tile(tk):
    pos = np.arange(tk)
    lo = pos % POS_SPLIT
    tile = np.zeros((tk, LANES), np.float32)
    tile[:, 0:3] = lo[:, None]
    tile[:, 3:6] = (pos - lo)[:, None]
    return jnp.asarray(tile, BF16)


def _alibi_query_rows(slopes_row, width):
    row = lax.broadcasted_iota(I32, (LANES, 1), 0)
    piece = jnp.zeros((LANES, 1), F32)
    for j, p in enumerate(LOG2E_PIECES):
        piece = jnp.where(jnp.logical_or(row == j, row == j + 3), p, piece)
    return (piece * slopes_row).astype(BF16) + jnp.zeros((LANES, width), BF16)


def _split_halves(q_t, tq):
    row = lax.broadcasted_iota(I32, (PAIR, 1), 0)
    zero = jnp.zeros_like(q_t)
    return jnp.concatenate([jnp.where(row < PAIR // 2, q_t, zero),
                            jnp.where(row >= PAIR // 2, q_t, zero)], axis=1)


def _diag_distance(off, q0, tk, tq):
    kpos = off + lax.broadcasted_iota(I32, (tk, 1), 0)
    qpos = q0 + lax.broadcasted_iota(I32, (1, tq), 1)
    return jnp.maximum(kpos - qpos, 0).astype(F32), kpos, qpos


def _online_update(s, v_aug, chunk_base, m_ref, acc_ref, idx):
    m_old = m_ref[idx]
    m_new = jnp.maximum(m_old, jnp.max(s, axis=0, keepdims=True) + chunk_base)
    p = jnp.exp2(s - (m_new - chunk_base)).astype(BF16)
    pv = jnp.dot(v_aug, p, preferred_element_type=F32)
    acc_ref[idx] = jnp.exp2(m_old - m_new) * acc_ref[idx] + pv
    m_ref[idx] = m_new


def _largest_norm(q_t):
    q = q_t.astype(F32)
    return jnp.sqrt(jnp.max(jnp.sum(q * q, axis=0, keepdims=True)))


def _chunk_key_norms(k16, tk):
    b, lk, _ = k16.shape
    k = k16.astype(F32).reshape(b, lk // tk, tk, WIDTH // PAIR, PAIR)
    return jnp.swapaxes(jnp.sqrt(jnp.max(jnp.sum(k * k, axis=-1), axis=2)), 1, 2)


def _first_live_chunk(q_norm, k_norms, slope, ref_pos, tk):
    chunk = lax.broadcasted_iota(I32, k_norms.shape, 1)
    newest_key = ((chunk + 1) * tk - 1).astype(F32)
    bound = q_norm * k_norms * NORM_SLACK + 1.0 - slope * LOG2E * (ref_pos - newest_key)
    live = jnp.where(bound > -SKIP_EXP2, chunk, k_norms.shape[1])
    return jnp.min(live)


def _attend(n_groups, group_sets, n_chunks, tk, scores, values, base, frame, m_ref, acc_ref,
            p_ref):
    last = n_chunks - 1
    sub = SUB_KEYS if tk % SUB_KEYS == 0 else LANES

    def chunk_off(c):
        return pl.multiple_of(c * tk, tk)

    acc_ref[...] = jnp.zeros(acc_ref.shape, F32)

    def run_pipeline(groups, first):
        def weigh(c, diagonal, slot):
            off = chunk_off(c)
            for at, g in enumerate(groups):
                shift = frame(g) - base(g, off)
                for j in range(tk // sub):
                    s = scores(g, off, j, sub, diagonal)
                    p_ref[slot, at, j * sub:(j + 1) * sub, :] = jnp.exp2(s - shift).astype(BF16)

        def gather(c, slot):
            off = chunk_off(c)
            for at, g in enumerate(groups):
                acc_ref[g] += jnp.dot(values(g, off, tk), p_ref[slot, at],
                                      preferred_element_type=F32)

        def step(i, slot):
            gather(jnp.where(i == 0, last, first + i - 1), slot)
            weigh(first + i, False, 1 - slot)

        def two_steps(i2, carry):
            step(2 * i2, 0)
            step(2 * i2 + 1, 1)
            return carry

        count = last - first
        weigh(last, True, 0)
        lax.fori_loop(0, count // 2, two_steps, 0)

        @pl.when(lax.rem(count, 2) == 1)
        def _():
            step(count - 1, 0)

        gather(jnp.where(count == 0, last, last - 1), lax.rem(count, 2))

    for groups, first in group_sets:
        run_pipeline(groups, first)

    low, high, peak = None, None, None
    for g in range(n_groups):
        acc = acc_ref[g]
        sums = acc[acc.shape[0] - ONES_ROWS:acc.shape[0] - ONES_ROWS + 1]
        mag = jnp.max(jnp.abs(acc), axis=0, keepdims=True)
        low = sums if low is None else jnp.minimum(low, sums)
        high = sums if high is None else jnp.maximum(high, sums)
        peak = mag if peak is None else jnp.maximum(peak, mag)
    trusted = jnp.logical_and(jnp.min(low) >= 2.0 ** -MAX_EXP2,
                              jnp.logical_and(jnp.max(high) <= 2.0 ** MAX_EXP2,
                                              jnp.max(peak) <= 2.0 ** (2 * MAX_EXP2 - 1)))

    @pl.when(jnp.logical_not(trusted))
    def _():
        m_ref[...] = jnp.full(m_ref.shape, NEG, F32)
        acc_ref[...] = jnp.zeros(acc_ref.shape, F32)

        def chunk(c, diagonal):
            off = chunk_off(c)
            for g in range(n_groups):
                _online_update(scores(g, off, 0, tk, diagonal), values(g, off, tk),
                               base(g, off), m_ref, acc_ref, g)

        def body(c, carry):
            chunk(c, False)
            return carry

        lax.fori_loop(0, last, body, 0)
        chunk(last, True)


def _attn_a_kernel(lq1, lk1, lq2, lk2, subg_ref, e_ref, kn_ref, qt_ref, k_ref, vt_ref, o_ref,
                   m_sc, acc_sc, p_sc, *, tq, tk, q_pos0, lam_init):
    q0 = q_pos0 + pl.program_id(1) * tq
    n_chunks = _visible_chunks(q0, tq, tk)
    lam = (jnp.exp(jnp.sum(lq1[...] * lk1[...], axis=-1, keepdims=True))
           - jnp.exp(jnp.sum(lq2[...] * lk2[...], axis=-1, keepdims=True)) + lam_init)
    slopes = [_alibi_slope(h, A_HEADS) for h in range(A_HEADS)]
    q_aug, first = [], []
    for h in range(A_HEADS):
        q_t = qt_ref[0, h * PAIR:(h + 1) * PAIR, :]
        q_aug.append(jnp.concatenate([_split_halves(q_t, tq),
                                      _alibi_query_rows(slopes[h], 2 * tq)], axis=0))
        first.append(_first_live_chunk(_largest_norm(q_t), kn_ref[0, h:h + 1, :], slopes[h],
                                       q0.astype(F32), tk))
    group_sets = [((2 * i, 2 * i + 1), jnp.minimum(jnp.minimum(first[2 * i], first[2 * i + 1]),
                                                   n_chunks - 1))
                  for i in range(A_HEADS // 2)]
    ahead, kpos, qpos = _diag_distance((n_chunks - 1) * tk, q0, tk, tq)
    ahead = jnp.where(lax.shift_right_logical(kpos, CHUNK_SHIFT)
                      <= lax.shift_right_logical(qpos, CHUNK_SHIFT), ahead * (-2.0 * LOG2E), NEG)

    def scores(h, off, j, size, diagonal):
        hs = slice(h * PAIR, (h + 1) * PAIR)
        start = pl.multiple_of(off + j * size, size)
        k_aug = jnp.concatenate([k_ref[0, pl.ds(start, size), hs],
                                 e_ref[j * size:(j + 1) * size, :]], axis=1)
        s = jnp.dot(k_aug, q_aug[h], preferred_element_type=F32)
        if diagonal:
            fix = ahead[j * size:(j + 1) * size, :] * slopes[h]
            s = s + jnp.concatenate([fix, fix], axis=1)
        return s

    def values(h, start, size):
        start = pl.multiple_of(start, size)
        return vt_ref[0, h * V_ROWS:(h + 1) * V_ROWS, pl.ds(start, size)]

    def base(h, off):
        return off.astype(F32) * (slopes[h] * LOG2E)

    qpos2 = jnp.concatenate([qpos, qpos], axis=1).astype(F32)

    def frame(h):
        return qpos2 * (slopes[h] * LOG2E)

    _attend(A_HEADS, group_sets, n_chunks, tk, scores, values, base, frame, m_sc, acc_sc, p_sc)

    for h in range(A_HEADS):
        acc = acc_sc[h]
        o = acc[:PAIR] / acc[PAIR:PAIR + 1]
        o = o[:, :tq] - lam * o[:, tq:]
        o = o * lax.rsqrt(jnp.mean(o * o, axis=0, keepdims=True) + EPS) * subg_ref[...]
        o_ref[0, :, h * PAIR:(h + 1) * PAIR] = (o * (1.0 - lam_init)).T.astype(BF16)


def _attn_a(qa_t, ka16, va_t, lam_params, subln_g, *, tq, tk, q_pos0, lam_init):
    b, _, lq = qa_t.shape
    lkp = ka16.shape[1]
    small = lambda bi, i: (0, 0)
    return pl.pallas_call(
        functools.partial(_attn_a_kernel, tq=tq, tk=tk, q_pos0=q_pos0, lam_init=lam_init),
        grid=(b, lq // tq),
        in_specs=[pl.BlockSpec((1, A_DH), small)] * 4 + [
            pl.BlockSpec((2 * A_DH, 1), small), pl.BlockSpec((tk, LANES), small),
            pl.BlockSpec((1, A_HEADS, lkp // tk), lambda bi, i: (bi, 0, 0)),
            pl.BlockSpec((1, WIDTH, tq), lambda bi, i: (bi, 0, i)),
            _resident((1, lkp, WIDTH), lambda bi, i: (bi, 0, 0)),
            _resident((1, A_HEADS * V_ROWS, lkp), lambda bi, i: (bi, 0, 0))],
        out_specs=pl.BlockSpec((1, tq, WIDTH), lambda bi, i: (bi, i, 0)),
        out_shape=jax.ShapeDtypeStruct((b, lq, WIDTH), BF16),
        scratch_shapes=[pltpu.VMEM((A_HEADS, 1, 2 * tq), F32),
                        pltpu.VMEM((A_HEADS, PAIR + ONES_ROWS, 2 * tq), F32),
                        pltpu.VMEM((2, 2, tk, 2 * tq), BF16)],
        compiler_params=pltpu.CompilerParams(dimension_semantics=("arbitrary", "arbitrary"),
                                             vmem_limit_bytes=VMEM_LIMIT),
        name="attn_a",
    )(*lam_params, subln_g.reshape(2 * A_DH, 1), _alibi_key_tile(tk), _chunk_key_norms(ka16, tk),
      qa_t, ka16, va_t)


def _sum_keys(x):
    rows, tq = x.shape
    lanes_of_adds = 64
    if rows % lanes_of_adds == 0 and rows > lanes_of_adds:
        x = jnp.sum(x.reshape(rows // lanes_of_adds, lanes_of_adds, tq), axis=0)
    return jnp.sum(x, axis=0, keepdims=True)


def _count_true(hit):
    return _sum_keys(jnp.where(hit, 1.0, 0.0))


def _ordered_key(x):
    bits = pltpu.bitcast(x, I32)
    return bits ^ (lax.shift_right_arithmetic(bits, 31) & INT_MAX)


def _ordered_key_inv(k):
    return pltpu.bitcast(k ^ (lax.shift_right_arithmetic(k, 31) & INT_MAX), F32)


def _attn_b_kernel(e_ref, kn_ref, qbt_ref, qit_ref, wt_ref, kidx_ref, kb_ref, vbt_ref, o_ref,
                   sc, thr_sc, m_sc, acc_sc, p_sc, *, tq, tk, q_pos0, topk, idx_bits):
    q0 = q_pos0 + pl.program_id(1) * tq
    n_chunks = _visible_chunks(q0, tq, tk)
    last = n_chunks - 1
    topk_f = float(topk)
    qpos = q0 + lax.broadcasted_iota(I32, (1, tq), 1)
    qchunk = lax.shift_right_logical(qpos, CHUNK_SHIFT)
    n_valid = ((qchunk + 1) * CHUNK).astype(F32)

    def chunk_off(c):
        return pl.multiple_of(c * tk, tk)

    def key_pos(off):
        return off + lax.broadcasted_iota(I32, (tk, 1), 0)

    def visible(off):
        return lax.shift_right_logical(key_pos(off), CHUNK_SHIFT) <= qchunk

    qi_all = jnp.concatenate([qit_ref[0, h * IDX_DH:(h + 1) * IDX_DH, :]
                              for h in range(IDX_HEADS)], axis=1)
    w_rows = [wt_ref[0, h:h + 1, :] for h in range(IDX_HEADS)]

    sub = SUB_KEYS if tk % SUB_KEYS == 0 else LANES

    def merge_stats(a, b):
        return (jnp.maximum(a[0], b[0]), jnp.minimum(a[1], b[1]), a[2] + b[2], a[3] + b[3])

    def score_chunk(c, masked):
        stats = None
        for j in range(tk // sub):
            start = pl.multiple_of(c * tk + j * sub, sub)
            logits = jnp.dot(kidx_ref[0, pl.ds(start, sub), :], qi_all,
                             preferred_element_type=F32)
            score = jnp.zeros((sub, tq), F32)
            for h in range(IDX_HEADS):
                score = score + jnp.maximum(logits[:, h * tq:(h + 1) * tq], 0.0) * w_rows[h]
            lowest = score
            if masked:
                vis = (lax.shift_right_logical(start + lax.broadcasted_iota(I32, (sub, 1), 0),
                                               CHUNK_SHIFT) <= qchunk)
                score = jnp.where(vis, score, -jnp.inf)
                lowest = jnp.where(vis, score, jnp.inf)
            sc[pl.ds(start, sub), :] = score
            part = (jnp.max(score, axis=0, keepdims=True), jnp.min(lowest, axis=0, keepdims=True),
                    _count_true(score >= 0.0), _count_true(score > 0.0))
            stats = part if stats is None else merge_stats(stats, part)
        return stats

    def score_two(i2, st):
        st = merge_stats(st, score_chunk(2 * i2, False))
        return merge_stats(st, score_chunk(2 * i2 + 1, False))

    stats = lax.fori_loop(
        0, last // 2, score_two,
        (jnp.full((1, tq), -jnp.inf, F32), jnp.full((1, tq), jnp.inf, F32),
         jnp.zeros((1, tq), F32), jnp.zeros((1, tq), F32)))
    stats = lax.cond(lax.rem(last, 2) == 1,
                     lambda st: merge_stats(st, score_chunk(last - 1, False)),
                     lambda st: st, stats)
    row_max, row_min, n_ge0, n_pos = merge_stats(stats, score_chunk(last, True))

    def count(pred):
        def one(c):
            off = chunk_off(c)
            return _count_true(pred(sc[pl.ds(off, tk), :], off))

        def body(i, acc):
            for u in range(SCAN_CHUNKS):
                c = SCAN_CHUNKS * i + u
                part = one(jnp.minimum(c, last))
                acc = acc + (part if u == 0 else jnp.where(c <= last, 1.0, 0.0) * part)
            return acc
        return lax.fori_loop(0, (n_chunks + SCAN_CHUNKS - 1) // SCAN_CHUNKS, body,
                             jnp.zeros((1, tq), F32))

    def count_ge(t):
        return count(lambda s, off: s >= t)

    def spread(cnt):
        c = jnp.clip(cnt, 0.5, n_valid - 0.5)
        return jnp.log2(c / (n_valid - c))

    target = spread(jnp.full((1, tq), topk_f, F32))
    key_lo, key_hi = _ordered_key(row_min), _ordered_key(row_max) + 1
    cnt_lo = n_valid
    t_a, f_a = row_min, spread(n_valid) - target
    t_b, f_b = row_max, spread(jnp.zeros((1, tq), F32)) - target
    zero_up = n_ge0 >= topk_f
    key_lo = jnp.where(zero_up, jnp.maximum(key_lo, 0), key_lo)
    cnt_lo = jnp.where(zero_up, n_ge0, cnt_lo)
    t_a = jnp.where(zero_up, 0.0, t_a)
    f_a = jnp.where(zero_up, spread(n_ge0) - target, f_a)
    zero_down = n_pos < topk_f
    key_hi = jnp.where(zero_down, jnp.minimum(key_hi, 1), key_hi)
    t_b = jnp.where(zero_down, 0.0, t_b)
    f_b = jnp.where(zero_down, spread(n_pos) - target, f_b)
    state0 = (jnp.int32(0), key_lo, key_hi, cnt_lo, t_a, f_a, t_b, f_b, jnp.zeros((1, tq), F32))

    def span(klo, khi):
        return khi - klo

    def finished(klo, khi, cnt_lo):
        d = span(klo, khi)
        tight = jnp.logical_or(d == 0, d == 1)
        return jnp.logical_or(jnp.logical_or(cnt_lo == topk_f, tight), n_valid <= topk_f)

    def search_cond(st):
        it, klo, khi, cnt_lo = st[:4]
        todo = jnp.where(finished(klo, khi, cnt_lo), 0.0, 1.0)
        return jnp.logical_and(it < MAX_SEARCH_STEPS, jnp.max(todo) > 0.0)

    def search_body(st):
        it, klo, khi, cnt_lo, ta, fa, tb, fb, side = st
        done = finished(klo, khi, cnt_lo)
        guess = ta + (tb - ta) * (fa / (fa - fb))
        guess = jnp.where(guess == guess, guess, ta)
        guess = jnp.clip(guess, -3e38, 3e38)
        mid = klo + lax.shift_right_logical(span(klo, khi), 1)
        cand = jnp.where(it % BISECT_EVERY == BISECT_EVERY - 1, mid, _ordered_key(guess))
        cand = jnp.minimum(jnp.maximum(cand, klo + 1), khi - 1)
        cand = jnp.where(done, klo, cand)
        t = _ordered_key_inv(cand)
        cnt = count_ge(t)
        up = jnp.logical_and(cnt >= topk_f, jnp.logical_not(done))
        down = jnp.logical_and(cnt < topk_f, jnp.logical_not(done))
        f = spread(cnt) - target
        fb = jnp.where(jnp.logical_and(up, side > 0.0), 0.5 * fb, fb)
        fa = jnp.where(jnp.logical_and(down, side < 0.0), 0.5 * fa, fa)
        return (it + 1, jnp.where(up, cand, klo), jnp.where(down, cand, khi),
                jnp.where(up, cnt, cnt_lo),
                jnp.where(up, t, ta), jnp.where(up, f, fa),
                jnp.where(down, t, tb), jnp.where(down, f, fb),
                jnp.where(up, 1.0, jnp.where(down, -1.0, side)))

    st = lax.while_loop(search_cond, lambda st: search_body(search_body(st)), state0)
    key_lo, cnt_lo = st[1], st[3]
    thr = _ordered_key_inv(key_lo)

    thr_sc[...] = jnp.full((1, tq), INT_MAX, I32)
    tied = jnp.where(jnp.logical_and(cnt_lo != topk_f, n_valid > topk_f), 1.0, 0.0)

    @pl.when(jnp.max(tied) > 0.0)
    def _():
        room = topk_f - count(lambda s, off: s > thr)
        jmax = jnp.zeros((1, tq), I32)
        for bit in range(idx_bits - 1, -1, -1):
            cand = jmax | (1 << bit)
            g = count(lambda s, off: jnp.logical_and(s == thr, key_pos(off) <= cand))
            jmax = jnp.where(g <= room, cand, jmax)
        thr_sc[...] = jmax

    jmax = thr_sc[...]

    ahead, _, _ = _diag_distance(chunk_off(last), q0, tk, tq)

    def mask_chunk(c, masked):
        off = chunk_off(c)
        s = sc[pl.ds(off, tk), :]
        sel = jnp.logical_or(s > thr, jnp.logical_and(s == thr, key_pos(off) <= jmax))
        eff = key_pos(off).astype(F32)
        if masked:
            sel = jnp.logical_and(sel, visible(off))
            eff = eff - 2.0 * ahead
        sc[pl.ds(off, tk), :] = jnp.where(sel, 0.0, NEG)
        return jnp.max(jnp.where(sel, eff, -jnp.inf), axis=0, keepdims=True)

    nearest = lax.fori_loop(0, last, lambda c, n: jnp.maximum(n, mask_chunk(c, False)),
                            jnp.full((1, tq), -jnp.inf, F32))
    nearest = jnp.maximum(nearest, mask_chunk(last, True))

    lane2 = lax.broadcasted_iota(I32, (1, 2 * tq), 1)
    n_pairs = B_HEADS // 2
    slope_rows, q_aug, first = [], [], []
    nearest_min = jnp.min(nearest)
    for pi in range(n_pairs):
        slope_row = jnp.where(lane2 < tq, _alibi_slope(2 * pi, B_HEADS),
                              _alibi_slope(2 * pi + 1, B_HEADS)).astype(F32)
        slope_rows.append(slope_row)
        q_t = qbt_ref[0, pi * PAIR:(pi + 1) * PAIR, :]
        q_aug.append(jnp.concatenate([_split_halves(q_t, tq),
                                      _alibi_query_rows(slope_row, 2 * tq)], axis=0))
        first.append(_first_live_chunk(_largest_norm(q_t), kn_ref[0, pi:pi + 1, :],
                                       _alibi_slope(2 * pi + 1, B_HEADS), nearest_min, tk))
    group_sets = [((2 * i, 2 * i + 1), jnp.minimum(jnp.minimum(first[2 * i], first[2 * i + 1]),
                                                   last))
                  for i in range(n_pairs // 2)]
    ahead2 = jnp.concatenate([ahead, ahead], axis=1) * (-2.0 * LOG2E)
    nearest2 = jnp.concatenate([nearest, nearest], axis=1)

    def scores(pi, off, j, size, diagonal):
        start = pl.multiple_of(off + j * size, size)
        k_aug = jnp.concatenate([kb_ref[0, pl.ds(start, size), pi * PAIR:(pi + 1) * PAIR],
                                 e_ref[j * size:(j + 1) * size, :]], axis=1)
        mask = sc[pl.ds(start, size), :]
        s = (jnp.dot(k_aug, q_aug[pi], preferred_element_type=F32)
             + jnp.concatenate([mask, mask], axis=1))
        if diagonal:
            s = s + ahead2[j * size:(j + 1) * size, :] * slope_rows[pi]
        return s

    def frame(pi):
        return nearest2 * (slope_rows[pi] * LOG2E)

    def values(pi, start, size):
        start = pl.multiple_of(start, size)
        return vbt_ref[0, pi * V_ROWS:(pi + 1) * V_ROWS, pl.ds(start, size)]

    def base(pi, off):
        return off.astype(F32) * (slope_rows[pi] * LOG2E)

    _attend(n_pairs, group_sets, n_chunks, tk, scores, values, base, frame, m_sc, acc_sc, p_sc)

    for pi in range(n_pairs):
        acc = acc_sc[pi]
        o = acc[:PAIR] / acc[PAIR:PAIR + 1]
        o = jnp.concatenate([o[:B_DH, :tq], o[B_DH:, tq:]], axis=0)
        o_ref[0, :, pi * PAIR:(pi + 1) * PAIR] = o.T.astype(BF16)


def _attn_b(qb_t, qi_t, w_t, kidx16, kb16, vb_t, *, tq, tk, q_pos0, topk):
    b, _, lq = qb_t.shape
    lkp = kb16.shape[1]
    q_blk = lambda rows: pl.BlockSpec((1, rows, tq), lambda bi, i: (bi, 0, i))
    per_batch = lambda bi, i: (bi, 0, 0)
    return pl.pallas_call(
        functools.partial(_attn_b_kernel, tq=tq, tk=tk, q_pos0=q_pos0, topk=topk,
                          idx_bits=max(1, (lkp - 1).bit_length())),
        grid=(b, lq // tq),
        in_specs=[pl.BlockSpec((tk, LANES), lambda bi, i: (0, 0)),
                  pl.BlockSpec((1, B_HEADS // 2, lkp // tk), per_batch),
                  q_blk(WIDTH), q_blk(WIDTH), q_blk(IDX_HEADS),
                  _resident((1, lkp, IDX_DH), per_batch),
                  _resident((1, lkp, WIDTH), per_batch),
                  _resident((1, B_HEADS // 2 * V_ROWS, lkp), per_batch)],
        out_specs=pl.BlockSpec((1, tq, WIDTH), lambda bi, i: (bi, i, 0)),
        out_shape=jax.ShapeDtypeStruct((b, lq, WIDTH), BF16),
        scratch_shapes=[pltpu.VMEM((lkp, tq), F32), pltpu.VMEM((1, tq), I32),
                        pltpu.VMEM((B_HEADS // 2, 1, 2 * tq), F32),
                        pltpu.VMEM((B_HEADS // 2, PAIR + ONES_ROWS, 2 * tq), F32),
                        pltpu.VMEM((2, 2, tk, 2 * tq), BF16)],
        compiler_params=pltpu.CompilerParams(dimension_semantics=("arbitrary", "arbitrary"),
                                             vmem_limit_bytes=VMEM_LIMIT),
        name="attn_b",
    )(_alibi_key_tile(tk), _chunk_key_norms(kb16, tk), qb_t, qi_t, w_t, kidx16, kb16, vb_t)


def _nt_dot(a, b):
    return lax.dot_general(a, b, (((1,), (1,)), ((), ())), preferred_element_type=F32)


def _post_kernel(x_ref, oa_ref, ob_ref, qm_ref, ga_ref, gb_ref, gm_ref, gates_ref, mk_ref,
                 mv_ref, bg_ref, wa_ref, wb_ref, wm_ref, wo_ref, fg_ref, y_ref):
    d = x_ref.shape[-1]
    om = []
    for h in range(M_HEADS):
        hs = slice(h * M_DH, (h + 1) * M_DH)
        s = _nt_dot(qm_ref[0, :, hs], mk_ref[0, :, hs])
        p = jnp.exp2(s - jnp.max(s, axis=-1, keepdims=True))
        o = jnp.dot(p.astype(BF16), mv_ref[0, :, hs], preferred_element_type=F32)
        om.append(o / jnp.sum(p, axis=-1, keepdims=True))
    om = jnp.concatenate(om, axis=1)

    def branch(gate_ref, o, w_ref):
        g = gate_ref[0].astype(F32)
        return jnp.dot((g * jax.nn.sigmoid(g) * o).astype(BF16), w_ref[...],
                       preferred_element_type=F32)

    pa = branch(ga_ref, oa_ref[0].astype(F32), wa_ref)
    pb = branch(gb_ref, ob_ref[0].astype(F32), wb_ref)
    pm = branch(gm_ref, om, wm_ref)
    g = jax.nn.sigmoid(gates_ref[0].astype(F32) + bg_ref[...])
    merged = g[:, :d] * pa + g[:, d:2 * d] * pb + g[:, 2 * d:] * pm
    y = x_ref[0] + jnp.dot(merged.astype(BF16), wo_ref[...], preferred_element_type=F32)
    y_ref[0] = y * lax.rsqrt(jnp.mean(y * y, axis=-1, keepdims=True) + EPS) * fg_ref[...]


def _post(x, oa, ob, qm, ga, gb, gm, gates, mk16, mv16, b_gate, wa, wb, wm, wo, final_g, tm):
    b, lq, d = x.shape
    n_mem = mk16.shape[1]
    blk = lambda w: pl.BlockSpec((1, tm, w), lambda bi, i: (bi, i, 0))
    per_batch = pl.BlockSpec((1, n_mem, WIDTH), lambda bi, i: (bi, 0, 0))
    full = lambda r, c: pl.BlockSpec((r, c), lambda bi, i: (0, 0))
    return pl.pallas_call(
        _post_kernel,
        grid=(b, lq // tm),
        in_specs=[blk(d)] + [blk(WIDTH)] * 6 + [blk(N_BRANCH * d), per_batch, per_batch,
                  full(1, N_BRANCH * d), full(WIDTH, d), full(WIDTH, d), full(WIDTH, d),
                  full(d, d), full(1, d)],
        out_specs=blk(d),
        out_shape=jax.ShapeDtypeStruct((b, lq, d), F32),
        compiler_params=pltpu.CompilerParams(dimension_semantics=("arbitrary", "arbitrary"),
                                             vmem_limit_bytes=VMEM_LIMIT),
        name="post",
    )(x, oa, ob, qm, ga, gb, gm, gates, mk16, mv16, b_gate.reshape(1, -1), wa, wb, wm, wo,
      final_g.reshape(1, d))


def _values_on_lanes(v16):
    b, lk, _ = v16.shape
    v_t = jnp.swapaxes(v16, 1, 2).reshape(b, WIDTH // PAIR, PAIR, lk)
    ones = jnp.ones((b, WIDTH // PAIR, ONES_ROWS, lk), v16.dtype)
    return jnp.concatenate([v_t, ones], axis=2).reshape(b, WIDTH // PAIR * V_ROWS, lk)


def _pad_axis(x, size, axis):
    pad = size - x.shape[axis]
    if pad == 0:
        return x
    widths = [(0, 0)] * x.ndim
    widths[axis] = (0, pad)
    return jnp.pad(x, widths)


def _layer(x, past, mk16, mv16, q_pos0, norm_g, w16, b_gate, lam_params, lam_init, subln_g,
           wa, wb, wm, wo, final_g, *, tm, tq_a, tq, tk):
    b, lq, d = x.shape
    p = _proj(x, norm_g, w16, tm)
    new_rows = (
        jnp.transpose(p["ka_t32"].reshape(b, A_HEADS, 2, A_DH, lq), (0, 4, 1, 2, 3)),
        p["va4"].reshape(b, lq, A_HEADS, 2 * A_DH),
        jnp.transpose(p["kb_t32"].reshape(b, B_HEADS, B_DH, lq), (0, 3, 1, 2)),
        jnp.transpose(p["vb_t32"].reshape(b, B_HEADS, B_DH, lq), (0, 3, 1, 2)),
        jnp.swapaxes(p["kw_t32"][:, :IDX_DH], 1, 2))
    p["w_t"] = p["kw_t32"][:, IDX_DH:IDX_DH + IDX_HEADS]
    ka16, kb16, kidx16, va_t, vb_t = p["ka16"], p["kb16"], p["kidx16"], p["va_t"], p["vb_t"]
    if past is not None:
        pa_k, pa_v, pb_k, pb_v, p_kidx = past
        n_past = pa_k.shape[1]
        flat16 = lambda t: t.reshape(b, n_past, -1).astype(BF16)
        rows = lambda old, new: jnp.concatenate([flat16(old), new], axis=1)
        lanes = lambda old, new: jnp.concatenate([_values_on_lanes(flat16(old)), new], axis=2)
        ka16, kb16, kidx16 = rows(pa_k, ka16), rows(pb_k, kb16), rows(p_kidx, kidx16)
        va_t, vb_t = lanes(pa_v, va_t), lanes(pb_v, vb_t)
    lk = ka16.shape[1]
    topk = min(TOPK_MAX, lk // 4)
    lkp = -(-lk // tk) * tk
    ka16, kb16, kidx16 = [_pad_axis(t, lkp, 1) for t in (ka16, kb16, kidx16)]
    va_t, vb_t = _pad_axis(va_t, lkp, 2), _pad_axis(vb_t, lkp, 2)
    lqp = -(-lq // max(tq_a, tq)) * max(tq_a, tq)
    qa_t, qb_t, qi_t, w_t = [_pad_axis(p[n], lqp, 2) for n in ("qa_t", "qb_t", "qi_t", "w_t")]

    oa = _attn_a(qa_t, ka16, va_t, lam_params, subln_g, tq=tq_a, tk=tk, q_pos0=q_pos0,
                 lam_init=lam_init)[:, :lq]
    ob = _attn_b(qb_t, qi_t, w_t, kidx16, kb16, vb_t, tq=tq, tk=tk, q_pos0=q_pos0,
                 topk=topk)[:, :lq]
    y = _post(x, oa, ob, p["qm"], p["ga"], p["gb"], p["gm"], p["gates"], mk16, mv16, b_gate,
              wa, wb, wm, wo, final_g, tm)
    return y, new_rows


def kernel(x_prompt, x_sample, mem_prompt, cache_a_k, cache_a_v, cache_b_k, cache_b_v, cache_b_kidx, cache_mem_k, cache_mem_v, norm_g, w_in, b_gate, lam_q1, lam_k1, lam_q2, lam_k2, subln_g, mem_norm_g, w_mem_kv, w_br_a, w_br_b, w_br_m, w_out, final_g):
    depth, d, _ = w_in.shape
    assert depth == 1, "single-layer step only"
    l = 0
    lam_init = 0.8 - 0.6 * math.exp(-0.3 * l)
    w16 = jnp.concatenate([w_in[l][:, :IN_KW_END], jnp.zeros((d, KW_PAD), w_in.dtype),
                           w_in[l][:, IN_KW_END:]], axis=1).astype(BF16)
    lam_params = [p[l].reshape(1, A_DH) for p in (lam_q1, lam_k1, lam_q2, lam_k2)]
    shared = (norm_g[l], w16, b_gate[l], lam_params, lam_init, subln_g[l],
              w_br_a[l].astype(BF16), w_br_b[l].astype(BF16), w_br_m[l].astype(BF16),
              w_out[l].astype(BF16), final_g)

    bp, n_mem, _ = mem_prompt.shape
    mk32, mv32, mk16, mv16 = _memkv(mem_prompt.reshape(bp * n_mem, d), mem_norm_g[l],
                                    w_mem_kv[l].astype(BF16), tm=256)
    y_p, rows_p = _layer(x_prompt, None, mk16.reshape(bp, n_mem, WIDTH),
                         mv16.reshape(bp, n_mem, WIDTH), 0, *shared, tm=256, tq_a=256, tq=128,
                         tk=1024)

    bs, n_past = cache_a_k.shape[1], cache_a_k.shape[2]
    past = (cache_a_k[l], cache_a_v[l], cache_b_k[l], cache_b_v[l], cache_b_kidx[l])
    ls = x_sample.shape[1]
    y_s, rows_s = _layer(x_sample, past, cache_mem_k[l].reshape(bs, n_mem, WIDTH).astype(BF16),
                         cache_mem_v[l].reshape(bs, n_mem, WIDTH).astype(BF16), n_past, *shared,
                         tm=ls, tq_a=128, tq=128, tk=384)

    mem_shape = (1, bp, n_mem, M_HEADS, M_DH)
    return (y_p, y_s, *[r[None] for r in rows_p], mk32.reshape(mem_shape),
            mv32.reshape(mem_shape), *[r[None] for r in rows_s])
```

```python
import functools
import math

import jax
import jax.numpy as jnp
import numpy as np
from jax import lax
from jax.experimental import pallas as pl
from jax.experimental.pallas import tpu as pltpu

F32 = jnp.float32
BF16 = jnp.bfloat16
I32 = jnp.int32

EPS = 1e-6
CHUNK = 64
CHUNK_SHIFT = 6
A_HEADS, A_DH = 4, 64
B_HEADS, B_DH = 8, 64
IDX_HEADS, IDX_DH = 8, 64
M_HEADS, M_DH = 4, 128
N_BRANCH = 3
TOPK_MAX = 256
WIDTH = 512
LANES = 128
PAIR = 128
ONES_ROWS = 16
V_ROWS = PAIR + ONES_ROWS
LOG2E = 1.4426950408889634
NEG = -1e30
INT_MAX = 2147483647
KEY_NEG_INF = -2139095041
VMEM_LIMIT = 60 * 1024 * 1024
BISECT_EVERY = 8
MAX_SEARCH_STEPS = BISECT_EVERY * 34
SCAN_CHUNKS = 2
SUB_KEYS = 256
MAX_EXP2 = 64.0
SKIP_EXP2 = 160.0
NORM_SLACK = 1.01

C_QA, C_KA, C_VA, C_GA = 0, 512, 1024, 1536
C_QB, C_KB, C_VB, C_GB = 2048, 2560, 3072, 3584
C_QI, C_KW, C_QM, C_GM, C_GATES = 4096, 4608, 4736, 5248, 5760
KW_PAD = LANES - IDX_DH - IDX_HEADS
W_COLS = C_GATES + N_BRANCH * 1024
IN_KW_END = 4680


def _bf16_pieces(x, n):
    out, rest = [], float(x)
    for _ in range(n):
        p = float(np.asarray(rest, np.float32).astype(BF16).astype(np.float32))
        out.append(p)
        rest -= p
    return out


LOG2E_PIECES = _bf16_pieces(LOG2E, 3)
POS_SPLIT = 256


def _resident(block_shape, index_map):
    return pl.BlockSpec(block_shape, index_map, pipeline_mode=pl.Buffered(1))


def _alibi_slope(h, n):
    return 2.0 ** (-8.0 * (h + 1) / n)


PROJ_OUTPUTS = (
    ("ka16", BF16, ("rows", WIDTH)), ("kb16", BF16, ("rows", WIDTH)),
    ("ga", BF16, ("rows", WIDTH)), ("gb", BF16, ("rows", WIDTH)), ("gm", BF16, ("rows", WIDTH)),
    ("qm", BF16, ("rows", WIDTH)), ("gates", BF16, ("rows", N_BRANCH * 1024)),
    ("kidx16", BF16, ("rows", IDX_DH)),
    ("qa_t", BF16, ("lanes", WIDTH)), ("qb_t", BF16, ("lanes", WIDTH)),
    ("qi_t", BF16, ("lanes", WIDTH)),
    ("va_t", BF16, ("lanes", A_HEADS * V_ROWS)), ("vb_t", BF16, ("lanes", B_HEADS // 2 * V_ROWS)),
    ("ka_t32", F32, ("lanes", WIDTH)), ("kb_t32", F32, ("lanes", WIDTH)),
    ("vb_t32", F32, ("lanes", WIDTH)), ("kw_t32", F32, ("lanes", LANES)),
    ("va4", F32, ("split", LANES)),
    ("kn", F32, ("norms", LANES)),
)


def _proj_kernel(x_ref, g_ref, w_ref, *out_refs):
    o = dict(zip([name for name, _, _ in PROJ_OUTPUTS], out_refs))
    tm = x_ref.shape[1]
    x = x_ref[0]
    hn = (x * lax.rsqrt(jnp.mean(x * x, axis=-1, keepdims=True) + EPS) * g_ref[...]).astype(BF16)

    def mm(c0, width):
        return jnp.dot(hn, w_ref[:, c0:c0 + width], preferred_element_type=F32)

    o["qa_t"][0] = (mm(C_QA, WIDTH) * (A_DH ** -0.5 * LOG2E)).T.astype(BF16)
    o["qb_t"][0] = (mm(C_QB, WIDTH) * (B_DH ** -0.5 * LOG2E)).T.astype(BF16)
    o["qi_t"][0] = (mm(C_QI, WIDTH) * (IDX_DH ** -0.5)).T.astype(BF16)
    o["qm"][0] = (mm(C_QM, WIDTH) * (M_DH ** -0.5 * LOG2E)).astype(BF16)
    norms = []
    for c0, k16, k_t32 in ((C_KA, "ka16", "ka_t32"), (C_KB, "kb16", "kb_t32")):
        z = mm(c0, WIDTH)
        o[k16][0] = z.astype(BF16)
        o[k_t32][0] = z.T
        for h in range(WIDTH // PAIR):
            z_h = z[:, h * PAIR:(h + 1) * PAIR]
            top = jnp.max(jnp.sum(z_h * z_h, axis=1, keepdims=True), axis=0, keepdims=True)
            norms.append(jnp.broadcast_to(jnp.sqrt(top), (1, LANES)))
    o["kn"][0, 0] = jnp.concatenate(norms, axis=0)
    ones = jnp.ones((ONES_ROWS, tm), BF16)
    for c0, v_t, v_t32 in ((C_VA, "va_t", None), (C_VB, "vb_t", "vb_t32")):
        z = mm(c0, WIDTH)
        for h in range(WIDTH // PAIR):
            z_h = z[:, h * PAIR:(h + 1) * PAIR]
            o[v_t][0, h * V_ROWS:h * V_ROWS + PAIR, :] = z_h.T.astype(BF16)
            o[v_t][0, h * V_ROWS + PAIR:(h + 1) * V_ROWS, :] = ones
            if v_t32 is None:
                o["va4"][0, pl.ds(h, tm, stride=A_HEADS), :] = z_h
        if v_t32 is not None:
            o[v_t32][0] = z.T
    for c0, name in ((C_GA, "ga"), (C_GB, "gb"), (C_GM, "gm")):
        o[name][0] = mm(c0, WIDTH).astype(BF16)
    kw = mm(C_KW, LANES)
    o["kidx16"][0] = kw[:, :IDX_DH].astype(BF16)
    o["kw_t32"][0] = kw.T
    for c in range(N_BRANCH * 1024 // WIDTH):
        o["gates"][0, :, c * WIDTH:(c + 1) * WIDTH] = mm(C_GATES + c * WIDTH, WIDTH).astype(BF16)


def _proj(x, norm_g, w16, tm):
    b, s, d = x.shape
    fixed = lambda bi, i: (0, 0)
    out_shape, out_specs = [], []
    for _, dtype, (kind, n) in PROJ_OUTPUTS:
        if kind == "rows":
            shape, block, index = (b, s, n), (1, tm, n), (lambda bi, i: (bi, i, 0))
        elif kind == "lanes":
            shape, block, index = (b, n, s), (1, n, tm), (lambda bi, i: (bi, 0, i))
        elif kind == "norms":
            shape, block, index = ((b, s // tm, 2 * (WIDTH // PAIR), n),
                                   (1, 1, 2 * (WIDTH // PAIR), n), (lambda bi, i: (bi, i, 0, 0)))
        else:
            shape, block, index = ((b, s * A_HEADS, n), (1, tm * A_HEADS, n),
                                   (lambda bi, i: (bi, i, 0)))
        out_shape.append(jax.ShapeDtypeStruct(shape, dtype))
        out_specs.append(pl.BlockSpec(block, index))
    outs = pl.pallas_call(
        _proj_kernel,
        grid=(b, s // tm),
        in_specs=[pl.BlockSpec((1, tm, d), lambda bi, i: (bi, i, 0)), pl.BlockSpec((1, d), fixed),
                  _resident((d, W_COLS), fixed)],
        out_specs=out_specs,
        out_shape=out_shape,
        compiler_params=pltpu.CompilerParams(dimension_semantics=("arbitrary", "arbitrary"),
                                             vmem_limit_bytes=VMEM_LIMIT),
        name="proj",
    )(x, norm_g.reshape(1, d), w16)
    return dict(zip([name for name, _, _ in PROJ_OUTPUTS], outs))


def _memkv_kernel(x_ref, g_ref, w_ref, k32, v32, k16, v16):
    x = x_ref[...]
    hn = (x * lax.rsqrt(jnp.mean(x * x, axis=-1, keepdims=True) + EPS) * g_ref[...]).astype(BF16)
    zk = jnp.dot(hn, w_ref[:, :WIDTH], preferred_element_type=F32)
    zv = jnp.dot(hn, w_ref[:, WIDTH:], preferred_element_type=F32)
    k32[...] = zk
    v32[...] = zv
    k16[...] = zk.astype(BF16)
    v16[...] = zv.astype(BF16)


def _memkv(mem2d, g, w16, tm):
    n, d = mem2d.shape
    row = lambda i: (i, 0)
    fixed = lambda i: (0, 0)
    return pl.pallas_call(
        _memkv_kernel,
        grid=(n // tm,),
        in_specs=[pl.BlockSpec((tm, d), row), pl.BlockSpec((1, d), fixed),
                  pl.BlockSpec((d, 2 * WIDTH), fixed)],
        out_specs=[pl.BlockSpec((tm, WIDTH), row)] * 4,
        out_shape=[jax.ShapeDtypeStruct((n, WIDTH), F32)] * 2
        + [jax.ShapeDtypeStruct((n, WIDTH), BF16)] * 2,
        compiler_params=pltpu.CompilerParams(dimension_semantics=("arbitrary",)),
        name="memkv",
    )(mem2d, g.reshape(1, d), w16)


def _visible_chunks(q0, tq, tk):
    n_vis = (lax.shift_right_logical(q0 + tq - 1, CHUNK_SHIFT) + 1) * CHUNK
    return (n_vis + tk - 1) // tk


def _alibi_key_tile(tk):
    pos = np.arange(tk)
    lo = pos % POS_SPLIT
    tile = np.zeros((tk, LANES), np.float32)
    tile[:, 0:3] = lo[:, None]
    tile[:, 3:6] = (pos - lo)[:, None]
    return jnp.asarray(tile, BF16)


def _alibi_query_rows(slopes_row, width):
    row = lax.broadcasted_iota(I32, (LANES, 1), 0)
    piece = jnp.zeros((LANES, 1), F32)
    for j, p in enumerate(LOG2E_PIECES):
        piece = jnp.where(jnp.logical_or(row == j, row == j + 3), p, piece)
    return (piece * slopes_row).astype(BF16) + jnp.zeros((LANES, width), BF16)


def _split_halves(q_t, tq):
    row = lax.broadcasted_iota(I32, (PAIR, 1), 0)
    zero = jnp.zeros_like(q_t)
    return jnp.concatenate([jnp.where(row < PAIR // 2, q_t, zero),
                            jnp.where(row >= PAIR // 2, q_t, zero)], axis=1)


def _diag_distance(off, q0, tk, tq):
    kpos = off + lax.broadcasted_iota(I32, (tk, 1), 0)
    qpos = q0 + lax.broadcasted_iota(I32, (1, tq), 1)
    return jnp.maximum(kpos - qpos, 0).astype(F32), kpos, qpos


def _online_update(s, v_aug, chunk_base, m_ref, acc_ref, idx):
    m_old = m_ref[idx]
    m_new = jnp.maximum(m_old, jnp.max(s, axis=0, keepdims=True) + chunk_base)
    p = jnp.exp2(s - (m_new - chunk_base)).astype(BF16)
    pv = jnp.dot(v_aug, p, preferred_element_type=F32)
    acc_ref[idx] = jnp.exp2(m_old - m_new) * acc_ref[idx] + pv
    m_ref[idx] = m_new


def _largest_norm(q_t):
    q = q_t.astype(F32)
    return jnp.sqrt(jnp.max(jnp.sum(q * q, axis=0, keepdims=True)))


def _chunk_key_norms(k16, tk):
    b, lk, _ = k16.shape
    k = k16.astype(F32).reshape(b, lk // tk, tk, WIDTH // PAIR, PAIR)
    return jnp.swapaxes(jnp.sqrt(jnp.max(jnp.sum(k * k, axis=-1), axis=2)), 1, 2)


def _first_live_chunk(q_norm, k_norms, slope, ref_pos, tk):
    chunk = lax.broadcasted_iota(I32, k_norms.shape, 1)
    newest_key = ((chunk + 1) * tk - 1).astype(F32)
    bound = q_norm * k_norms * NORM_SLACK + 1.0 - slope * LOG2E * (ref_pos - newest_key)
    live = jnp.where(bound > -SKIP_EXP2, chunk, k_norms.shape[1])
    return jnp.min(live)


def _attend(n_groups, group_sets, n_chunks, tk, scores, values, base, frame, m_ref, acc_ref,
            p_ref):
    last = n_chunks - 1
    sub = SUB_KEYS if tk % SUB_KEYS == 0 else LANES

    def chunk_off(c):
        return pl.multiple_of(c * tk, tk)

    acc_ref[...] = jnp.zeros(acc_ref.shape, F32)

    def run_pipeline(groups, first):
        def weigh(c, diagonal, slot):
            off = chunk_off(c)
            for at, g in enumerate(groups):
                shift = frame(g) - base(g, off)
                for j in range(tk // sub):
                    s = scores(g, off, j, sub, diagonal)
                    p_ref[slot, at, j * sub:(j + 1) * sub, :] = jnp.exp2(s - shift).astype(BF16)

        def gather(c, slot):
            off = chunk_off(c)
            for at, g in enumerate(groups):
                acc_ref[g] += jnp.dot(values(g, off, tk), p_ref[slot, at],
                                      preferred_element_type=F32)

        def step(i, slot):
            gather(jnp.where(i == 0, last, first + i - 1), slot)
            weigh(first + i, False, 1 - slot)

        def two_steps(i2, carry):
            step(2 * i2, 0)
            step(2 * i2 + 1, 1)
            return carry

        def four_steps(i4, carry):
            two_steps(2 * i4, carry)
            return two_steps(2 * i4 + 1, carry)

        count = last - first
        weigh(last, True, 0)
        lax.fori_loop(0, count // 4, four_steps, 0)
        lax.fori_loop(2 * (count // 4), count // 2, two_steps, 0)

        @pl.when(lax.rem(count, 2) == 1)
        def _():
            step(count - 1, 0)

        gather(jnp.where(count == 0, last, last - 1), lax.rem(count, 2))

    for groups, first in group_sets:
        run_pipeline(groups, first)

    low, high, peak = None, None, None
    for g in range(n_groups):
        acc = acc_ref[g]
        sums = acc[acc.shape[0] - ONES_ROWS:acc.shape[0] - ONES_ROWS + 1]
        mag = jnp.max(jnp.abs(acc), axis=0, keepdims=True)
        low = sums if low is None else jnp.minimum(low, sums)
        high = sums if high is None else jnp.maximum(high, sums)
        peak = mag if peak is None else jnp.maximum(peak, mag)
    trusted = jnp.logical_and(jnp.min(low) >= 2.0 ** -MAX_EXP2,
                              jnp.logical_and(jnp.max(high) <= 2.0 ** MAX_EXP2,
                                              jnp.max(peak) <= 2.0 ** (2 * MAX_EXP2 - 1)))

    @pl.when(jnp.logical_not(trusted))
    def _():
        m_ref[...] = jnp.full(m_ref.shape, NEG, F32)
        acc_ref[...] = jnp.zeros(acc_ref.shape, F32)

        def chunk(c, diagonal):
            off = chunk_off(c)
            for g in range(n_groups):
                _online_update(scores(g, off, 0, tk, diagonal), values(g, off, tk),
                               base(g, off), m_ref, acc_ref, g)

        def body(c, carry):
            chunk(c, False)
            return carry

        lax.fori_loop(0, last, body, 0)
        chunk(last, True)


def _attn_a_kernel(lq1, lk1, lq2, lk2, subg_ref, e_ref, kn_ref, qt_ref, k_ref, vt_ref, o_ref,
                   m_sc, acc_sc, p_sc, *, tq, tk, q_pos0, lam_init):
    q0 = q_pos0 + pl.program_id(1) * tq
    n_chunks = _visible_chunks(q0, tq, tk)
    lam = (jnp.exp(jnp.sum(lq1[...] * lk1[...], axis=-1, keepdims=True))
           - jnp.exp(jnp.sum(lq2[...] * lk2[...], axis=-1, keepdims=True)) + lam_init)
    slopes = [_alibi_slope(h, A_HEADS) for h in range(A_HEADS)]
    q_aug, first = [], []
    for h in range(A_HEADS):
        q_t = qt_ref[0, h * PAIR:(h + 1) * PAIR, :]
        q_aug.append(jnp.concatenate([_split_halves(q_t, tq),
                                      _alibi_query_rows(slopes[h], 2 * tq)], axis=0))
        first.append(_first_live_chunk(_largest_norm(q_t), kn_ref[0, h:h + 1, :], slopes[h],
                                       q0.astype(F32), tk))
    group_sets = [((2 * i, 2 * i + 1), jnp.minimum(jnp.minimum(first[2 * i], first[2 * i + 1]),
                                                   n_chunks - 1))
                  for i in range(A_HEADS // 2)]
    ahead, kpos, qpos = _diag_distance((n_chunks - 1) * tk, q0, tk, tq)
    ahead = jnp.where(lax.shift_right_logical(kpos, CHUNK_SHIFT)
                      <= lax.shift_right_logical(qpos, CHUNK_SHIFT), ahead * (-2.0 * LOG2E), NEG)

    def scores(h, off, j, size, diagonal):
        hs = slice(h * PAIR, (h + 1) * PAIR)
        start = pl.multiple_of(off + j * size, size)
        k_aug = jnp.concatenate([k_ref[0, pl.ds(start, size), hs],
                                 e_ref[j * size:(j + 1) * size, :]], axis=1)
        s = jnp.dot(k_aug, q_aug[h], preferred_element_type=F32)
        if diagonal:
            fix = ahead[j * size:(j + 1) * size, :] * slopes[h]
            s = s + jnp.concatenate([fix, fix], axis=1)
        return s

    def values(h, start, size):
        start = pl.multiple_of(start, size)
        return vt_ref[0, h * V_ROWS:(h + 1) * V_ROWS, pl.ds(start, size)]

    def base(h, off):
        return off.astype(F32) * (slopes[h] * LOG2E)

    qpos2 = jnp.concatenate([qpos, qpos], axis=1).astype(F32)

    def frame(h):
        return qpos2 * (slopes[h] * LOG2E)

    _attend(A_HEADS, group_sets, n_chunks, tk, scores, values, base, frame, m_sc, acc_sc, p_sc)

    for h in range(A_HEADS):
        acc = acc_sc[h]
        o = acc[:PAIR] / acc[PAIR:PAIR + 1]
        o = o[:, :tq] - lam * o[:, tq:]
        o = o * lax.rsqrt(jnp.mean(o * o, axis=0, keepdims=True) + EPS) * subg_ref[...]
        o_ref[0, :, h * PAIR:(h + 1) * PAIR] = (o * (1.0 - lam_init)).T.astype(BF16)


def _attn_a(qa_t, ka16, va_t, k_norms, lam_params, subln_g, *, tq, tk, q_pos0, lam_init):
    b, _, lq = qa_t.shape
    lkp = ka16.shape[1]
    small = lambda bi, i: (0, 0)
    return pl.pallas_call(
        functools.partial(_attn_a_kernel, tq=tq, tk=tk, q_pos0=q_pos0, lam_init=lam_init),
        grid=(b, lq // tq),
        in_specs=[pl.BlockSpec((1, A_DH), small)] * 4 + [
            pl.BlockSpec((2 * A_DH, 1), small), pl.BlockSpec((tk, LANES), small),
            pl.BlockSpec((1, A_HEADS, lkp // tk), lambda bi, i: (bi, 0, 0)),
            pl.BlockSpec((1, WIDTH, tq), lambda bi, i: (bi, 0, i)),
            _resident((1, lkp, WIDTH), lambda bi, i: (bi, 0, 0)),
            _resident((1, A_HEADS * V_ROWS, lkp), lambda bi, i: (bi, 0, 0))],
        out_specs=pl.BlockSpec((1, tq, WIDTH), lambda bi, i: (bi, i, 0)),
        out_shape=jax.ShapeDtypeStruct((b, lq, WIDTH), BF16),
        scratch_shapes=[pltpu.VMEM((A_HEADS, 1, 2 * tq), F32),
                        pltpu.VMEM((A_HEADS, PAIR + ONES_ROWS, 2 * tq), F32),
                        pltpu.VMEM((2, 2, tk, 2 * tq), BF16)],
        compiler_params=pltpu.CompilerParams(dimension_semantics=("arbitrary", "arbitrary"),
                                             vmem_limit_bytes=VMEM_LIMIT),
        name="attn_a",
    )(*lam_params, subln_g.reshape(2 * A_DH, 1), _alibi_key_tile(tk), k_norms, qa_t, ka16, va_t)


def _sum_keys(x):
    rows, tq = x.shape
    lanes_of_adds = 64
    if rows % lanes_of_adds == 0 and rows > lanes_of_adds:
        x = jnp.sum(x.reshape(rows // lanes_of_adds, lanes_of_adds, tq), axis=0)
    return jnp.sum(x, axis=0, keepdims=True)


def _count_true(hit):
    return _sum_keys(jnp.where(hit, 1.0, 0.0))


def _ordered_key(x):
    bits = pltpu.bitcast(x, I32)
    return bits ^ (lax.shift_right_arithmetic(bits, 31) & INT_MAX)


def _ordered_key_inv(k):
    return pltpu.bitcast(k ^ (lax.shift_right_arithmetic(k, 31) & INT_MAX), F32)


def _attn_b_kernel(e_ref, kn_ref, qbt_ref, qit_ref, wt_ref, kidx_ref, kb_ref, vbt_ref, o_ref,
                   sc, thr_sc, m_sc, acc_sc, p_sc, *, tq, tk, q_pos0, topk, idx_bits):
    q0 = q_pos0 + pl.program_id(1) * tq
    n_chunks = _visible_chunks(q0, tq, tk)
    last = n_chunks - 1
    topk_f = float(topk)
    qpos = q0 + lax.broadcasted_iota(I32, (1, tq), 1)
    qchunk = lax.shift_right_logical(qpos, CHUNK_SHIFT)
    n_valid = ((qchunk + 1) * CHUNK).astype(F32)

    def chunk_off(c):
        return pl.multiple_of(c * tk, tk)

    def key_pos(off):
        return off + lax.broadcasted_iota(I32, (tk, 1), 0)

    def visible(off):
        return lax.shift_right_logical(key_pos(off), CHUNK_SHIFT) <= qchunk

    qi_all = jnp.concatenate([qit_ref[0, h * IDX_DH:(h + 1) * IDX_DH, :]
                              for h in range(IDX_HEADS)], axis=1)
    w_rows = [wt_ref[0, h:h + 1, :] for h in range(IDX_HEADS)]

    sub = SUB_KEYS if tk % SUB_KEYS == 0 else LANES

    def merge_stats(a, b):
        return (jnp.maximum(a[0], b[0]), jnp.minimum(a[1], b[1]), a[2] + b[2], a[3] + b[3])

    def score_chunk(c, masked):
        stats = None
        for j in range(tk // sub):
            start = pl.multiple_of(c * tk + j * sub, sub)
            logits = jnp.dot(kidx_ref[0, pl.ds(start, sub), :], qi_all,
                             preferred_element_type=F32)
            score = jnp.zeros((sub, tq), F32)
            for h in range(IDX_HEADS):
                score = score + jnp.maximum(logits[:, h * tq:(h + 1) * tq], 0.0) * w_rows[h]
            lowest = score
            if masked:
                vis = (lax.shift_right_logical(start + lax.broadcasted_iota(I32, (sub, 1), 0),
                                               CHUNK_SHIFT) <= qchunk)
                score = jnp.where(vis, score, -jnp.inf)
                lowest = jnp.where(vis, score, jnp.inf)
            sc[pl.ds(start, sub), :] = score
            part = (jnp.max(score, axis=0, keepdims=True), jnp.min(lowest, axis=0, keepdims=True),
                    _count_true(score >= 0.0), _count_true(score > 0.0))
            stats = part if stats is None else merge_stats(stats, part)
        return stats

    def score_two(i2, st):
        st = merge_stats(st, score_chunk(2 * i2, False))
        return merge_stats(st, score_chunk(2 * i2 + 1, False))

    stats = lax.fori_loop(
        0, last // 2, score_two,
        (jnp.full((1, tq), -jnp.inf, F32), jnp.full((1, tq), jnp.inf, F32),
         jnp.zeros((1, tq), F32), jnp.zeros((1, tq), F32)))
    stats = lax.cond(lax.rem(last, 2) == 1,
                     lambda st: merge_stats(st, score_chunk(last - 1, False)),
                     lambda st: st, stats)
    row_max, row_min, n_ge0, n_pos = merge_stats(stats, score_chunk(last, True))

    def count(pred):
        def one(c):
            off = chunk_off(c)
            return _count_true(pred(sc[pl.ds(off, tk), :], off))

        def body(i, acc):
            for u in range(SCAN_CHUNKS):
                c = SCAN_CHUNKS * i + u
                part = one(jnp.minimum(c, last))
                acc = acc + (part if u == 0 else jnp.where(c <= last, 1.0, 0.0) * part)
            return acc
        return lax.fori_loop(0, (n_chunks + SCAN_CHUNKS - 1) // SCAN_CHUNKS, body,
                             jnp.zeros((1, tq), F32))

    def count_ge(t):
        return count(lambda s, off: s >= t)

    def spread(cnt):
        c = jnp.clip(cnt, 0.5, n_valid - 0.5)
        return jnp.log2(c / (n_valid - c))

    target = spread(jnp.full((1, tq), topk_f, F32))
    key_lo, key_hi = _ordered_key(row_min), _ordered_key(row_max) + 1
    cnt_lo = n_valid
    t_a, f_a = row_min, spread(n_valid) - target
    t_b, f_b = row_max, spread(jnp.zeros((1, tq), F32)) - target
    zero_up = n_ge0 >= topk_f
    key_lo = jnp.where(zero_up, jnp.maximum(key_lo, 0), key_lo)
    cnt_lo = jnp.where(zero_up, n_ge0, cnt_lo)
    t_a = jnp.where(zero_up, 0.0, t_a)
    f_a = jnp.where(zero_up, spread(n_ge0) - target, f_a)
    zero_down = n_pos < topk_f
    key_hi = jnp.where(zero_down, jnp.minimum(key_hi, 1), key_hi)
    t_b = jnp.where(zero_down, 0.0, t_b)
    f_b = jnp.where(zero_down, spread(n_pos) - target, f_b)
    state0 = (jnp.int32(0), key_lo, key_hi, cnt_lo, t_a, f_a, t_b, f_b, jnp.zeros((1, tq), F32))

    def span(klo, khi):
        return khi - klo

    def finished(klo, khi, cnt_lo):
        d = span(klo, khi)
        tight = jnp.logical_or(d == 0, d == 1)
        return jnp.logical_or(jnp.logical_or(cnt_lo == topk_f, tight), n_valid <= topk_f)

    def search_cond(st):
        it, klo, khi, cnt_lo = st[:4]
        todo = jnp.where(finished(klo, khi, cnt_lo), 0.0, 1.0)
        return jnp.logical_and(it < MAX_SEARCH_STEPS, jnp.max(todo) > 0.0)

    def search_body(st):
        it, klo, khi, cnt_lo, ta, fa, tb, fb, side = st
        done = finished(klo, khi, cnt_lo)
        guess = ta + (tb - ta) * (fa / (fa - fb))
        guess = jnp.where(guess == guess, guess, ta)
        guess = jnp.clip(guess, -3e38, 3e38)
        mid = klo + lax.shift_right_logical(span(klo, khi), 1)
        cand = jnp.where(it % BISECT_EVERY == BISECT_EVERY - 1, mid, _ordered_key(guess))
        cand = jnp.minimum(jnp.maximum(cand, klo + 1), khi - 1)
        cand = jnp.where(done, klo, cand)
        t = _ordered_key_inv(cand)
        cnt = count_ge(t)
        up = jnp.logical_and(cnt >= topk_f, jnp.logical_not(done))
        down = jnp.logical_and(cnt < topk_f, jnp.logical_not(done))
        f = spread(cnt) - target
        fb = jnp.where(jnp.logical_and(up, side > 0.0), 0.5 * fb, fb)
        fa = jnp.where(jnp.logical_and(down, side < 0.0), 0.5 * fa, fa)
        return (it + 1, jnp.where(up, cand, klo), jnp.where(down, cand, khi),
                jnp.where(up, cnt, cnt_lo),
                jnp.where(up, t, ta), jnp.where(up, f, fa),
                jnp.where(down, t, tb), jnp.where(down, f, fb),
                jnp.where(up, 1.0, jnp.where(down, -1.0, side)))

    st = lax.while_loop(search_cond, lambda st: search_body(search_body(st)), state0)
    key_lo, cnt_lo = st[1], st[3]
    thr = _ordered_key_inv(key_lo)

    thr_sc[...] = jnp.full((1, tq), INT_MAX, I32)
    tied = jnp.where(jnp.logical_and(cnt_lo != topk_f, n_valid > topk_f), 1.0, 0.0)

    @pl.when(jnp.max(tied) > 0.0)
    def _():
        room = topk_f - count(lambda s, off: s > thr)
        jmax = jnp.zeros((1, tq), I32)
        for bit in range(idx_bits - 1, -1, -1):
            cand = jmax | (1 << bit)
            g = count(lambda s, off: jnp.logical_and(s == thr, key_pos(off) <= cand))
            jmax = jnp.where(g <= room, cand, jmax)
        thr_sc[...] = jmax

    jmax = thr_sc[...]

    ahead, _, _ = _diag_distance(chunk_off(last), q0, tk, tq)

    def mask_chunk(c, masked):
        off = chunk_off(c)
        s = sc[pl.ds(off, tk), :]
        sel = jnp.logical_or(s > thr, jnp.logical_and(s == thr, key_pos(off) <= jmax))
        eff = key_pos(off).astype(F32)
        if masked:
            sel = jnp.logical_and(sel, visible(off))
            eff = eff - 2.0 * ahead
        sc[pl.ds(off, tk), :] = jnp.where(sel, 0.0, NEG)
        return jnp.max(jnp.where(sel, eff, -jnp.inf), axis=0, keepdims=True)

    nearest = lax.fori_loop(0, last, lambda c, n: jnp.maximum(n, mask_chunk(c, False)),
                            jnp.full((1, tq), -jnp.inf, F32))
    nearest = jnp.maximum(nearest, mask_chunk(last, True))

    lane2 = lax.broadcasted_iota(I32, (1, 2 * tq), 1)
    n_pairs = B_HEADS // 2
    slope_rows, q_aug, first = [], [], []
    nearest_min = jnp.min(nearest)
    for pi in range(n_pairs):
        slope_row = jnp.where(lane2 < tq, _alibi_slope(2 * pi, B_HEADS),
                              _alibi_slope(2 * pi + 1, B_HEADS)).astype(F32)
        slope_rows.append(slope_row)
        q_t = qbt_ref[0, pi * PAIR:(pi + 1) * PAIR, :]
        q_aug.append(jnp.concatenate([_split_halves(q_t, tq),
                                      _alibi_query_rows(slope_row, 2 * tq)], axis=0))
        first.append(_first_live_chunk(_largest_norm(q_t), kn_ref[0, pi:pi + 1, :],
                                       _alibi_slope(2 * pi + 1, B_HEADS), nearest_min, tk))
    group_sets = [((2 * i, 2 * i + 1), jnp.minimum(jnp.minimum(first[2 * i], first[2 * i + 1]),
                                                   last))
                  for i in range(n_pairs // 2)]
    ahead2 = jnp.concatenate([ahead, ahead], axis=1) * (-2.0 * LOG2E)
    nearest2 = jnp.concatenate([nearest, nearest], axis=1)

    def scores(pi, off, j, size, diagonal):
        start = pl.multiple_of(off + j * size, size)
        k_aug = jnp.concatenate([kb_ref[0, pl.ds(start, size), pi * PAIR:(pi + 1) * PAIR],
                                 e_ref[j * size:(j + 1) * size, :]], axis=1)
        mask = sc[pl.ds(start, size), :]
        s = (jnp.dot(k_aug, q_aug[pi], preferred_element_type=F32)
             + jnp.concatenate([mask, mask], axis=1))
        if diagonal:
            s = s + ahead2[j * size:(j + 1) * size, :] * slope_rows[pi]
        return s

    def frame(pi):
        return nearest2 * (slope_rows[pi] * LOG2E)

    def values(pi, start, size):
        start = pl.multiple_of(start, size)
        return vbt_ref[0, pi * V_ROWS:(pi + 1) * V_ROWS, pl.ds(start, size)]

    def base(pi, off):
        return off.astype(F32) * (slope_rows[pi] * LOG2E)

    _attend(n_pairs, group_sets, n_chunks, tk, scores, values, base, frame, m_sc, acc_sc, p_sc)

    for pi in range(n_pairs):
        acc = acc_sc[pi]
        o = acc[:PAIR] / acc[PAIR:PAIR + 1]
        o = jnp.concatenate([o[:B_DH, :tq], o[B_DH:, tq:]], axis=0)
        o_ref[0, :, pi * PAIR:(pi + 1) * PAIR] = o.T.astype(BF16)


def _attn_b(qb_t, qi_t, w_t, kidx16, kb16, vb_t, k_norms, *, tq, tk, q_pos0, topk):
    b, _, lq = qb_t.shape
    lkp = kb16.shape[1]
    q_blk = lambda rows: pl.BlockSpec((1, rows, tq), lambda bi, i: (bi, 0, i))
    per_batch = lambda bi, i: (bi, 0, 0)
    return pl.pallas_call(
        functools.partial(_attn_b_kernel, tq=tq, tk=tk, q_pos0=q_pos0, topk=topk,
                          idx_bits=max(1, (lkp - 1).bit_length())),
        grid=(b, lq // tq),
        in_specs=[pl.BlockSpec((tk, LANES), lambda bi, i: (0, 0)),
                  pl.BlockSpec((1, B_HEADS // 2, lkp // tk), per_batch),
                  q_blk(WIDTH), q_blk(WIDTH), q_blk(IDX_HEADS),
                  _resident((1, lkp, IDX_DH), per_batch),
                  _resident((1, lkp, WIDTH), per_batch),
                  _resident((1, B_HEADS // 2 * V_ROWS, lkp), per_batch)],
        out_specs=pl.BlockSpec((1, tq, WIDTH), lambda bi, i: (bi, i, 0)),
        out_shape=jax.ShapeDtypeStruct((b, lq, WIDTH), BF16),
        scratch_shapes=[pltpu.VMEM((lkp, tq), F32), pltpu.VMEM((1, tq), I32),
                        pltpu.VMEM((B_HEADS // 2, 1, 2 * tq), F32),
                        pltpu.VMEM((B_HEADS // 2, PAIR + ONES_ROWS, 2 * tq), F32),
                        pltpu.VMEM((2, 2, tk, 2 * tq), BF16)],
        compiler_params=pltpu.CompilerParams(dimension_semantics=("arbitrary", "arbitrary"),
                                             vmem_limit_bytes=VMEM_LIMIT),
        name="attn_b",
    )(_alibi_key_tile(tk), k_norms, qb_t, qi_t, w_t, kidx16, kb16, vb_t)


def _nt_dot(a, b):
    return lax.dot_general(a, b, (((1,), (1,)), ((), ())), preferred_element_type=F32)


def _post_kernel(x_ref, oa_ref, ob_ref, qm_ref, ga_ref, gb_ref, gm_ref, gates_ref, mk_ref,
                 mv_ref, bg_ref, wa_ref, wb_ref, wm_ref, wo_ref, fg_ref, y_ref):
    d = x_ref.shape[-1]
    om = []
    for h in range(M_HEADS):
        hs = slice(h * M_DH, (h + 1) * M_DH)
        s = _nt_dot(qm_ref[0, :, hs], mk_ref[0, :, hs])
        p = jnp.exp2(s - jnp.max(s, axis=-1, keepdims=True))
        o = jnp.dot(p.astype(BF16), mv_ref[0, :, hs], preferred_element_type=F32)
        om.append(o / jnp.sum(p, axis=-1, keepdims=True))
    om = jnp.concatenate(om, axis=1)

    def branch(gate_ref, o, w_ref):
        g = gate_ref[0].astype(F32)
        return jnp.dot((g * jax.nn.sigmoid(g) * o).astype(BF16), w_ref[...],
                       preferred_element_type=F32)

    pa = branch(ga_ref, oa_ref[0].astype(F32), wa_ref)
    pb = branch(gb_ref, ob_ref[0].astype(F32), wb_ref)
    pm = branch(gm_ref, om, wm_ref)
    g = jax.nn.sigmoid(gates_ref[0].astype(F32) + bg_ref[...])
    merged = g[:, :d] * pa + g[:, d:2 * d] * pb + g[:, 2 * d:] * pm
    y = x_ref[0] + jnp.dot(merged.astype(BF16), wo_ref[...], preferred_element_type=F32)
    y_ref[0] = y * lax.rsqrt(jnp.mean(y * y, axis=-1, keepdims=True) + EPS) * fg_ref[...]


def _post(x, oa, ob, qm, ga, gb, gm, gates, mk16, mv16, b_gate, wa, wb, wm, wo, final_g, tm):
    b, lq, d = x.shape
    n_mem = mk16.shape[1]
    blk = lambda w: pl.BlockSpec((1, tm, w), lambda bi, i: (bi, i, 0))
    per_batch = pl.BlockSpec((1, n_mem, WIDTH), lambda bi, i: (bi, 0, 0))
    full = lambda r, c: pl.BlockSpec((r, c), lambda bi, i: (0, 0))
    return pl.pallas_call(
        _post_kernel,
        grid=(b, lq // tm),
        in_specs=[blk(d)] + [blk(WIDTH)] * 6 + [blk(N_BRANCH * d), per_batch, per_batch,
                  full(1, N_BRANCH * d), full(WIDTH, d), full(WIDTH, d), full(WIDTH, d),
                  full(d, d), full(1, d)],
        out_specs=blk(d),
        out_shape=jax.ShapeDtypeStruct((b, lq, d), F32),
        compiler_params=pltpu.CompilerParams(dimension_semantics=("arbitrary", "arbitrary"),
                                             vmem_limit_bytes=VMEM_LIMIT),
        name="post",
    )(x, oa, ob, qm, ga, gb, gm, gates, mk16, mv16, b_gate.reshape(1, -1), wa, wb, wm, wo,
      final_g.reshape(1, d))


def _values_on_lanes(v16):
    b, lk, _ = v16.shape
    v_t = jnp.swapaxes(v16, 1, 2).reshape(b, WIDTH // PAIR, PAIR, lk)
    ones = jnp.ones((b, WIDTH // PAIR, ONES_ROWS, lk), v16.dtype)
    return jnp.concatenate([v_t, ones], axis=2).reshape(b, WIDTH // PAIR * V_ROWS, lk)


def _pad_axis(x, size, axis):
    pad = size - x.shape[axis]
    if pad == 0:
        return x
    widths = [(0, 0)] * x.ndim
    widths[axis] = (0, pad)
    return jnp.pad(x, widths)


def _layer(x, past, mk16, mv16, q_pos0, norm_g, w16, b_gate, lam_params, lam_init, subln_g,
           wa, wb, wm, wo, final_g, *, tm, tq_a, tq, tk):
    b, lq, d = x.shape
    p = _proj(x, norm_g, w16, tm)
    new_rows = (
        jnp.transpose(p["ka_t32"].reshape(b, A_HEADS, 2, A_DH, lq), (0, 4, 1, 2, 3)),
        p["va4"].reshape(b, lq, A_HEADS, 2 * A_DH),
        jnp.transpose(p["kb_t32"].reshape(b, B_HEADS, B_DH, lq), (0, 3, 1, 2)),
        jnp.transpose(p["vb_t32"].reshape(b, B_HEADS, B_DH, lq), (0, 3, 1, 2)),
        jnp.swapaxes(p["kw_t32"][:, :IDX_DH], 1, 2))
    p["w_t"] = p["kw_t32"][:, IDX_DH:IDX_DH + IDX_HEADS]
    ka16, kb16, kidx16, va_t, vb_t = p["ka16"], p["kb16"], p["kidx16"], p["va_t"], p["vb_t"]
    if past is not None:
        pa_k, pa_v, pb_k, pb_v, p_kidx = past
        n_past = pa_k.shape[1]
        flat16 = lambda t: t.reshape(b, n_past, -1).astype(BF16)
        rows = lambda old, new: jnp.concatenate([flat16(old), new], axis=1)
        lanes = lambda old, new: jnp.concatenate([_values_on_lanes(flat16(old)), new], axis=2)
        ka16, kb16, kidx16 = rows(pa_k, ka16), rows(pb_k, kb16), rows(p_kidx, kidx16)
        va_t, vb_t = lanes(pa_v, va_t), lanes(pb_v, vb_t)
    lk = ka16.shape[1]
    topk = min(TOPK_MAX, lk // 4)
    lkp = -(-lk // tk) * tk
    ka16, kb16, kidx16 = [_pad_axis(t, lkp, 1) for t in (ka16, kb16, kidx16)]
    va_t, vb_t = _pad_axis(va_t, lkp, 2), _pad_axis(vb_t, lkp, 2)
    lqp = -(-lq // max(tq_a, tq)) * max(tq_a, tq)
    qa_t, qb_t, qi_t, w_t = [_pad_axis(p[n], lqp, 2) for n in ("qa_t", "qb_t", "qi_t", "w_t")]
    if past is None and lkp == lq and tk % tm == 0:
        kn = jnp.max(p["kn"][..., 0].reshape(b, lkp // tk, tk // tm, 2 * (WIDTH // PAIR)), axis=2)
        kn = jnp.swapaxes(kn, 1, 2)
        kn_a, kn_b = kn[:, :WIDTH // PAIR], kn[:, WIDTH // PAIR:]
    else:
        kn_a, kn_b = _chunk_key_norms(ka16, tk), _chunk_key_norms(kb16, tk)

    oa = _attn_a(qa_t, ka16, va_t, kn_a, lam_params, subln_g, tq=tq_a, tk=tk, q_pos0=q_pos0,
                 lam_init=lam_init)[:, :lq]
    ob = _attn_b(qb_t, qi_t, w_t, kidx16, kb16, vb_t, kn_b, tq=tq, tk=tk, q_pos0=q_pos0,
                 topk=topk)[:, :lq]
    y = _post(x, oa, ob, p["qm"], p["ga"], p["gb"], p["gm"], p["gates"], mk16, mv16, b_gate,
              wa, wb, wm, wo, final_g, tm)
    return y, new_rows


def kernel(x_prompt, x_sample, mem_prompt, cache_a_k, cache_a_v, cache_b_k, cache_b_v, cache_b_kidx, cache_mem_k, cache_mem_v, norm_g, w_in, b_gate, lam_q1, lam_k1, lam_q2, lam_k2, subln_g, mem_norm_g, w_mem_kv, w_br_a, w_br_b, w_br_m, w_out, final_g):
    depth, d, _ = w_in.shape
    assert depth == 1, "single-layer step only"
    l = 0
    lam_init = 0.8 - 0.6 * math.exp(-0.3 * l)
    w16 = jnp.concatenate([w_in[l][:, :IN_KW_END], jnp.zeros((d, KW_PAD), w_in.dtype),
                           w_in[l][:, IN_KW_END:]], axis=1).astype(BF16)
    lam_params = [p[l].reshape(1, A_DH) for p in (lam_q1, lam_k1, lam_q2, lam_k2)]
    shared = (norm_g[l], w16, b_gate[l], lam_params, lam_init, subln_g[l],
              w_br_a[l].astype(BF16), w_br_b[l].astype(BF16), w_br_m[l].astype(BF16),
              w_out[l].astype(BF16), final_g)

    bp, n_mem, _ = mem_prompt.shape
    mk32, mv32, mk16, mv16 = _memkv(mem_prompt.reshape(bp * n_mem, d), mem_norm_g[l],
                                    w_mem_kv[l].astype(BF16), tm=256)
    y_p, rows_p = _layer(x_prompt, None, mk16.reshape(bp, n_mem, WIDTH),
                         mv16.reshape(bp, n_mem, WIDTH), 0, *shared, tm=256, tq_a=256, tq=128,
                         tk=1024)

    bs, n_past = cache_a_k.shape[1], cache_a_k.shape[2]
    past = (cache_a_k[l], cache_a_v[l], cache_b_k[l], cache_b_v[l], cache_b_kidx[l])
    ls = x_sample.shape[1]
    y_s, rows_s = _layer(x_sample, past, cache_mem_k[l].reshape(bs, n_mem, WIDTH).astype(BF16),
                         cache_mem_v[l].reshape(bs, n_mem, WIDTH).astype(BF16), n_past, *shared,
                         tm=ls, tq_a=128, tq=128, tk=384)

    mem_shape = (1, bp, n_mem, M_HEADS, M_DH)
    return (y_p, y_s, *[r[None] for r in rows_p], mk32.reshape(mem_shape),
            mv32.reshape(mem_shape), *[r[None] for r in rows_s])
```

```python
import functools
import math

import jax
import jax.numpy as jnp
import numpy as np
from jax import lax
from jax.experimental import pallas as pl
from jax.experimental.pallas import tpu as pltpu

F32 = jnp.float32
BF16 = jnp.bfloat16
I32 = jnp.int32

EPS = 1e-6
CHUNK = 64
CHUNK_SHIFT = 6
A_HEADS, A_DH = 4, 64
B_HEADS, B_DH = 8, 64
IDX_HEADS, IDX_DH = 8, 64
M_HEADS, M_DH = 4, 128
N_BRANCH = 3
TOPK_MAX = 256
WIDTH = 512
LANES = 128
PAIR = 128
ONES_ROWS = 16
V_ROWS = PAIR + ONES_ROWS
LOG2E = 1.4426950408889634
NEG = -1e30
INT_MAX = 2147483647
KEY_NEG_INF = -2139095041
VMEM_LIMIT = 60 * 1024 * 1024
BISECT_EVERY = 8
MAX_SEARCH_STEPS = BISECT_EVERY * 34
SCAN_CHUNKS = 2
SUB_KEYS = 256
MAX_EXP2 = 64.0
SKIP_EXP2 = 160.0
NORM_SLACK = 1.01

C_QA, C_KA, C_VA, C_GA = 0, 512, 1024, 1536
C_QB, C_KB, C_VB, C_GB = 2048, 2560, 3072, 3584
C_QI, C_KW, C_QM, C_GM, C_GATES = 4096, 4608, 4736, 5248, 5760
KW_PAD = LANES - IDX_DH - IDX_HEADS
W_COLS = C_GATES + N_BRANCH * 1024
IN_KW_END = 4680


def _bf16_pieces(x, n):
    out, rest = [], float(x)
    for _ in range(n):
        p = float(np.asarray(rest, np.float32).astype(BF16).astype(np.float32))
        out.append(p)
        rest -= p
    return out


LOG2E_PIECES = _bf16_pieces(LOG2E, 3)
POS_SPLIT = 256


def _resident(block_shape, index_map):
    return pl.BlockSpec(block_shape, index_map, pipeline_mode=pl.Buffered(1))


def _alibi_slope(h, n):
    return 2.0 ** (-8.0 * (h + 1) / n)


PROJ_OUTPUTS = (
    ("ka16", BF16, ("rows", WIDTH)), ("kb16", BF16, ("rows", WIDTH)),
    ("ga", BF16, ("rows", WIDTH)), ("gb", BF16, ("rows", WIDTH)), ("gm", BF16, ("rows", WIDTH)),
    ("qm", BF16, ("rows", WIDTH)), ("gates", BF16, ("rows", N_BRANCH * 1024)),
    ("kidx16", BF16, ("rows", IDX_DH)),
    ("qa_t", BF16, ("lanes", WIDTH)), ("qb_t", BF16, ("lanes", WIDTH)),
    ("qi_t", BF16, ("lanes", WIDTH)),
    ("va_t", BF16, ("lanes", A_HEADS * V_ROWS)), ("vb_t", BF16, ("lanes", B_HEADS // 2 * V_ROWS)),
    ("ka_t32", F32, ("lanes", WIDTH)), ("kb_t32", F32, ("lanes", WIDTH)),
    ("vb_t32", F32, ("lanes", WIDTH)), ("kw_t32", F32, ("lanes", LANES)),
    ("va4", F32, ("split", LANES)),
    ("kn", F32, ("norms", LANES)),
)


def _proj_kernel(x_ref, g_ref, w_ref, *out_refs):
    o = dict(zip([name for name, _, _ in PROJ_OUTPUTS], out_refs))
    tm = x_ref.shape[1]
    x = x_ref[0]
    hn = (x * lax.rsqrt(jnp.mean(x * x, axis=-1, keepdims=True) + EPS) * g_ref[...]).astype(BF16)

    def mm(c0, width):
        return jnp.dot(hn, w_ref[:, c0:c0 + width], preferred_element_type=F32)

    o["qa_t"][0] = (mm(C_QA, WIDTH) * (A_DH ** -0.5 * LOG2E)).T.astype(BF16)
    o["qb_t"][0] = (mm(C_QB, WIDTH) * (B_DH ** -0.5 * LOG2E)).T.astype(BF16)
    o["qi_t"][0] = (mm(C_QI, WIDTH) * (IDX_DH ** -0.5)).T.astype(BF16)
    o["qm"][0] = (mm(C_QM, WIDTH) * (M_DH ** -0.5 * LOG2E)).astype(BF16)
    norms = []
    for c0, k16, k_t32 in ((C_KA, "ka16", "ka_t32"), (C_KB, "kb16", "kb_t32")):
        z = mm(c0, WIDTH)
        o[k16][0] = z.astype(BF16)
        o[k_t32][0] = z.T
        for h in range(WIDTH // PAIR):
            z_h = z[:, h * PAIR:(h + 1) * PAIR]
            top = jnp.max(jnp.sum(z_h * z_h, axis=1, keepdims=True), axis=0, keepdims=True)
            norms.append(jnp.broadcast_to(jnp.sqrt(top), (1, LANES)))
    o["kn"][0, 0] = jnp.concatenate(norms, axis=0)
    ones = jnp.ones((ONES_ROWS, tm), BF16)
    for c0, v_t, v_t32 in ((C_VA, "va_t", None), (C_VB, "vb_t", "vb_t32")):
        z = mm(c0, WIDTH)
        for h in range(WIDTH // PAIR):
            z_h = z[:, h * PAIR:(h + 1) * PAIR]
            o[v_t][0, h * V_ROWS:h * V_ROWS + PAIR, :] = z_h.T.astype(BF16)
            o[v_t][0, h * V_ROWS + PAIR:(h + 1) * V_ROWS, :] = ones
            if v_t32 is None:
                o["va4"][0, pl.ds(h, tm, stride=A_HEADS), :] = z_h
        if v_t32 is not None:
            o[v_t32][0] = z.T
    for c0, name in ((C_GA, "ga"), (C_GB, "gb"), (C_GM, "gm")):
        o[name][0] = mm(c0, WIDTH).astype(BF16)
    kw = mm(C_KW, LANES)
    o["kidx16"][0] = kw[:, :IDX_DH].astype(BF16)
    o["kw_t32"][0] = kw.T
    for c in range(N_BRANCH * 1024 // WIDTH):
        o["gates"][0, :, c * WIDTH:(c + 1) * WIDTH] = mm(C_GATES + c * WIDTH, WIDTH).astype(BF16)


def _proj(x, norm_g, w16, tm):
    b, s, d = x.shape
    fixed = lambda bi, i: (0, 0)
    out_shape, out_specs = [], []
    for _, dtype, (kind, n) in PROJ_OUTPUTS:
        if kind == "rows":
            shape, block, index = (b, s, n), (1, tm, n), (lambda bi, i: (bi, i, 0))
        elif kind == "lanes":
            shape, block, index = (b, n, s), (1, n, tm), (lambda bi, i: (bi, 0, i))
        elif kind == "norms":
            shape, block, index = ((b, s // tm, 2 * (WIDTH // PAIR), n),
                                   (1, 1, 2 * (WIDTH // PAIR), n), (lambda bi, i: (bi, i, 0, 0)))
        else:
            shape, block, index = ((b, s * A_HEADS, n), (1, tm * A_HEADS, n),
                                   (lambda bi, i: (bi, i, 0)))
        out_shape.append(jax.ShapeDtypeStruct(shape, dtype))
        out_specs.append(pl.BlockSpec(block, index))
    outs = pl.pallas_call(
        _proj_kernel,
        grid=(b, s // tm),
        in_specs=[pl.BlockSpec((1, tm, d), lambda bi, i: (bi, i, 0)), pl.BlockSpec((1, d), fixed),
                  _resident((d, W_COLS), fixed)],
        out_specs=out_specs,
        out_shape=out_shape,
        compiler_params=pltpu.CompilerParams(dimension_semantics=("arbitrary", "arbitrary"),
                                             vmem_limit_bytes=VMEM_LIMIT),
        name="proj",
    )(x, norm_g.reshape(1, d), w16)
    return dict(zip([name for name, _, _ in PROJ_OUTPUTS], outs))


def _memkv_kernel(x_ref, g_ref, w_ref, k32, v32, k16, v16):
    x = x_ref[...]
    hn = (x * lax.rsqrt(jnp.mean(x * x, axis=-1, keepdims=True) + EPS) * g_ref[...]).astype(BF16)
    zk = jnp.dot(hn, w_ref[:, :WIDTH], preferred_element_type=F32)
    zv = jnp.dot(hn, w_ref[:, WIDTH:], preferred_element_type=F32)
    k32[...] = zk
    v32[...] = zv
    k16[...] = zk.astype(BF16)
    v16[...] = zv.astype(BF16)


def _memkv(mem2d, g, w16, tm):
    n, d = mem2d.shape
    row = lambda i: (i, 0)
    fixed = lambda i: (0, 0)
    return pl.pallas_call(
        _memkv_kernel,
        grid=(n // tm,),
        in_specs=[pl.BlockSpec((tm, d), row), pl.BlockSpec((1, d), fixed),
                  pl.BlockSpec((d, 2 * WIDTH), fixed)],
        out_specs=[pl.BlockSpec((tm, WIDTH), row)] * 4,
        out_shape=[jax.ShapeDtypeStruct((n, WIDTH), F32)] * 2
        + [jax.ShapeDtypeStruct((n, WIDTH), BF16)] * 2,
        compiler_params=pltpu.CompilerParams(dimension_semantics=("arbitrary",)),
        name="memkv",
    )(mem2d, g.reshape(1, d), w16)


def _visible_chunks(q0, tq, tk):
    n_vis = (lax.shift_right_logical(q0 + tq - 1, CHUNK_SHIFT) + 1) * CHUNK
    return (n_vis + tk - 1) // tk


def _alibi_key_tile(tk):
    pos = np.arange(tk)
    lo = pos % POS_SPLIT
    tile = np.zeros((tk, LANES), np.float32)
    tile[:, 0:3] = lo[:, None]
    tile[:, 3:6] = (pos - lo)[:, None]
    return jnp.asarray(tile, BF16)


def _alibi_query_rows(slopes_row, width):
    row = lax.broadcasted_iota(I32, (LANES, 1), 0)
    piece = jnp.zeros((LANES, 1), F32)
    for j, p in enumerate(LOG2E_PIECES):
        piece = jnp.where(jnp.logical_or(row == j, row == j + 3), p, piece)
    return (piece * slopes_row).astype(BF16) + jnp.zeros((LANES, width), BF16)


def _split_halves(q_t, tq):
    row = lax.broadcasted_iota(I32, (PAIR, 1), 0)
    zero = jnp.zeros_like(q_t)
    return jnp.concatenate([jnp.where(row < PAIR // 2, q_t, zero),
                            jnp.where(row >= PAIR // 2, q_t, zero)], axis=1)


def _diag_distance(off, q0, tk, tq):
    kpos = off + lax.broadcasted_iota(I32, (tk, 1), 0)
    qpos = q0 + lax.broadcasted_iota(I32, (1, tq), 1)
    return jnp.maximum(kpos - qpos, 0).astype(F32), kpos, qpos


def _online_update(s, v_aug, chunk_base, m_ref, acc_ref, idx):
    m_old = m_ref[idx]
    m_new = jnp.maximum(m_old, jnp.max(s, axis=0, keepdims=True) + chunk_base)
    p = jnp.exp2(s - (m_new - chunk_base)).astype(BF16)
    pv = jnp.dot(v_aug, p, preferred_element_type=F32)
    acc_ref[idx] = jnp.exp2(m_old - m_new) * acc_ref[idx] + pv
    m_ref[idx] = m_new


def _largest_norm(q_t):
    q = q_t.astype(F32)
    return jnp.sqrt(jnp.max(jnp.sum(q * q, axis=0, keepdims=True)))


def _chunk_key_norms(k16, tk):
    b, lk, _ = k16.shape
    k = k16.astype(F32).reshape(b, lk // tk, tk, WIDTH // PAIR, PAIR)
    return jnp.swapaxes(jnp.sqrt(jnp.max(jnp.sum(k * k, axis=-1), axis=2)), 1, 2)


def _first_live_chunk(q_norm, k_norms, slope, ref_pos, tk):
    chunk = lax.broadcasted_iota(I32, k_norms.shape, 1)
    newest_key = ((chunk + 1) * tk - 1).astype(F32)
    bound = q_norm * k_norms * NORM_SLACK + 1.0 - slope * LOG2E * (ref_pos - newest_key)
    live = jnp.where(bound > -SKIP_EXP2, chunk, k_norms.shape[1])
    return jnp.min(live)


def _attend(n_groups, group_sets, n_chunks, tk, scores, values, base, frame, m_ref, acc_ref,
            p_ref):
    last = n_chunks - 1
    sub = SUB_KEYS if tk % SUB_KEYS == 0 else LANES

    def chunk_off(c):
        return pl.multiple_of(c * tk, tk)

    acc_ref[...] = jnp.zeros(acc_ref.shape, F32)

    def run_pipeline(groups, first):
        def weigh(c, diagonal, slot):
            off = chunk_off(c)
            for at, g in enumerate(groups):
                shift = frame(g) - base(g, off)
                for j in range(tk // sub):
                    s = scores(g, off, j, sub, diagonal)
                    p_ref[slot, at, j * sub:(j + 1) * sub, :] = jnp.exp2(s - shift).astype(BF16)

        def gather(c, slot):
            off = chunk_off(c)
            for at, g in enumerate(groups):
                acc_ref[g] += jnp.dot(values(g, off, tk), p_ref[slot, at],
                                      preferred_element_type=F32)

        def step(i, slot):
            gather(jnp.where(i == 0, last, first + i - 1), slot)
            weigh(first + i, False, 1 - slot)

        def two_steps(i2, carry):
            step(2 * i2, 0)
            step(2 * i2 + 1, 1)
            return carry

        def four_steps(i4, carry):
            two_steps(2 * i4, carry)
            return two_steps(2 * i4 + 1, carry)

        count = last - first
        weigh(last, True, 0)
        lax.fori_loop(0, count // 4, four_steps, 0)
        lax.fori_loop(2 * (count // 4), count // 2, two_steps, 0)

        @pl.when(lax.rem(count, 2) == 1)
        def _():
            step(count - 1, 0)

        gather(jnp.where(count == 0, last, last - 1), lax.rem(count, 2))

    for groups, first in group_sets:
        run_pipeline(groups, first)

    low, high, peak = None, None, None
    for g in range(n_groups):
        acc = acc_ref[g]
        sums = acc[acc.shape[0] - ONES_ROWS:acc.shape[0] - ONES_ROWS + 1]
        mag = jnp.max(jnp.abs(acc), axis=0, keepdims=True)
        low = sums if low is None else jnp.minimum(low, sums)
        high = sums if high is None else jnp.maximum(high, sums)
        peak = mag if peak is None else jnp.maximum(peak, mag)
    trusted = jnp.logical_and(jnp.min(low) >= 2.0 ** -MAX_EXP2,
                              jnp.logical_and(jnp.max(high) <= 2.0 ** MAX_EXP2,
                                              jnp.max(peak) <= 2.0 ** (2 * MAX_EXP2 - 1)))

    @pl.when(jnp.logical_not(trusted))
    def _():
        m_ref[...] = jnp.full(m_ref.shape, NEG, F32)
        acc_ref[...] = jnp.zeros(acc_ref.shape, F32)

        def chunk(c, diagonal):
            off = chunk_off(c)
            for g in range(n_groups):
                _online_update(scores(g, off, 0, tk, diagonal), values(g, off, tk),
                               base(g, off), m_ref, acc_ref, g)

        def body(c, carry):
            chunk(c, False)
            return carry

        lax.fori_loop(0, last, body, 0)
        chunk(last, True)


def _attn_a_kernel(lq1, lk1, lq2, lk2, subg_ref, e_ref, kn_ref, qt_ref, k_ref, vt_ref, o_ref,
                   m_sc, acc_sc, p_sc, *, tq, tk, q_pos0, lam_init):
    q0 = q_pos0 + pl.program_id(1) * tq
    n_chunks = _visible_chunks(q0, tq, tk)
    lam = (jnp.exp(jnp.sum(lq1[...] * lk1[...], axis=-1, keepdims=True))
           - jnp.exp(jnp.sum(lq2[...] * lk2[...], axis=-1, keepdims=True)) + lam_init)
    slopes = [_alibi_slope(h, A_HEADS) for h in range(A_HEADS)]
    q_aug, first = [], []
    for h in range(A_HEADS):
        q_t = qt_ref[0, h * PAIR:(h + 1) * PAIR, :]
        q_aug.append(jnp.concatenate([_split_halves(q_t, tq),
                                      _alibi_query_rows(slopes[h], 2 * tq)], axis=0))
        first.append(_first_live_chunk(_largest_norm(q_t), kn_ref[0, h:h + 1, :], slopes[h],
                                       q0.astype(F32), tk))
    group_sets = [((2 * i, 2 * i + 1), jnp.minimum(jnp.minimum(first[2 * i], first[2 * i + 1]),
                                                   n_chunks - 1))
                  for i in range(A_HEADS // 2)]
    ahead, kpos, qpos = _diag_distance((n_chunks - 1) * tk, q0, tk, tq)
    ahead = jnp.where(lax.shift_right_logical(kpos, CHUNK_SHIFT)
                      <= lax.shift_right_logical(qpos, CHUNK_SHIFT), ahead * (-2.0 * LOG2E), NEG)

    def scores(h, off, j, size, diagonal):
        hs = slice(h * PAIR, (h + 1) * PAIR)
        start = pl.multiple_of(off + j * size, size)
        k_aug = jnp.concatenate([k_ref[0, pl.ds(start, size), hs],
                                 e_ref[j * size:(j + 1) * size, :]], axis=1)
        s = jnp.dot(k_aug, q_aug[h], preferred_element_type=F32)
        if diagonal:
            fix = ahead[j * size:(j + 1) * size, :] * slopes[h]
            s = s + jnp.concatenate([fix, fix], axis=1)
        return s

    def values(h, start, size):
        start = pl.multiple_of(start, size)
        return vt_ref[0, h * V_ROWS:(h + 1) * V_ROWS, pl.ds(start, size)]

    def base(h, off):
        return off.astype(F32) * (slopes[h] * LOG2E)

    qpos2 = jnp.concatenate([qpos, qpos], axis=1).astype(F32)

    def frame(h):
        return qpos2 * (slopes[h] * LOG2E)

    _attend(A_HEADS, group_sets, n_chunks, tk, scores, values, base, frame, m_sc, acc_sc, p_sc)

    for h in range(A_HEADS):
        acc = acc_sc[h]
        o = acc[:PAIR] / acc[PAIR:PAIR + 1]
        o = o[:, :tq] - lam * o[:, tq:]
        o = o * lax.rsqrt(jnp.mean(o * o, axis=0, keepdims=True) + EPS) * subg_ref[...]
        o_ref[0, :, h * PAIR:(h + 1) * PAIR] = (o * (1.0 - lam_init)).T.astype(BF16)


def _attn_a(qa_t, ka16, va_t, k_norms, lam_params, subln_g, *, tq, tk, q_pos0, lam_init):
    b, _, lq = qa_t.shape
    lkp = ka16.shape[1]
    small = lambda bi, i: (0, 0)
    return pl.pallas_call(
        functools.partial(_attn_a_kernel, tq=tq, tk=tk, q_pos0=q_pos0, lam_init=lam_init),
        grid=(b, lq // tq),
        in_specs=[pl.BlockSpec((1, A_DH), small)] * 4 + [
            pl.BlockSpec((2 * A_DH, 1), small), pl.BlockSpec((tk, LANES), small),
            pl.BlockSpec((1, A_HEADS, lkp // tk), lambda bi, i: (bi, 0, 0)),
            pl.BlockSpec((1, WIDTH, tq), lambda bi, i: (bi, 0, i)),
            _resident((1, lkp, WIDTH), lambda bi, i: (bi, 0, 0)),
            _resident((1, A_HEADS * V_ROWS, lkp), lambda bi, i: (bi, 0, 0))],
        out_specs=pl.BlockSpec((1, tq, WIDTH), lambda bi, i: (bi, i, 0)),
        out_shape=jax.ShapeDtypeStruct((b, lq, WIDTH), BF16),
        scratch_shapes=[pltpu.VMEM((A_HEADS, 1, 2 * tq), F32),
                        pltpu.VMEM((A_HEADS, PAIR + ONES_ROWS, 2 * tq), F32),
                        pltpu.VMEM((2, 2, tk, 2 * tq), BF16)],
        compiler_params=pltpu.CompilerParams(dimension_semantics=("arbitrary", "arbitrary"),
                                             vmem_limit_bytes=VMEM_LIMIT),
        name="attn_a",
    )(*lam_params, subln_g.reshape(2 * A_DH, 1), _alibi_key_tile(tk), k_norms, qa_t, ka16, va_t)


def _sum_keys(x):
    rows, tq = x.shape
    lanes_of_adds = 64
    if rows % lanes_of_adds == 0 and rows > lanes_of_adds:
        x = jnp.sum(x.reshape(rows // lanes_of_adds, lanes_of_adds, tq), axis=0)
    return jnp.sum(x, axis=0, keepdims=True)


def _count_true(hit):
    rows, tq = hit.shape
    if rows % 64 != 0:
        return _sum_keys(jnp.where(hit, 1.0, 0.0))
    acc = jnp.zeros((64, tq), F32)
    for i in range(rows // 64):
        acc = jnp.where(hit[i * 64:(i + 1) * 64], acc + 1.0, acc)
    return jnp.sum(acc, axis=0, keepdims=True)


def _ordered_key(x):
    bits = pltpu.bitcast(x, I32)
    return bits ^ (lax.shift_right_arithmetic(bits, 31) & INT_MAX)


def _ordered_key_inv(k):
    return pltpu.bitcast(k ^ (lax.shift_right_arithmetic(k, 31) & INT_MAX), F32)


def _attn_b_kernel(e_ref, kn_ref, qbt_ref, qit_ref, wt_ref, kidx_ref, kb_ref, vbt_ref, o_ref,
                   sc, thr_sc, m_sc, acc_sc, p_sc, *, tq, tk, q_pos0, topk, idx_bits):
    q0 = q_pos0 + pl.program_id(1) * tq
    n_chunks = _visible_chunks(q0, tq, tk)
    last = n_chunks - 1
    topk_f = float(topk)
    qpos = q0 + lax.broadcasted_iota(I32, (1, tq), 1)
    qchunk = lax.shift_right_logical(qpos, CHUNK_SHIFT)
    n_valid = ((qchunk + 1) * CHUNK).astype(F32)

    def chunk_off(c):
        return pl.multiple_of(c * tk, tk)

    def key_pos(off):
        return off + lax.broadcasted_iota(I32, (tk, 1), 0)

    def visible(off):
        return lax.shift_right_logical(key_pos(off), CHUNK_SHIFT) <= qchunk

    qi_all = jnp.concatenate([qit_ref[0, h * IDX_DH:(h + 1) * IDX_DH, :]
                              for h in range(IDX_HEADS)], axis=1)
    w_rows = [wt_ref[0, h:h + 1, :] for h in range(IDX_HEADS)]

    sub = SUB_KEYS if tk % SUB_KEYS == 0 else LANES

    def merge_stats(a, b):
        return (jnp.maximum(a[0], b[0]), jnp.minimum(a[1], b[1]), a[2] + b[2], a[3] + b[3])

    def score_chunk(c, masked):
        stats = None
        for j in range(tk // sub):
            start = pl.multiple_of(c * tk + j * sub, sub)
            logits = jnp.dot(kidx_ref[0, pl.ds(start, sub), :], qi_all,
                             preferred_element_type=F32)
            score = jnp.zeros((sub, tq), F32)
            for h in range(IDX_HEADS):
                score = score + jnp.maximum(logits[:, h * tq:(h + 1) * tq], 0.0) * w_rows[h]
            lowest = score
            if masked:
                vis = (lax.shift_right_logical(start + lax.broadcasted_iota(I32, (sub, 1), 0),
                                               CHUNK_SHIFT) <= qchunk)
                score = jnp.where(vis, score, -jnp.inf)
                lowest = jnp.where(vis, score, jnp.inf)
            sc[pl.ds(start, sub), :] = score
            part = (jnp.max(score, axis=0, keepdims=True), jnp.min(lowest, axis=0, keepdims=True),
                    _count_true(score >= 0.0), _count_true(score > 0.0))
            stats = part if stats is None else merge_stats(stats, part)
        return stats

    def score_two(i2, st):
        st = merge_stats(st, score_chunk(2 * i2, False))
        return merge_stats(st, score_chunk(2 * i2 + 1, False))

    stats = lax.fori_loop(
        0, last // 2, score_two,
        (jnp.full((1, tq), -jnp.inf, F32), jnp.full((1, tq), jnp.inf, F32),
         jnp.zeros((1, tq), F32), jnp.zeros((1, tq), F32)))
    stats = lax.cond(lax.rem(last, 2) == 1,
                     lambda st: merge_stats(st, score_chunk(last - 1, False)),
                     lambda st: st, stats)
    row_max, row_min, n_ge0, n_pos = merge_stats(stats, score_chunk(last, True))

    def count(pred):
        def one(c):
            off = chunk_off(c)
            return _count_true(pred(sc[pl.ds(off, tk), :], off))

        def body(i, acc):
            for u in range(SCAN_CHUNKS):
                c = SCAN_CHUNKS * i + u
                part = one(jnp.minimum(c, last))
                acc = acc + (part if u == 0 else jnp.where(c <= last, 1.0, 0.0) * part)
            return acc
        return lax.fori_loop(0, (n_chunks + SCAN_CHUNKS - 1) // SCAN_CHUNKS, body,
                             jnp.zeros((1, tq), F32))

    def count_ge(t):
        return count(lambda s, off: s >= t)

    def spread(cnt):
        c = jnp.clip(cnt, 0.5, n_valid - 0.5)
        return jnp.log2(c / (n_valid - c))

    target = spread(jnp.full((1, tq), topk_f, F32))
    key_lo, key_hi = _ordered_key(row_min), _ordered_key(row_max) + 1
    cnt_lo = n_valid
    t_a, f_a = row_min, spread(n_valid) - target
    t_b, f_b = row_max, spread(jnp.zeros((1, tq), F32)) - target
    zero_up = n_ge0 >= topk_f
    key_lo = jnp.where(zero_up, jnp.maximum(key_lo, 0), key_lo)
    cnt_lo = jnp.where(zero_up, n_ge0, cnt_lo)
    t_a = jnp.where(zero_up, 0.0, t_a)
    f_a = jnp.where(zero_up, spread(n_ge0) - target, f_a)
    zero_down = n_pos < topk_f
    key_hi = jnp.where(zero_down, jnp.minimum(key_hi, 1), key_hi)
    t_b = jnp.where(zero_down, 0.0, t_b)
    f_b = jnp.where(zero_down, spread(n_pos) - target, f_b)
    state0 = (jnp.int32(0), key_lo, key_hi, cnt_lo, t_a, f_a, t_b, f_b, jnp.zeros((1, tq), F32))

    def span(klo, khi):
        return khi - klo

    def finished(klo, khi, cnt_lo):
        d = span(klo, khi)
        tight = jnp.logical_or(d == 0, d == 1)
        return jnp.logical_or(jnp.logical_or(cnt_lo == topk_f, tight), n_valid <= topk_f)

    def search_cond(st):
        it, klo, khi, cnt_lo = st[:4]
        todo = jnp.where(finished(klo, khi, cnt_lo), 0.0, 1.0)
        return jnp.logical_and(it < MAX_SEARCH_STEPS, jnp.max(todo) > 0.0)

    def search_body(st):
        it, klo, khi, cnt_lo, ta, fa, tb, fb, side = st
        done = finished(klo, khi, cnt_lo)
        guess = ta + (tb - ta) * (fa / (fa - fb))
        guess = jnp.where(guess == guess, guess, ta)
        guess = jnp.clip(guess, -3e38, 3e38)
        mid = klo + lax.shift_right_logical(span(klo, khi), 1)
        cand = jnp.where(it % BISECT_EVERY == BISECT_EVERY - 1, mid, _ordered_key(guess))
        cand = jnp.minimum(jnp.maximum(cand, klo + 1), khi - 1)
        cand = jnp.where(done, klo, cand)
        t = _ordered_key_inv(cand)
        cnt = count_ge(t)
        up = jnp.logical_and(cnt >= topk_f, jnp.logical_not(done))
        down = jnp.logical_and(cnt < topk_f, jnp.logical_not(done))
        f = spread(cnt) - target
        fb = jnp.where(jnp.logical_and(up, side > 0.0), 0.5 * fb, fb)
        fa = jnp.where(jnp.logical_and(down, side < 0.0), 0.5 * fa, fa)
        return (it + 1, jnp.where(up, cand, klo), jnp.where(down, cand, khi),
                jnp.where(up, cnt, cnt_lo),
                jnp.where(up, t, ta), jnp.where(up, f, fa),
                jnp.where(down, t, tb), jnp.where(down, f, fb),
                jnp.where(up, 1.0, jnp.where(down, -1.0, side)))

    st = lax.while_loop(search_cond, lambda st: search_body(search_body(st)), state0)
    key_lo, cnt_lo = st[1], st[3]
    thr = _ordered_key_inv(key_lo)

    thr_sc[...] = jnp.full((1, tq), INT_MAX, I32)
    tied = jnp.where(jnp.logical_and(cnt_lo != topk_f, n_valid > topk_f), 1.0, 0.0)
    any_tied = jnp.max(tied) > 0.0

    @pl.when(any_tied)
    def _():
        room = topk_f - count(lambda s, off: s > thr)
        jmax = jnp.zeros((1, tq), I32)
        for bit in range(idx_bits - 1, -1, -1):
            cand = jmax | (1 << bit)
            g = count(lambda s, off: jnp.logical_and(s == thr, key_pos(off) <= cand))
            jmax = jnp.where(g <= room, cand, jmax)
        thr_sc[...] = jmax

    jmax = thr_sc[...]

    ahead, _, _ = _diag_distance(chunk_off(last), q0, tk, tq)

    def mask_pass(with_ties):
        def mask_chunk(c, masked):
            off = chunk_off(c)
            s = sc[pl.ds(off, tk), :]
            if with_ties:
                sel = jnp.logical_or(s > thr, jnp.logical_and(s == thr, key_pos(off) <= jmax))
            else:
                sel = s >= thr
            eff = key_pos(off).astype(F32)
            if masked:
                sel = jnp.logical_and(sel, visible(off))
                eff = eff - 2.0 * ahead
            sc[pl.ds(off, tk), :] = jnp.where(sel, 0.0, NEG)
            return jnp.max(jnp.where(sel, eff, -jnp.inf), axis=0, keepdims=True)

        near = lax.fori_loop(0, last, lambda c, n: jnp.maximum(n, mask_chunk(c, False)),
                             jnp.full((1, tq), -jnp.inf, F32))
        return jnp.maximum(near, mask_chunk(last, True))

    nearest = lax.cond(any_tied, lambda: mask_pass(True), lambda: mask_pass(False))

    lane2 = lax.broadcasted_iota(I32, (1, 2 * tq), 1)
    n_pairs = B_HEADS // 2
    slope_rows, q_aug, first = [], [], []
    nearest_min = jnp.min(nearest)
    for pi in range(n_pairs):
        slope_row = jnp.where(lane2 < tq, _alibi_slope(2 * pi, B_HEADS),
                              _alibi_slope(2 * pi + 1, B_HEADS)).astype(F32)
        slope_rows.append(slope_row)
        q_t = qbt_ref[0, pi * PAIR:(pi + 1) * PAIR, :]
        q_aug.append(jnp.concatenate([_split_halves(q_t, tq),
                                      _alibi_query_rows(slope_row, 2 * tq)], axis=0))
        first.append(_first_live_chunk(_largest_norm(q_t), kn_ref[0, pi:pi + 1, :],
                                       _alibi_slope(2 * pi + 1, B_HEADS), nearest_min, tk))
    group_sets = [((2 * i, 2 * i + 1), jnp.minimum(jnp.minimum(first[2 * i], first[2 * i + 1]),
                                                   last))
                  for i in range(n_pairs // 2)]
    ahead2 = jnp.concatenate([ahead, ahead], axis=1) * (-2.0 * LOG2E)
    nearest2 = jnp.concatenate([nearest, nearest], axis=1)

    def scores(pi, off, j, size, diagonal):
        start = pl.multiple_of(off + j * size, size)
        k_aug = jnp.concatenate([kb_ref[0, pl.ds(start, size), pi * PAIR:(pi + 1) * PAIR],
                                 e_ref[j * size:(j + 1) * size, :]], axis=1)
        mask = sc[pl.ds(start, size), :]
        s = (jnp.dot(k_aug, q_aug[pi], preferred_element_type=F32)
             + jnp.concatenate([mask, mask], axis=1))
        if diagonal:
            s = s + ahead2[j * size:(j + 1) * size, :] * slope_rows[pi]
        return s

    def frame(pi):
        return nearest2 * (slope_rows[pi] * LOG2E)

    def values(pi, start, size):
        start = pl.multiple_of(start, size)
        return vbt_ref[0, pi * V_ROWS:(pi + 1) * V_ROWS, pl.ds(start, size)]

    def base(pi, off):
        return off.astype(F32) * (slope_rows[pi] * LOG2E)

    _attend(n_pairs, group_sets, n_chunks, tk, scores, values, base, frame, m_sc, acc_sc, p_sc)

    for pi in range(n_pairs):
        acc = acc_sc[pi]
        o = acc[:PAIR] / acc[PAIR:PAIR + 1]
        o = jnp.concatenate([o[:B_DH, :tq], o[B_DH:, tq:]], axis=0)
        o_ref[0, :, pi * PAIR:(pi + 1) * PAIR] = o.T.astype(BF16)


def _attn_b(qb_t, qi_t, w_t, kidx16, kb16, vb_t, k_norms, *, tq, tk, q_pos0, topk):
    b, _, lq = qb_t.shape
    lkp = kb16.shape[1]
    q_blk = lambda rows: pl.BlockSpec((1, rows, tq), lambda bi, i: (bi, 0, i))
    per_batch = lambda bi, i: (bi, 0, 0)
    return pl.pallas_call(
        functools.partial(_attn_b_kernel, tq=tq, tk=tk, q_pos0=q_pos0, topk=topk,
                          idx_bits=max(1, (lkp - 1).bit_length())),
        grid=(b, lq // tq),
        in_specs=[pl.BlockSpec((tk, LANES), lambda bi, i: (0, 0)),
                  pl.BlockSpec((1, B_HEADS // 2, lkp // tk), per_batch),
                  q_blk(WIDTH), q_blk(WIDTH), q_blk(IDX_HEADS),
                  _resident((1, lkp, IDX_DH), per_batch),
                  _resident((1, lkp, WIDTH), per_batch),
                  _resident((1, B_HEADS // 2 * V_ROWS, lkp), per_batch)],
        out_specs=pl.BlockSpec((1, tq, WIDTH), lambda bi, i: (bi, i, 0)),
        out_shape=jax.ShapeDtypeStruct((b, lq, WIDTH), BF16),
        scratch_shapes=[pltpu.VMEM((lkp, tq), F32), pltpu.VMEM((1, tq), I32),
                        pltpu.VMEM((B_HEADS // 2, 1, 2 * tq), F32),
                        pltpu.VMEM((B_HEADS // 2, PAIR + ONES_ROWS, 2 * tq), F32),
                        pltpu.VMEM((2, 2, tk, 2 * tq), BF16)],
        compiler_params=pltpu.CompilerParams(dimension_semantics=("arbitrary", "arbitrary"),
                                             vmem_limit_bytes=VMEM_LIMIT),
        name="attn_b",
    )(_alibi_key_tile(tk), k_norms, qb_t, qi_t, w_t, kidx16, kb16, vb_t)


def _nt_dot(a, b):
    return lax.dot_general(a, b, (((1,), (1,)), ((), ())), preferred_element_type=F32)


def _post_kernel(x_ref, oa_ref, ob_ref, qm_ref, ga_ref, gb_ref, gm_ref, gates_ref, mk_ref,
                 mv_ref, bg_ref, wa_ref, wb_ref, wm_ref, wo_ref, fg_ref, y_ref):
    d = x_ref.shape[-1]
    om = []
    for h in range(M_HEADS):
        hs = slice(h * M_DH, (h + 1) * M_DH)
        s = _nt_dot(qm_ref[0, :, hs], mk_ref[0, :, hs])
        p = jnp.exp2(s - jnp.max(s, axis=-1, keepdims=True))
        o = jnp.dot(p.astype(BF16), mv_ref[0, :, hs], preferred_element_type=F32)
        om.append(o / jnp.sum(p, axis=-1, keepdims=True))
    om = jnp.concatenate(om, axis=1)

    def branch(gate_ref, o, w_ref):
        g = gate_ref[0].astype(F32)
        return jnp.dot((g * jax.nn.sigmoid(g) * o).astype(BF16), w_ref[...],
                       preferred_element_type=F32)

    pa = branch(ga_ref, oa_ref[0].astype(F32), wa_ref)
    pb = branch(gb_ref, ob_ref[0].astype(F32), wb_ref)
    pm = branch(gm_ref, om, wm_ref)
    g = jax.nn.sigmoid(gates_ref[0].astype(F32) + bg_ref[...])
    merged = g[:, :d] * pa + g[:, d:2 * d] * pb + g[:, 2 * d:] * pm
    y = x_ref[0] + jnp.dot(merged.astype(BF16), wo_ref[...], preferred_element_type=F32)
    y_ref[0] = y * lax.rsqrt(jnp.mean(y * y, axis=-1, keepdims=True) + EPS) * fg_ref[...]


def _post(x, oa, ob, qm, ga, gb, gm, gates, mk16, mv16, b_gate, wa, wb, wm, wo, final_g, tm):
    b, lq, d = x.shape
    n_mem = mk16.shape[1]
    blk = lambda w: pl.BlockSpec((1, tm, w), lambda bi, i: (bi, i, 0))
    per_batch = pl.BlockSpec((1, n_mem, WIDTH), lambda bi, i: (bi, 0, 0))
    full = lambda r, c: pl.BlockSpec((r, c), lambda bi, i: (0, 0))
    return pl.pallas_call(
        _post_kernel,
        grid=(b, lq // tm),
        in_specs=[blk(d)] + [blk(WIDTH)] * 6 + [blk(N_BRANCH * d), per_batch, per_batch,
                  full(1, N_BRANCH * d), full(WIDTH, d), full(WIDTH, d), full(WIDTH, d),
                  full(d, d), full(1, d)],
        out_specs=blk(d),
        out_shape=jax.ShapeDtypeStruct((b, lq, d), F32),
        compiler_params=pltpu.CompilerParams(dimension_semantics=("arbitrary", "arbitrary"),
                                             vmem_limit_bytes=VMEM_LIMIT),
        name="post",
    )(x, oa, ob, qm, ga, gb, gm, gates, mk16, mv16, b_gate.reshape(1, -1), wa, wb, wm, wo,
      final_g.reshape(1, d))


def _values_on_lanes(v16):
    b, lk, _ = v16.shape
    v_t = jnp.swapaxes(v16, 1, 2).reshape(b, WIDTH // PAIR, PAIR, lk)
    ones = jnp.ones((b, WIDTH // PAIR, ONES_ROWS, lk), v16.dtype)
    return jnp.concatenate([v_t, ones], axis=2).reshape(b, WIDTH // PAIR * V_ROWS, lk)


def _pad_axis(x, size, axis):
    pad = size - x.shape[axis]
    if pad == 0:
        return x
    widths = [(0, 0)] * x.ndim
    widths[axis] = (0, pad)
    return jnp.pad(x, widths)


def _layer(x, past, mk16, mv16, q_pos0, norm_g, w16, b_gate, lam_params, lam_init, subln_g,
           wa, wb, wm, wo, final_g, *, tm, tq_a, tq, tk):
    b, lq, d = x.shape
    p = _proj(x, norm_g, w16, tm)
    new_rows = (
        jnp.transpose(p["ka_t32"].reshape(b, A_HEADS, 2, A_DH, lq), (0, 4, 1, 2, 3)),
        p["va4"].reshape(b, lq, A_HEADS, 2 * A_DH),
        jnp.transpose(p["kb_t32"].reshape(b, B_HEADS, B_DH, lq), (0, 3, 1, 2)),
        jnp.transpose(p["vb_t32"].reshape(b, B_HEADS, B_DH, lq), (0, 3, 1, 2)),
        jnp.swapaxes(p["kw_t32"][:, :IDX_DH], 1, 2))
    p["w_t"] = p["kw_t32"][:, IDX_DH:IDX_DH + IDX_HEADS]
    ka16, kb16, kidx16, va_t, vb_t = p["ka16"], p["kb16"], p["kidx16"], p["va_t"], p["vb_t"]
    if past is not None:
        pa_k, pa_v, pb_k, pb_v, p_kidx = past
        n_past = pa_k.shape[1]
        flat16 = lambda t: t.reshape(b, n_past, -1).astype(BF16)
        rows = lambda old, new: jnp.concatenate([flat16(old), new], axis=1)
        lanes = lambda old, new: jnp.concatenate([_values_on_lanes(flat16(old)), new], axis=2)
        ka16, kb16, kidx16 = rows(pa_k, ka16), rows(pb_k, kb16), rows(p_kidx, kidx16)
        va_t, vb_t = lanes(pa_v, va_t), lanes(pb_v, vb_t)
    lk = ka16.shape[1]
    topk = min(TOPK_MAX, lk // 4)
    lkp = -(-lk // tk) * tk
    ka16, kb16, kidx16 = [_pad_axis(t, lkp, 1) for t in (ka16, kb16, kidx16)]
    va_t, vb_t = _pad_axis(va_t, lkp, 2), _pad_axis(vb_t, lkp, 2)
    lqp = -(-lq // max(tq_a, tq)) * max(tq_a, tq)
    qa_t, qb_t, qi_t, w_t = [_pad_axis(p[n], lqp, 2) for n in ("qa_t", "qb_t", "qi_t", "w_t")]
    if past is None and lkp == lq and tk % tm == 0:
        kn = jnp.max(p["kn"][..., 0].reshape(b, lkp // tk, tk // tm, 2 * (WIDTH // PAIR)), axis=2)
        kn = jnp.swapaxes(kn, 1, 2)
        kn_a, kn_b = kn[:, :WIDTH // PAIR], kn[:, WIDTH // PAIR:]
    else:
        kn_a, kn_b = _chunk_key_norms(ka16, tk), _chunk_key_norms(kb16, tk)

    oa = _attn_a(qa_t, ka16, va_t, kn_a, lam_params, subln_g, tq=tq_a, tk=tk, q_pos0=q_pos0,
                 lam_init=lam_init)[:, :lq]
    ob = _attn_b(qb_t, qi_t, w_t, kidx16, kb16, vb_t, kn_b, tq=tq, tk=tk, q_pos0=q_pos0,
                 topk=topk)[:, :lq]
    y = _post(x, oa, ob, p["qm"], p["ga"], p["gb"], p["gm"], p["gates"], mk16, mv16, b_gate,
              wa, wb, wm, wo, final_g, tm)
    return y, new_rows


def kernel(x_prompt, x_sample, mem_prompt, cache_a_k, cache_a_v, cache_b_k, cache_b_v, cache_b_kidx, cache_mem_k, cache_mem_v, norm_g, w_in, b_gate, lam_q1, lam_k1, lam_q2, lam_k2, subln_g, mem_norm_g, w_mem_kv, w_br_a, w_br_b, w_br_m, w_out, final_g):
    depth, d, _ = w_in.shape
    assert depth == 1, "single-layer step only"
    l = 0
    lam_init = 0.8 - 0.6 * math.exp(-0.3 * l)
    w16 = jnp.concatenate([w_in[l][:, :IN_KW_END], jnp.zeros((d, KW_PAD), w_in.dtype),
                           w_in[l][:, IN_KW_END:]], axis=1).astype(BF16)
    lam_params = [p[l].reshape(1, A_DH) for p in (lam_q1, lam_k1, lam_q2, lam_k2)]
    shared = (norm_g[l], w16, b_gate[l], lam_params, lam_init, subln_g[l],
              w_br_a[l].astype(BF16), w_br_b[l].astype(BF16), w_br_m[l].astype(BF16),
              w_out[l].astype(BF16), final_g)

    bp, n_mem, _ = mem_prompt.shape
    mk32, mv32, mk16, mv16 = _memkv(mem_prompt.reshape(bp * n_mem, d), mem_norm_g[l],
                                    w_mem_kv[l].astype(BF16), tm=256)
    y_p, rows_p = _layer(x_prompt, None, mk16.reshape(bp, n_mem, WIDTH),
                         mv16.reshape(bp, n_mem, WIDTH), 0, *shared, tm=256, tq_a=256, tq=128,
                         tk=1024)

    bs, n_past = cache_a_k.shape[1], cache_a_k.shape[2]
    past = (cache_a_k[l], cache_a_v[l], cache_b_k[l], cache_b_v[l], cache_b_kidx[l])
    ls = x_sample.shape[1]
    y_s, rows_s = _layer(x_sample, past, cache_mem_k[l].reshape(bs, n_mem, WIDTH).astype(BF16),
                         cache_mem_v[l].reshape(bs, n_mem, WIDTH).astype(BF16), n_past, *shared,
                         tm=ls, tq_a=128, tq=128, tk=384)

    mem_shape = (1, bp, n_mem, M_HEADS, M_DH)
    return (y_p, y_s, *[r[None] for r in rows_p], mk32.reshape(mem_shape),
            mv32.reshape(mem_shape), *[r[None] for r in rows_s])
```

```python
import functools
import math

import jax
import jax.numpy as jnp
import numpy as np
from jax import lax
from jax.experimental import pallas as pl
from jax.experimental.pallas import tpu as pltpu

F32 = jnp.float32
BF16 = jnp.bfloat16
I32 = jnp.int32

EPS = 1e-6
CHUNK = 64
CHUNK_SHIFT = 6
A_HEADS, A_DH = 4, 64
B_HEADS, B_DH = 8, 64
IDX_HEADS, IDX_DH = 8, 64
M_HEADS, M_DH = 4, 128
N_BRANCH = 3
TOPK_MAX = 256
WIDTH = 512
LANES = 128
PAIR = 128
ONES_ROWS = 16
V_ROWS = PAIR + ONES_ROWS
LOG2E = 1.4426950408889634
NEG = -1e30
INT_MAX = 2147483647
KEY_NEG_INF = -2139095041
VMEM_LIMIT = 60 * 1024 * 1024
BISECT_EVERY = 16
MAX_SEARCH_STEPS = BISECT_EVERY * 34
SCAN_CHUNKS = 2
SUB_KEYS = 256
MAX_EXP2 = 64.0
SKIP_EXP2 = 160.0
NORM_SLACK = 1.01

C_QA, C_KA, C_VA, C_GA = 0, 512, 1024, 1536
C_QB, C_KB, C_VB, C_GB = 2048, 2560, 3072, 3584
C_QI, C_KW, C_QM, C_GM, C_GATES = 4096, 4608, 4736, 5248, 5760
KW_PAD = LANES - IDX_DH - IDX_HEADS
W_COLS = C_GATES + N_BRANCH * 1024
IN_KW_END = 4680


def _bf16_pieces(x, n):
    out, rest = [], float(x)
    for _ in range(n):
        p = float(np.asarray(rest, np.float32).astype(BF16).astype(np.float32))
        out.append(p)
        rest -= p
    return out


LOG2E_PIECES = _bf16_pieces(LOG2E, 3)
POS_SPLIT = 256


def _resident(block_shape, index_map):
    return pl.BlockSpec(block_shape, index_map, pipeline_mode=pl.Buffered(1))


def _alibi_slope(h, n):
    return 2.0 ** (-8.0 * (h + 1) / n)


PROJ_OUTPUTS = (
    ("ka16", BF16, ("rows", WIDTH)), ("kb16", BF16, ("rows", WIDTH)),
    ("ga", BF16, ("rows", WIDTH)), ("gb", BF16, ("rows", WIDTH)), ("gm", BF16, ("rows", WIDTH)),
    ("qm", BF16, ("rows", WIDTH)), ("gates", BF16, ("rows", N_BRANCH * 1024)),
    ("kidx16", BF16, ("rows", IDX_DH)),
    ("qa_t", BF16, ("lanes", WIDTH)), ("qb_t", BF16, ("lanes", WIDTH)),
    ("qi_t", BF16, ("lanes", WIDTH)),
    ("va_t", BF16, ("lanes", A_HEADS * V_ROWS)), ("vb_t", BF16, ("lanes", B_HEADS // 2 * V_ROWS)),
    ("ka_t32", F32, ("lanes", WIDTH)), ("kb_t32", F32, ("lanes", WIDTH)),
    ("vb_t32", F32, ("lanes", WIDTH)), ("kw_t32", F32, ("lanes", LANES)),
    ("va4", F32, ("split", LANES)),
    ("kn", F32, ("norms", LANES)),
)


def _proj_kernel(x_ref, g_ref, w_ref, *out_refs):
    o = dict(zip([name for name, _, _ in PROJ_OUTPUTS], out_refs))
    tm = x_ref.shape[1]
    x = x_ref[0]
    hn = (x * lax.rsqrt(jnp.mean(x * x, axis=-1, keepdims=True) + EPS) * g_ref[...]).astype(BF16)

    def mm(c0, width):
        return jnp.dot(hn, w_ref[:, c0:c0 + width], preferred_element_type=F32)

    o["qa_t"][0] = (mm(C_QA, WIDTH) * (A_DH ** -0.5 * LOG2E)).T.astype(BF16)
    o["qb_t"][0] = (mm(C_QB, WIDTH) * (B_DH ** -0.5 * LOG2E)).T.astype(BF16)
    o["qi_t"][0] = (mm(C_QI, WIDTH) * (IDX_DH ** -0.5)).T.astype(BF16)
    o["qm"][0] = (mm(C_QM, WIDTH) * (M_DH ** -0.5 * LOG2E)).astype(BF16)
    norms = []
    for c0, k16, k_t32 in ((C_KA, "ka16", "ka_t32"), (C_KB, "kb16", "kb_t32")):
        z = mm(c0, WIDTH)
        o[k16][0] = z.astype(BF16)
        o[k_t32][0] = z.T
        for h in range(WIDTH // PAIR):
            z_h = z[:, h * PAIR:(h + 1) * PAIR]
            top = jnp.max(jnp.sum(z_h * z_h, axis=1, keepdims=True), axis=0, keepdims=True)
            norms.append(jnp.broadcast_to(jnp.sqrt(top), (1, LANES)))
    o["kn"][0, 0] = jnp.concatenate(norms, axis=0)
    ones = jnp.ones((ONES_ROWS, tm), BF16)
    for c0, v_t, v_t32 in ((C_VA, "va_t", None), (C_VB, "vb_t", "vb_t32")):
        z = mm(c0, WIDTH)
        for h in range(WIDTH // PAIR):
            z_h = z[:, h * PAIR:(h + 1) * PAIR]
            o[v_t][0, h * V_ROWS:h * V_ROWS + PAIR, :] = z_h.T.astype(BF16)
            o[v_t][0, h * V_ROWS + PAIR:(h + 1) * V_ROWS, :] = ones
            if v_t32 is None:
                o["va4"][0, pl.ds(h, tm, stride=A_HEADS), :] = z_h
        if v_t32 is not None:
            o[v_t32][0] = z.T
    for c0, name in ((C_GA, "ga"), (C_GB, "gb"), (C_GM, "gm")):
        o[name][0] = mm(c0, WIDTH).astype(BF16)
    kw = mm(C_KW, LANES)
    o["kidx16"][0] = kw[:, :IDX_DH].astype(BF16)
    o["kw_t32"][0] = kw.T
    for c in range(N_BRANCH * 1024 // WIDTH):
        o["gates"][0, :, c * WIDTH:(c + 1) * WIDTH] = mm(C_GATES + c * WIDTH, WIDTH).astype(BF16)


def _proj(x, norm_g, w16, tm):
    b, s, d = x.shape
    fixed = lambda bi, i: (0, 0)
    out_shape, out_specs = [], []
    for _, dtype, (kind, n) in PROJ_OUTPUTS:
        if kind == "rows":
            shape, block, index = (b, s, n), (1, tm, n), (lambda bi, i: (bi, i, 0))
        elif kind == "lanes":
            shape, block, index = (b, n, s), (1, n, tm), (lambda bi, i: (bi, 0, i))
        elif kind == "norms":
            shape, block, index = ((b, s // tm, 2 * (WIDTH // PAIR), n),
                                   (1, 1, 2 * (WIDTH // PAIR), n), (lambda bi, i: (bi, i, 0, 0)))
        else:
            shape, block, index = ((b, s * A_HEADS, n), (1, tm * A_HEADS, n),
                                   (lambda bi, i: (bi, i, 0)))
        out_shape.append(jax.ShapeDtypeStruct(shape, dtype))
        out_specs.append(pl.BlockSpec(block, index))
    outs = pl.pallas_call(
        _proj_kernel,
        grid=(b, s // tm),
        in_specs=[pl.BlockSpec((1, tm, d), lambda bi, i: (bi, i, 0)), pl.BlockSpec((1, d), fixed),
                  _resident((d, W_COLS), fixed)],
        out_specs=out_specs,
        out_shape=out_shape,
        compiler_params=pltpu.CompilerParams(dimension_semantics=("arbitrary", "arbitrary"),
                                             vmem_limit_bytes=VMEM_LIMIT),
        name="proj",
    )(x, norm_g.reshape(1, d), w16)
    return dict(zip([name for name, _, _ in PROJ_OUTPUTS], outs))


def _memkv_kernel(x_ref, g_ref, w_ref, k32, v32, k16, v16):
    x = x_ref[...]
    hn = (x * lax.rsqrt(jnp.mean(x * x, axis=-1, keepdims=True) + EPS) * g_ref[...]).astype(BF16)
    zk = jnp.dot(hn, w_ref[:, :WIDTH], preferred_element_type=F32)
    zv = jnp.dot(hn, w_ref[:, WIDTH:], preferred_element_type=F32)
    k32[...] = zk
    v32[...] = zv
    k16[...] = zk.astype(BF16)
    v16[...] = zv.astype(BF16)


def _memkv(mem2d, g, w16, tm):
    n, d = mem2d.shape
    row = lambda i: (i, 0)
    fixed = lambda i: (0, 0)
    return pl.pallas_call(
        _memkv_kernel,
        grid=(n // tm,),
        in_specs=[pl.BlockSpec((tm, d), row), pl.BlockSpec((1, d), fixed),
                  pl.BlockSpec((d, 2 * WIDTH), fixed)],
        out_specs=[pl.BlockSpec((tm, WIDTH), row)] * 4,
        out_shape=[jax.ShapeDtypeStruct((n, WIDTH), F32)] * 2
        + [jax.ShapeDtypeStruct((n, WIDTH), BF16)] * 2,
        compiler_params=pltpu.CompilerParams(dimension_semantics=("arbitrary",)),
        name="memkv",
    )(mem2d, g.reshape(1, d), w16)


def _visible_chunks(q0, tq, tk):
    n_vis = (lax.shift_right_logical(q0 + tq - 1, CHUNK_SHIFT) + 1) * CHUNK
    return (n_vis + tk - 1) // tk


def _alibi_key_tile(tk):
    pos = np.arange(tk)
    lo = pos % POS_SPLIT
    tile = np.zeros((tk, LANES), np.float32)
    tile[:, 0:3] = lo[:, None]
    tile[:, 3:6] = (pos - lo)[:, None]
    return jnp.asarray(tile, BF16)


def _alibi_query_rows(slopes_row, width):
    row = lax.broadcasted_iota(I32, (LANES, 1), 0)
    piece = jnp.zeros((LANES, 1), F32)
    for j, p in enumerate(LOG2E_PIECES):
        piece = jnp.where(jnp.logical_or(row == j, row == j + 3), p, piece)
    return (piece * slopes_row).astype(BF16) + jnp.zeros((LANES, width), BF16)


def _split_halves(q_t, tq):
    row = lax.broadcasted_iota(I32, (PAIR, 1), 0)
    zero = jnp.zeros_like(q_t)
    return jnp.concatenate([jnp.where(row < PAIR // 2, q_t, zero),
                            jnp.where(row >= PAIR // 2, q_t, zero)], axis=1)


def _diag_distance(off, q0, tk, tq):
    kpos = off + lax.broadcasted_iota(I32, (tk, 1), 0)
    qpos = q0 + lax.broadcasted_iota(I32, (1, tq), 1)
    return jnp.maximum(kpos - qpos, 0).astype(F32), kpos, qpos


def _online_update(s, v_aug, chunk_base, m_ref, acc_ref, idx):
    m_old = m_ref[idx]
    m_new = jnp.maximum(m_old, jnp.max(s, axis=0, keepdims=True) + chunk_base)
    p = jnp.exp2(s - (m_new - chunk_base)).astype(BF16)
    pv = jnp.dot(v_aug, p, preferred_element_type=F32)
    acc_ref[idx] = jnp.exp2(m_old - m_new) * acc_ref[idx] + pv
    m_ref[idx] = m_new


def _largest_norm(q_t):
    q = q_t.astype(F32)
    return jnp.sqrt(jnp.max(jnp.sum(q * q, axis=0, keepdims=True)))


def _chunk_key_norms(k16, tk):
    b, lk, _ = k16.shape
    k = k16.astype(F32).reshape(b, lk // tk, tk, WIDTH // PAIR, PAIR)
    return jnp.swapaxes(jnp.sqrt(jnp.max(jnp.sum(k * k, axis=-1), axis=2)), 1, 2)


def _first_live_chunk(q_norm, k_norms, slope, ref_pos, tk):
    chunk = lax.broadcasted_iota(I32, k_norms.shape, 1)
    newest_key = ((chunk + 1) * tk - 1).astype(F32)
    bound = q_norm * k_norms * NORM_SLACK + 1.0 - slope * LOG2E * (ref_pos - newest_key)
    live = jnp.where(bound > -SKIP_EXP2, chunk, k_norms.shape[1])
    return jnp.min(live)


def _attend(n_groups, group_sets, n_chunks, tk, scores, values, base, frame, m_ref, acc_ref,
            p_ref):
    last = n_chunks - 1
    sub = SUB_KEYS if tk % SUB_KEYS == 0 else LANES

    def chunk_off(c):
        return pl.multiple_of(c * tk, tk)

    acc_ref[...] = jnp.zeros(acc_ref.shape, F32)

    def make_pipeline(at_set, groups, first):
        def weigh(c, diagonal, slot):
            off = chunk_off(c)
            for at, g in enumerate(groups):
                shift = frame(g) - base(g, off)
                for j in range(tk // sub):
                    s = scores(g, off, j, sub, diagonal)
                    p_ref[at_set, slot, at, j * sub:(j + 1) * sub, :] = (
                        jnp.exp2(s - shift).astype(BF16))

        def gather(c, slot):
            off = chunk_off(c)
            for at, g in enumerate(groups):
                acc_ref[g] += jnp.dot(values(g, off, tk), p_ref[at_set, slot, at],
                                      preferred_element_type=F32)

        def step(i, slot):
            gather(jnp.where(i == 0, last, first + i - 1), slot)
            weigh(first + i, False, 1 - slot)

        def two_steps(i2, carry):
            step(2 * i2, 0)
            step(2 * i2 + 1, 1)
            return carry

        def four_steps(i4, carry):
            two_steps(2 * i4, carry)
            return two_steps(2 * i4 + 1, carry)

        count = last - first

        def middle():
            lax.fori_loop(0, count // 4, four_steps, 0)
            lax.fori_loop(2 * (count // 4), count // 2, two_steps, 0)

            @pl.when(lax.rem(count, 2) == 1)
            def _():
                step(count - 1, 0)

        return (lambda: weigh(last, True, 0), middle,
                lambda: gather(jnp.where(count == 0, last, last - 1), lax.rem(count, 2)))

    stages = [make_pipeline(i, groups, first) for i, (groups, first) in enumerate(group_sets)]
    for stage in range(3):
        for pipeline in stages:
            pipeline[stage]()

    low, high, peak = None, None, None
    for g in range(n_groups):
        acc = acc_ref[g]
        sums = acc[acc.shape[0] - ONES_ROWS:acc.shape[0] - ONES_ROWS + 1]
        mag = jnp.max(jnp.abs(acc), axis=0, keepdims=True)
        low = sums if low is None else jnp.minimum(low, sums)
        high = sums if high is None else jnp.maximum(high, sums)
        peak = mag if peak is None else jnp.maximum(peak, mag)
    trusted = jnp.logical_and(jnp.min(low) >= 2.0 ** -MAX_EXP2,
                              jnp.logical_and(jnp.max(high) <= 2.0 ** MAX_EXP2,
                                              jnp.max(peak) <= 2.0 ** (2 * MAX_EXP2 - 1)))

    @pl.when(jnp.logical_not(trusted))
    def _():
        m_ref[...] = jnp.full(m_ref.shape, NEG, F32)
        acc_ref[...] = jnp.zeros(acc_ref.shape, F32)

        def chunk(c, diagonal):
            off = chunk_off(c)
            for g in range(n_groups):
                _online_update(scores(g, off, 0, tk, diagonal), values(g, off, tk),
                               base(g, off), m_ref, acc_ref, g)

        def body(c, carry):
            chunk(c, False)
            return carry

        lax.fori_loop(0, last, body, 0)
        chunk(last, True)


def _attn_a_kernel(lq1, lk1, lq2, lk2, subg_ref, e_ref, kn_ref, qt_ref, k_ref, vt_ref, o_ref,
                   m_sc, acc_sc, p_sc, *, tq, tk, q_pos0, lam_init):
    q0 = q_pos0 + pl.program_id(1) * tq
    n_chunks = _visible_chunks(q0, tq, tk)
    lam = (jnp.exp(jnp.sum(lq1[...] * lk1[...], axis=-1, keepdims=True))
           - jnp.exp(jnp.sum(lq2[...] * lk2[...], axis=-1, keepdims=True)) + lam_init)
    slopes = [_alibi_slope(h, A_HEADS) for h in range(A_HEADS)]
    q_aug, first = [], []
    for h in range(A_HEADS):
        q_t = qt_ref[0, h * PAIR:(h + 1) * PAIR, :]
        q_aug.append(jnp.concatenate([_split_halves(q_t, tq),
                                      _alibi_query_rows(slopes[h], 2 * tq)], axis=0))
        first.append(_first_live_chunk(_largest_norm(q_t), kn_ref[0, h:h + 1, :], slopes[h],
                                       q0.astype(F32), tk))
    group_sets = [((2 * i, 2 * i + 1), jnp.minimum(jnp.minimum(first[2 * i], first[2 * i + 1]),
                                                   n_chunks - 1))
                  for i in range(A_HEADS // 2)]
    ahead, kpos, qpos = _diag_distance((n_chunks - 1) * tk, q0, tk, tq)
    ahead = jnp.where(lax.shift_right_logical(kpos, CHUNK_SHIFT)
                      <= lax.shift_right_logical(qpos, CHUNK_SHIFT), ahead * (-2.0 * LOG2E), NEG)

    def scores(h, off, j, size, diagonal):
        hs = slice(h * PAIR, (h + 1) * PAIR)
        start = pl.multiple_of(off + j * size, size)
        k_aug = jnp.concatenate([k_ref[0, pl.ds(start, size), hs],
                                 e_ref[j * size:(j + 1) * size, :]], axis=1)
        s = jnp.dot(k_aug, q_aug[h], preferred_element_type=F32)
        if diagonal:
            fix = ahead[j * size:(j + 1) * size, :] * slopes[h]
            s = s + jnp.concatenate([fix, fix], axis=1)
        return s

    def values(h, start, size):
        start = pl.multiple_of(start, size)
        return vt_ref[0, h * V_ROWS:(h + 1) * V_ROWS, pl.ds(start, size)]

    def base(h, off):
        return off.astype(F32) * (slopes[h] * LOG2E)

    qpos2 = jnp.concatenate([qpos, qpos], axis=1).astype(F32)

    def frame(h):
        return qpos2 * (slopes[h] * LOG2E)

    _attend(A_HEADS, group_sets, n_chunks, tk, scores, values, base, frame, m_sc, acc_sc, p_sc)

    for h in range(A_HEADS):
        acc = acc_sc[h]
        o = acc[:PAIR] / acc[PAIR:PAIR + 1]
        o = o[:, :tq] - lam * o[:, tq:]
        o = o * lax.rsqrt(jnp.mean(o * o, axis=0, keepdims=True) + EPS) * subg_ref[...]
        o_ref[0, :, h * PAIR:(h + 1) * PAIR] = (o * (1.0 - lam_init)).T.astype(BF16)


def _attn_a(qa_t, ka16, va_t, k_norms, lam_params, subln_g, *, tq, tk, q_pos0, lam_init):
    b, _, lq = qa_t.shape
    lkp = ka16.shape[1]
    small = lambda bi, i: (0, 0)
    return pl.pallas_call(
        functools.partial(_attn_a_kernel, tq=tq, tk=tk, q_pos0=q_pos0, lam_init=lam_init),
        grid=(b, lq // tq),
        in_specs=[pl.BlockSpec((1, A_DH), small)] * 4 + [
            pl.BlockSpec((2 * A_DH, 1), small), pl.BlockSpec((tk, LANES), small),
            pl.BlockSpec((1, A_HEADS, lkp // tk), lambda bi, i: (bi, 0, 0)),
            pl.BlockSpec((1, WIDTH, tq), lambda bi, i: (bi, 0, i)),
            _resident((1, lkp, WIDTH), lambda bi, i: (bi, 0, 0)),
            _resident((1, A_HEADS * V_ROWS, lkp), lambda bi, i: (bi, 0, 0))],
        out_specs=pl.BlockSpec((1, tq, WIDTH), lambda bi, i: (bi, i, 0)),
        out_shape=jax.ShapeDtypeStruct((b, lq, WIDTH), BF16),
        scratch_shapes=[pltpu.VMEM((A_HEADS, 1, 2 * tq), F32),
                        pltpu.VMEM((A_HEADS, PAIR + ONES_ROWS, 2 * tq), F32),
                        pltpu.VMEM((2, 2, 2, tk, 2 * tq), BF16)],
        compiler_params=pltpu.CompilerParams(dimension_semantics=("arbitrary", "arbitrary"),
                                             vmem_limit_bytes=VMEM_LIMIT),
        name="attn_a",
    )(*lam_params, subln_g.reshape(2 * A_DH, 1), _alibi_key_tile(tk), k_norms, qa_t, ka16, va_t)


def _sum_keys(x):
    rows, tq = x.shape
    lanes_of_adds = 64
    if rows % lanes_of_adds == 0 and rows > lanes_of_adds:
        x = jnp.sum(x.reshape(rows // lanes_of_adds, lanes_of_adds, tq), axis=0)
    return jnp.sum(x, axis=0, keepdims=True)


def _count_true(hit):
    rows, tq = hit.shape
    if rows % 64 != 0:
        return _sum_keys(jnp.where(hit, 1.0, 0.0))
    acc = jnp.zeros((64, tq), F32)
    for i in range(rows // 64):
        acc = jnp.where(hit[i * 64:(i + 1) * 64], acc + 1.0, acc)
    return jnp.sum(acc, axis=0, keepdims=True)


def _ordered_key(x):
    bits = pltpu.bitcast(x, I32)
    return bits ^ (lax.shift_right_arithmetic(bits, 31) & INT_MAX)


def _ordered_key_inv(k):
    return pltpu.bitcast(k ^ (lax.shift_right_arithmetic(k, 31) & INT_MAX), F32)


def _attn_b_kernel(e_ref, kn_ref, qbt_ref, qit_ref, wt_ref, kidx_ref, kb_ref, vbt_ref, o_ref,
                   sc, thr_sc, m_sc, acc_sc, p_sc, *, tq, tk, q_pos0, topk, idx_bits):
    q0 = q_pos0 + pl.program_id(1) * tq
    n_chunks = _visible_chunks(q0, tq, tk)
    last = n_chunks - 1
    topk_f = float(topk)
    qpos = q0 + lax.broadcasted_iota(I32, (1, tq), 1)
    qchunk = lax.shift_right_logical(qpos, CHUNK_SHIFT)
    n_valid = ((qchunk + 1) * CHUNK).astype(F32)

    def chunk_off(c):
        return pl.multiple_of(c * tk, tk)

    def key_pos(off):
        return off + lax.broadcasted_iota(I32, (tk, 1), 0)

    def visible(off):
        return lax.shift_right_logical(key_pos(off), CHUNK_SHIFT) <= qchunk

    qi_all = jnp.concatenate([qit_ref[0, h * IDX_DH:(h + 1) * IDX_DH, :]
                              for h in range(IDX_HEADS)], axis=1)
    w_rows = [wt_ref[0, h:h + 1, :] for h in range(IDX_HEADS)]

    sub = SUB_KEYS if tk % SUB_KEYS == 0 else LANES

    def merge_stats(a, b):
        return (jnp.maximum(a[0], b[0]), jnp.minimum(a[1], b[1]), a[2] + b[2], a[3] + b[3])

    def score_chunk(c, masked):
        stats = None
        for j in range(tk // sub):
            start = pl.multiple_of(c * tk + j * sub, sub)
            logits = jnp.dot(kidx_ref[0, pl.ds(start, sub), :], qi_all,
                             preferred_element_type=F32)
            score = jnp.zeros((sub, tq), F32)
            for h in range(IDX_HEADS):
                score = score + jnp.maximum(logits[:, h * tq:(h + 1) * tq], 0.0) * w_rows[h]
            lowest = score
            if masked:
                vis = (lax.shift_right_logical(start + lax.broadcasted_iota(I32, (sub, 1), 0),
                                               CHUNK_SHIFT) <= qchunk)
                score = jnp.where(vis, score, -jnp.inf)
                lowest = jnp.where(vis, score, jnp.inf)
            sc[pl.ds(start, sub), :] = score
            part = (jnp.max(score, axis=0, keepdims=True), jnp.min(lowest, axis=0, keepdims=True),
                    _count_true(score >= 0.0), _count_true(score > 0.0))
            stats = part if stats is None else merge_stats(stats, part)
        return stats

    def score_two(i2, st):
        st = merge_stats(st, score_chunk(2 * i2, False))
        return merge_stats(st, score_chunk(2 * i2 + 1, False))

    stats = lax.fori_loop(
        0, last // 2, score_two,
        (jnp.full((1, tq), -jnp.inf, F32), jnp.full((1, tq), jnp.inf, F32),
         jnp.zeros((1, tq), F32), jnp.zeros((1, tq), F32)))
    stats = lax.cond(lax.rem(last, 2) == 1,
                     lambda st: merge_stats(st, score_chunk(last - 1, False)),
                     lambda st: st, stats)
    row_max, row_min, n_ge0, n_pos = merge_stats(stats, score_chunk(last, True))

    def count(pred):
        def one(c):
            off = chunk_off(c)
            return _count_true(pred(sc[pl.ds(off, tk), :], off))

        def body(i, acc):
            for u in range(SCAN_CHUNKS):
                c = SCAN_CHUNKS * i + u
                part = one(jnp.minimum(c, last))
                acc = acc + (part if u == 0 else jnp.where(c <= last, 1.0, 0.0) * part)
            return acc
        return lax.fori_loop(0, (n_chunks + SCAN_CHUNKS - 1) // SCAN_CHUNKS, body,
                             jnp.zeros((1, tq), F32))

    def count_ge(t):
        return count(lambda s, off: s >= t)

    def spread(cnt):
        c = jnp.clip(cnt, 0.5, n_valid - 0.5)
        return jnp.log2(c / (n_valid - c))

    target = spread(jnp.full((1, tq), topk_f, F32))
    key_lo, key_hi = _ordered_key(row_min), _ordered_key(row_max) + 1
    cnt_lo = n_valid
    t_a, f_a = row_min, spread(n_valid) - target
    t_b, f_b = row_max, spread(jnp.zeros((1, tq), F32)) - target
    zero_up = n_ge0 >= topk_f
    key_lo = jnp.where(zero_up, jnp.maximum(key_lo, 0), key_lo)
    cnt_lo = jnp.where(zero_up, n_ge0, cnt_lo)
    t_a = jnp.where(zero_up, 0.0, t_a)
    f_a = jnp.where(zero_up, spread(n_ge0) - target, f_a)
    zero_down = n_pos < topk_f
    key_hi = jnp.where(zero_down, jnp.minimum(key_hi, 1), key_hi)
    t_b = jnp.where(zero_down, 0.0, t_b)
    f_b = jnp.where(zero_down, spread(n_pos) - target, f_b)
    state0 = (jnp.int32(0), key_lo, key_hi, cnt_lo, t_a, f_a, t_b, f_b, jnp.zeros((1, tq), F32))

    def span(klo, khi):
        return khi - klo

    def finished(klo, khi, cnt_lo):
        d = span(klo, khi)
        tight = jnp.logical_or(d == 0, d == 1)
        return jnp.logical_or(jnp.logical_or(cnt_lo == topk_f, tight), n_valid <= topk_f)

    def search_cond(st):
        it, klo, khi, cnt_lo = st[:4]
        todo = jnp.where(finished(klo, khi, cnt_lo), 0.0, 1.0)
        return jnp.logical_and(it < MAX_SEARCH_STEPS, jnp.max(todo) > 0.0)

    def search_body(st):
        it, klo, khi, cnt_lo, ta, fa, tb, fb, side = st
        done = finished(klo, khi, cnt_lo)
        guess = ta + (tb - ta) * (fa / (fa - fb))
        guess = jnp.where(guess == guess, guess, ta)
        guess = jnp.clip(guess, -3e38, 3e38)
        mid = klo + lax.shift_right_logical(span(klo, khi), 1)
        cand = jnp.where(it % BISECT_EVERY == BISECT_EVERY - 1, mid, _ordered_key(guess))
        cand = jnp.minimum(jnp.maximum(cand, klo + 1), khi - 1)
        cand = jnp.where(done, klo, cand)
        t = _ordered_key_inv(cand)
        cnt = count_ge(t)
        up = jnp.logical_and(cnt >= topk_f, jnp.logical_not(done))
        down = jnp.logical_and(cnt < topk_f, jnp.logical_not(done))
        f = spread(cnt) - target
        fb = jnp.where(jnp.logical_and(up, side > 0.0), 0.5 * fb, fb)
        fa = jnp.where(jnp.logical_and(down, side < 0.0), 0.5 * fa, fa)
        return (it + 1, jnp.where(up, cand, klo), jnp.where(down, cand, khi),
                jnp.where(up, cnt, cnt_lo),
                jnp.where(up, t, ta), jnp.where(up, f, fa),
                jnp.where(down, t, tb), jnp.where(down, f, fb),
                jnp.where(up, 1.0, jnp.where(down, -1.0, side)))

    st = lax.while_loop(search_cond, lambda st: search_body(search_body(st)), state0)
    key_lo, cnt_lo = st[1], st[3]
    thr = _ordered_key_inv(key_lo)

    thr_sc[...] = jnp.full((1, tq), INT_MAX, I32)
    tied = jnp.where(jnp.logical_and(cnt_lo != topk_f, n_valid > topk_f), 1.0, 0.0)
    any_tied = jnp.max(tied) > 0.0

    @pl.when(any_tied)
    def _():
        room = topk_f - count(lambda s, off: s > thr)
        jmax = jnp.zeros((1, tq), I32)
        for bit in range(idx_bits - 1, -1, -1):
            cand = jmax | (1 << bit)
            g = count(lambda s, off: jnp.logical_and(s == thr, key_pos(off) <= cand))
            jmax = jnp.where(g <= room, cand, jmax)
        thr_sc[...] = jmax

    jmax = thr_sc[...]

    ahead, _, _ = _diag_distance(chunk_off(last), q0, tk, tq)

    def mask_pass(with_ties):
        def mask_chunk(c, masked):
            off = chunk_off(c)
            s = sc[pl.ds(off, tk), :]
            if with_ties:
                sel = jnp.logical_or(s > thr, jnp.logical_and(s == thr, key_pos(off) <= jmax))
            else:
                sel = s >= thr
            eff = key_pos(off).astype(F32)
            if masked:
                sel = jnp.logical_and(sel, visible(off))
                eff = eff - 2.0 * ahead
            sc[pl.ds(off, tk), :] = jnp.where(sel, 0.0, NEG)
            return jnp.max(jnp.where(sel, eff, -jnp.inf), axis=0, keepdims=True)

        near = lax.fori_loop(0, last, lambda c, n: jnp.maximum(n, mask_chunk(c, False)),
                             jnp.full((1, tq), -jnp.inf, F32))
        return jnp.maximum(near, mask_chunk(last, True))

    nearest = lax.cond(any_tied, lambda: mask_pass(True), lambda: mask_pass(False))

    lane2 = lax.broadcasted_iota(I32, (1, 2 * tq), 1)
    n_pairs = B_HEADS // 2
    slope_rows, q_aug, first = [], [], []
    nearest_min = jnp.min(nearest)
    for pi in range(n_pairs):
        slope_row = jnp.where(lane2 < tq, _alibi_slope(2 * pi, B_HEADS),
                              _alibi_slope(2 * pi + 1, B_HEADS)).astype(F32)
        slope_rows.append(slope_row)
        q_t = qbt_ref[0, pi * PAIR:(pi + 1) * PAIR, :]
        q_aug.append(jnp.concatenate([_split_halves(q_t, tq),
                                      _alibi_query_rows(slope_row, 2 * tq)], axis=0))
        first.append(_first_live_chunk(_largest_norm(q_t), kn_ref[0, pi:pi + 1, :],
                                       _alibi_slope(2 * pi + 1, B_HEADS), nearest_min, tk))
    group_sets = [((2 * i, 2 * i + 1), jnp.minimum(jnp.minimum(first[2 * i], first[2 * i + 1]),
                                                   last))
                  for i in range(n_pairs // 2)]
    ahead2 = jnp.concatenate([ahead, ahead], axis=1) * (-2.0 * LOG2E)
    nearest2 = jnp.concatenate([nearest, nearest], axis=1)

    def scores(pi, off, j, size, diagonal):
        start = pl.multiple_of(off + j * size, size)
        k_aug = jnp.concatenate([kb_ref[0, pl.ds(start, size), pi * PAIR:(pi + 1) * PAIR],
                                 e_ref[j * size:(j + 1) * size, :]], axis=1)
        mask = sc[pl.ds(start, size), :]
        s = (jnp.dot(k_aug, q_aug[pi], preferred_element_type=F32)
             + jnp.concatenate([mask, mask], axis=1))
        if diagonal:
            s = s + ahead2[j * size:(j + 1) * size, :] * slope_rows[pi]
        return s

    def frame(pi):
        return nearest2 * (slope_rows[pi] * LOG2E)

    def values(pi, start, size):
        start = pl.multiple_of(start, size)
        return vbt_ref[0, pi * V_ROWS:(pi + 1) * V_ROWS, pl.ds(start, size)]

    def base(pi, off):
        return off.astype(F32) * (slope_rows[pi] * LOG2E)

    _attend(n_pairs, group_sets, n_chunks, tk, scores, values, base, frame, m_sc, acc_sc, p_sc)

    for pi in range(n_pairs):
        acc = acc_sc[pi]
        o = acc[:PAIR] / acc[PAIR:PAIR + 1]
        o = jnp.concatenate([o[:B_DH, :tq], o[B_DH:, tq:]], axis=0)
        o_ref[0, :, pi * PAIR:(pi + 1) * PAIR] = o.T.astype(BF16)


def _attn_b(qb_t, qi_t, w_t, kidx16, kb16, vb_t, k_norms, *, tq, tk, q_pos0, topk):
    b, _, lq = qb_t.shape
    lkp = kb16.shape[1]
    q_blk = lambda rows: pl.BlockSpec((1, rows, tq), lambda bi, i: (bi, 0, i))
    per_batch = lambda bi, i: (bi, 0, 0)
    return pl.pallas_call(
        functools.partial(_attn_b_kernel, tq=tq, tk=tk, q_pos0=q_pos0, topk=topk,
                          idx_bits=max(1, (lkp - 1).bit_length())),
        grid=(b, lq // tq),
        in_specs=[pl.BlockSpec((tk, LANES), lambda bi, i: (0, 0)),
                  pl.BlockSpec((1, B_HEADS // 2, lkp // tk), per_batch),
                  q_blk(WIDTH), q_blk(WIDTH), q_blk(IDX_HEADS),
                  _resident((1, lkp, IDX_DH), per_batch),
                  _resident((1, lkp, WIDTH), per_batch),
                  _resident((1, B_HEADS // 2 * V_ROWS, lkp), per_batch)],
        out_specs=pl.BlockSpec((1, tq, WIDTH), lambda bi, i: (bi, i, 0)),
        out_shape=jax.ShapeDtypeStruct((b, lq, WIDTH), BF16),
        scratch_shapes=[pltpu.VMEM((lkp, tq), F32), pltpu.VMEM((1, tq), I32),
                        pltpu.VMEM((B_HEADS // 2, 1, 2 * tq), F32),
                        pltpu.VMEM((B_HEADS // 2, PAIR + ONES_ROWS, 2 * tq), F32),
                        pltpu.VMEM((2, 2, 2, tk, 2 * tq), BF16)],
        compiler_params=pltpu.CompilerParams(dimension_semantics=("arbitrary", "arbitrary"),
                                             vmem_limit_bytes=VMEM_LIMIT),
        name="attn_b",
    )(_alibi_key_tile(tk), k_norms, qb_t, qi_t, w_t, kidx16, kb16, vb_t)


def _nt_dot(a, b):
    return lax.dot_general(a, b, (((1,), (1,)), ((), ())), preferred_element_type=F32)


def _post_kernel(x_ref, oa_ref, ob_ref, qm_ref, ga_ref, gb_ref, gm_ref, gates_ref, mk_ref,
                 mv_ref, bg_ref, wa_ref, wb_ref, wm_ref, wo_ref, fg_ref, y_ref):
    d = x_ref.shape[-1]
    om = []
    for h in range(M_HEADS):
        hs = slice(h * M_DH, (h + 1) * M_DH)
        s = _nt_dot(qm_ref[0, :, hs], mk_ref[0, :, hs])
        p = jnp.exp2(s - jnp.max(s, axis=-1, keepdims=True))
        o = jnp.dot(p.astype(BF16), mv_ref[0, :, hs], preferred_element_type=F32)
        om.append(o / jnp.sum(p, axis=-1, keepdims=True))
    om = jnp.concatenate(om, axis=1)

    def branch(gate_ref, o, w_ref):
        g = gate_ref[0].astype(F32)
        return jnp.dot((g * jax.nn.sigmoid(g) * o).astype(BF16), w_ref[...],
                       preferred_element_type=F32)

    pa = branch(ga_ref, oa_ref[0].astype(F32), wa_ref)
    pb = branch(gb_ref, ob_ref[0].astype(F32), wb_ref)
    pm = branch(gm_ref, om, wm_ref)
    g = jax.nn.sigmoid(gates_ref[0].astype(F32) + bg_ref[...])
    merged = g[:, :d] * pa + g[:, d:2 * d] * pb + g[:, 2 * d:] * pm
    y = x_ref[0] + jnp.dot(merged.astype(BF16), wo_ref[...], preferred_element_type=F32)
    y_ref[0] = y * lax.rsqrt(jnp.mean(y * y, axis=-1, keepdims=True) + EPS) * fg_ref[...]


def _post(x, oa, ob, qm, ga, gb, gm, gates, mk16, mv16, b_gate, wa, wb, wm, wo, final_g, tm):
    b, lq, d = x.shape
    n_mem = mk16.shape[1]
    blk = lambda w: pl.BlockSpec((1, tm, w), lambda bi, i: (bi, i, 0))
    per_batch = pl.BlockSpec((1, n_mem, WIDTH), lambda bi, i: (bi, 0, 0))
    full = lambda r, c: pl.BlockSpec((r, c), lambda bi, i: (0, 0))
    return pl.pallas_call(
        _post_kernel,
        grid=(b, lq // tm),
        in_specs=[blk(d)] + [blk(WIDTH)] * 6 + [blk(N_BRANCH * d), per_batch, per_batch,
                  full(1, N_BRANCH * d), full(WIDTH, d), full(WIDTH, d), full(WIDTH, d),
                  full(d, d), full(1, d)],
        out_specs=blk(d),
        out_shape=jax.ShapeDtypeStruct((b, lq, d), F32),
        compiler_params=pltpu.CompilerParams(dimension_semantics=("arbitrary", "arbitrary"),
                                             vmem_limit_bytes=VMEM_LIMIT),
        name="post",
    )(x, oa, ob, qm, ga, gb, gm, gates, mk16, mv16, b_gate.reshape(1, -1), wa, wb, wm, wo,
      final_g.reshape(1, d))


def _values_on_lanes(v16):
    b, lk, _ = v16.shape
    v_t = jnp.swapaxes(v16, 1, 2).reshape(b, WIDTH // PAIR, PAIR, lk)
    ones = jnp.ones((b, WIDTH // PAIR, ONES_ROWS, lk), v16.dtype)
    return jnp.concatenate([v_t, ones], axis=2).reshape(b, WIDTH // PAIR * V_ROWS, lk)


def _pad_axis(x, size, axis):
    pad = size - x.shape[axis]
    if pad == 0:
        return x
    widths = [(0, 0)] * x.ndim
    widths[axis] = (0, pad)
    return jnp.pad(x, widths)


def _layer(x, past, mk16, mv16, q_pos0, norm_g, w16, b_gate, lam_params, lam_init, subln_g,
           wa, wb, wm, wo, final_g, *, tm, tq_a, tq, tk):
    b, lq, d = x.shape
    p = _proj(x, norm_g, w16, tm)
    new_rows = (
        jnp.transpose(p["ka_t32"].reshape(b, A_HEADS, 2, A_DH, lq), (0, 4, 1, 2, 3)),
        p["va4"].reshape(b, lq, A_HEADS, 2 * A_DH),
        jnp.transpose(p["kb_t32"].reshape(b, B_HEADS, B_DH, lq), (0, 3, 1, 2)),
        jnp.transpose(p["vb_t32"].reshape(b, B_HEADS, B_DH, lq), (0, 3, 1, 2)),
        jnp.swapaxes(p["kw_t32"][:, :IDX_DH], 1, 2))
    p["w_t"] = p["kw_t32"][:, IDX_DH:IDX_DH + IDX_HEADS]
    ka16, kb16, kidx16, va_t, vb_t = p["ka16"], p["kb16"], p["kidx16"], p["va_t"], p["vb_t"]
    if past is not None:
        pa_k, pa_v, pb_k, pb_v, p_kidx = past
        n_past = pa_k.shape[1]
        flat16 = lambda t: t.reshape(b, n_past, -1).astype(BF16)
        rows = lambda old, new: jnp.concatenate([flat16(old), new], axis=1)
        lanes = lambda old, new: jnp.concatenate([_values_on_lanes(flat16(old)), new], axis=2)
        ka16, kb16, kidx16 = rows(pa_k, ka16), rows(pb_k, kb16), rows(p_kidx, kidx16)
        va_t, vb_t = lanes(pa_v, va_t), lanes(pb_v, vb_t)
    lk = ka16.shape[1]
    topk = min(TOPK_MAX, lk // 4)
    lkp = -(-lk // tk) * tk
    ka16, kb16, kidx16 = [_pad_axis(t, lkp, 1) for t in (ka16, kb16, kidx16)]
    va_t, vb_t = _pad_axis(va_t, lkp, 2), _pad_axis(vb_t, lkp, 2)
    lqp = -(-lq // max(tq_a, tq)) * max(tq_a, tq)
    qa_t, qb_t, qi_t, w_t = [_pad_axis(p[n], lqp, 2) for n in ("qa_t", "qb_t", "qi_t", "w_t")]
    if past is None and lkp == lq and tk % tm == 0:
        kn = jnp.max(p["kn"][..., 0].reshape(b, lkp // tk, tk // tm, 2 * (WIDTH // PAIR)), axis=2)
        kn = jnp.swapaxes(kn, 1, 2)
        kn_a, kn_b = kn[:, :WIDTH // PAIR], kn[:, WIDTH // PAIR:]
    else:
        kn_a, kn_b = _chunk_key_norms(ka16, tk), _chunk_key_norms(kb16, tk)

    oa = _attn_a(qa_t, ka16, va_t, kn_a, lam_params, subln_g, tq=tq_a, tk=tk, q_pos0=q_pos0,
                 lam_init=lam_init)[:, :lq]
    ob = _attn_b(qb_t, qi_t, w_t, kidx16, kb16, vb_t, kn_b, tq=tq, tk=tk, q_pos0=q_pos0,
                 topk=topk)[:, :lq]
    y = _post(x, oa, ob, p["qm"], p["ga"], p["gb"], p["gm"], p["gates"], mk16, mv16, b_gate,
              wa, wb, wm, wo, final_g, tm)
    return y, new_rows


def kernel(x_prompt, x_sample, mem_prompt, cache_a_k, cache_a_v, cache_b_k, cache_b_v, cache_b_kidx, cache_mem_k, cache_mem_v, norm_g, w_in, b_gate, lam_q1, lam_k1, lam_q2, lam_k2, subln_g, mem_norm_g, w_mem_kv, w_br_a, w_br_b, w_br_m, w_out, final_g):
    depth, d, _ = w_in.shape
    assert depth == 1, "single-layer step only"
    l = 0
    lam_init = 0.8 - 0.6 * math.exp(-0.3 * l)
    w16 = jnp.concatenate([w_in[l][:, :IN_KW_END], jnp.zeros((d, KW_PAD), w_in.dtype),
                           w_in[l][:, IN_KW_END:]], axis=1).astype(BF16)
    lam_params = [p[l].reshape(1, A_DH) for p in (lam_q1, lam_k1, lam_q2, lam_k2)]
    shared = (norm_g[l], w16, b_gate[l], lam_params, lam_init, subln_g[l],
              w_br_a[l].astype(BF16), w_br_b[l].astype(BF16), w_br_m[l].astype(BF16),
              w_out[l].astype(BF16), final_g)

    bp, n_mem, _ = mem_prompt.shape
    mk32, mv32, mk16, mv16 = _memkv(mem_prompt.reshape(bp * n_mem, d), mem_norm_g[l],
                                    w_mem_kv[l].astype(BF16), tm=256)
    y_p, rows_p = _layer(x_prompt, None, mk16.reshape(bp, n_mem, WIDTH),
                         mv16.reshape(bp, n_mem, WIDTH), 0, *shared, tm=256, tq_a=256, tq=128,
                         tk=1024)

    bs, n_past = cache_a_k.shape[1], cache_a_k.shape[2]
    past = (cache_a_k[l], cache_a_v[l], cache_b_k[l], cache_b_v[l], cache_b_kidx[l])
    ls = x_sample.shape[1]
    y_s, rows_s = _layer(x_sample, past, cache_mem_k[l].reshape(bs, n_mem, WIDTH).astype(BF16),
                         cache_mem_v[l].reshape(bs, n_mem, WIDTH).astype(BF16), n_past, *shared,
                         tm=ls, tq_a=128, tq=128, tk=384)

    mem_shape = (1, bp, n_mem, M_HEADS, M_DH)
    return (y_p, y_s, *[r[None] for r in rows_p], mk32.reshape(mem_shape),
            mv32.reshape(mem_shape), *[r[None] for r in rows_s])
```

```python
import functools
import math

import jax
import jax.numpy as jnp
import numpy as np
from jax import lax
from jax.experimental import pallas as pl
from jax.experimental.pallas import tpu as pltpu

F32 = jnp.float32
BF16 = jnp.bfloat16
I32 = jnp.int32

EPS = 1e-6
CHUNK = 64
CHUNK_SHIFT = 6
A_HEADS, A_DH = 4, 64
B_HEADS, B_DH = 8, 64
IDX_HEADS, IDX_DH = 8, 64
M_HEADS, M_DH = 4, 128
N_BRANCH = 3
TOPK_MAX = 256
WIDTH = 512
LANES = 128
PAIR = 128
ONES_ROWS = 16
V_ROWS = PAIR + ONES_ROWS
LOG2E = 1.4426950408889634
NEG = -1e30
INT_MAX = 2147483647
KEY_NEG_INF = -2139095041
VMEM_LIMIT = 60 * 1024 * 1024
BISECT_EVERY = 16
MAX_SEARCH_STEPS = BISECT_EVERY * 34
SCAN_CHUNKS = 2
SUB_KEYS = 256
MAX_EXP2 = 64.0
SKIP_EXP2 = 160.0
NORM_SLACK = 1.01

C_QA, C_KA, C_VA, C_GA = 0, 512, 1024, 1536
C_QB, C_KB, C_VB, C_GB = 2048, 2560, 3072, 3584
C_QI, C_KW, C_QM, C_GM, C_GATES = 4096, 4608, 4736, 5248, 5760
KW_PAD = LANES - IDX_DH - IDX_HEADS
W_COLS = C_GATES + N_BRANCH * 1024
IN_KW_END = 4680


def _bf16_pieces(x, n):
    out, rest = [], float(x)
    for _ in range(n):
        p = float(np.asarray(rest, np.float32).astype(BF16).astype(np.float32))
        out.append(p)
        rest -= p
    return out


LOG2E_PIECES = _bf16_pieces(LOG2E, 3)
POS_SPLIT = 256


def _resident(block_shape, index_map):
    return pl.BlockSpec(block_shape, index_map, pipeline_mode=pl.Buffered(1))


def _alibi_slope(h, n):
    return 2.0 ** (-8.0 * (h + 1) / n)


PROJ_OUTPUTS = (
    ("ka16", BF16, ("rows", WIDTH)), ("kb16", BF16, ("rows", WIDTH)),
    ("ga", BF16, ("rows", WIDTH)), ("gb", BF16, ("rows", WIDTH)), ("gm", BF16, ("rows", WIDTH)),
    ("qm", BF16, ("rows", WIDTH)), ("gates", BF16, ("rows", N_BRANCH * 1024)),
    ("kidx16", BF16, ("rows", IDX_DH)),
    ("qa_t", BF16, ("lanes", WIDTH)), ("qb_t", BF16, ("lanes", WIDTH)),
    ("qi_t", BF16, ("lanes", WIDTH)),
    ("va_t", BF16, ("lanes", A_HEADS * V_ROWS)), ("vb_t", BF16, ("lanes", B_HEADS // 2 * V_ROWS)),
    ("ka_t32", F32, ("lanes", WIDTH)), ("kb_t32", F32, ("lanes", WIDTH)),
    ("vb_t32", F32, ("lanes", WIDTH)), ("kw_t32", F32, ("lanes", LANES)),
    ("va4", F32, ("split", LANES)),
    ("kn", F32, ("norms", LANES)),
)


def _proj_kernel(x_ref, g_ref, w_ref, *out_refs):
    o = dict(zip([name for name, _, _ in PROJ_OUTPUTS], out_refs))
    tm = x_ref.shape[1]
    x = x_ref[0]
    hn = (x * lax.rsqrt(jnp.mean(x * x, axis=-1, keepdims=True) + EPS) * g_ref[...]).astype(BF16)

    def mm(c0, width):
        return jnp.dot(hn, w_ref[:, c0:c0 + width], preferred_element_type=F32)

    o["qa_t"][0] = (mm(C_QA, WIDTH) * (A_DH ** -0.5 * LOG2E)).T.astype(BF16)
    o["qb_t"][0] = (mm(C_QB, WIDTH) * (B_DH ** -0.5 * LOG2E)).T.astype(BF16)
    o["qi_t"][0] = (mm(C_QI, WIDTH) * (IDX_DH ** -0.5)).T.astype(BF16)
    o["qm"][0] = (mm(C_QM, WIDTH) * (M_DH ** -0.5 * LOG2E)).astype(BF16)
    norms = []
    for c0, k16, k_t32 in ((C_KA, "ka16", "ka_t32"), (C_KB, "kb16", "kb_t32")):
        z = mm(c0, WIDTH)
        o[k16][0] = z.astype(BF16)
        o[k_t32][0] = z.T
        for h in range(WIDTH // PAIR):
            z_h = z[:, h * PAIR:(h + 1) * PAIR]
            top = jnp.max(jnp.sum(z_h * z_h, axis=1, keepdims=True), axis=0, keepdims=True)
            norms.append(jnp.broadcast_to(jnp.sqrt(top), (1, LANES)))
    o["kn"][0, 0] = jnp.concatenate(norms, axis=0)
    ones = jnp.ones((ONES_ROWS, tm), BF16)
    for c0, v_t, v_t32 in ((C_VA, "va_t", None), (C_VB, "vb_t", "vb_t32")):
        z = mm(c0, WIDTH)
        for h in range(WIDTH // PAIR):
            z_h = z[:, h * PAIR:(h + 1) * PAIR]
            o[v_t][0, h * V_ROWS:h * V_ROWS + PAIR, :] = z_h.T.astype(BF16)
            o[v_t][0, h * V_ROWS + PAIR:(h + 1) * V_ROWS, :] = ones
            if v_t32 is None:
                o["va4"][0, pl.ds(h, tm, stride=A_HEADS), :] = z_h
        if v_t32 is not None:
            o[v_t32][0] = z.T
    for c0, name in ((C_GA, "ga"), (C_GB, "gb"), (C_GM, "gm")):
        o[name][0] = mm(c0, WIDTH).astype(BF16)
    kw = mm(C_KW, LANES)
    o["kidx16"][0] = kw[:, :IDX_DH].astype(BF16)
    o["kw_t32"][0] = kw.T
    for c in range(N_BRANCH * 1024 // WIDTH):
        o["gates"][0, :, c * WIDTH:(c + 1) * WIDTH] = mm(C_GATES + c * WIDTH, WIDTH).astype(BF16)


def _proj(x, norm_g, w16, tm):
    b, s, d = x.shape
    fixed = lambda bi, i: (0, 0)
    out_shape, out_specs = [], []
    for _, dtype, (kind, n) in PROJ_OUTPUTS:
        if kind == "rows":
            shape, block, index = (b, s, n), (1, tm, n), (lambda bi, i: (bi, i, 0))
        elif kind == "lanes":
            shape, block, index = (b, n, s), (1, n, tm), (lambda bi, i: (bi, 0, i))
        elif kind == "norms":
            shape, block, index = ((b, s // tm, 2 * (WIDTH // PAIR), n),
                                   (1, 1, 2 * (WIDTH // PAIR), n), (lambda bi, i: (bi, i, 0, 0)))
        else:
            shape, block, index = ((b, s * A_HEADS, n), (1, tm * A_HEADS, n),
                                   (lambda bi, i: (bi, i, 0)))
        out_shape.append(jax.ShapeDtypeStruct(shape, dtype))
        out_specs.append(pl.BlockSpec(block, index))
    outs = pl.pallas_call(
        _proj_kernel,
        grid=(b, s // tm),
        in_specs=[pl.BlockSpec((1, tm, d), lambda bi, i: (bi, i, 0)), pl.BlockSpec((1, d), fixed),
                  _resident((d, W_COLS), fixed)],
        out_specs=out_specs,
        out_shape=out_shape,
        compiler_params=pltpu.CompilerParams(dimension_semantics=("arbitrary", "arbitrary"),
                                             vmem_limit_bytes=VMEM_LIMIT),
        name="proj",
    )(x, norm_g.reshape(1, d), w16)
    return dict(zip([name for name, _, _ in PROJ_OUTPUTS], outs))


def _memkv_kernel(x_ref, g_ref, w_ref, k32, v32, k16, v16):
    x = x_ref[...]
    hn = (x * lax.rsqrt(jnp.mean(x * x, axis=-1, keepdims=True) + EPS) * g_ref[...]).astype(BF16)
    zk = jnp.dot(hn, w_ref[:, :WIDTH], preferred_element_type=F32)
    zv = jnp.dot(hn, w_ref[:, WIDTH:], preferred_element_type=F32)
    k32[...] = zk
    v32[...] = zv
    k16[...] = zk.astype(BF16)
    v16[...] = zv.astype(BF16)


def _memkv(mem2d, g, w16, tm):
    n, d = mem2d.shape
    row = lambda i: (i, 0)
    fixed = lambda i: (0, 0)
    return pl.pallas_call(
        _memkv_kernel,
        grid=(n // tm,),
        in_specs=[pl.BlockSpec((tm, d), row), pl.BlockSpec((1, d), fixed),
                  pl.BlockSpec((d, 2 * WIDTH), fixed)],
        out_specs=[pl.BlockSpec((tm, WIDTH), row)] * 4,
        out_shape=[jax.ShapeDtypeStruct((n, WIDTH), F32)] * 2
        + [jax.ShapeDtypeStruct((n, WIDTH), BF16)] * 2,
        compiler_params=pltpu.CompilerParams(dimension_semantics=("arbitrary",)),
        name="memkv",
    )(mem2d, g.reshape(1, d), w16)


def _visible_chunks(q0, tq, tk):
    n_vis = (lax.shift_right_logical(q0 + tq - 1, CHUNK_SHIFT) + 1) * CHUNK
    return (n_vis + tk - 1) // tk


def _alibi_key_tile(tk):
    pos = np.arange(tk)
    lo = pos % POS_SPLIT
    tile = np.zeros((tk, LANES), np.float32)
    tile[:, 0:3] = lo[:, None]
    tile[:, 3:6] = (pos - lo)[:, None]
    return jnp.asarray(tile, BF16)


def _alibi_query_rows(slopes_row, width):
    row = lax.broadcasted_iota(I32, (LANES, 1), 0)
    piece = jnp.zeros((LANES, 1), F32)
    for j, p in enumerate(LOG2E_PIECES):
        piece = jnp.where(jnp.logical_or(row == j, row == j + 3), p, piece)
    return (piece * slopes_row).astype(BF16) + jnp.zeros((LANES, width), BF16)


def _split_halves(q_t, tq):
    row = lax.broadcasted_iota(I32, (PAIR, 1), 0)
    zero = jnp.zeros_like(q_t)
    return jnp.concatenate([jnp.where(row < PAIR // 2, q_t, zero),
                            jnp.where(row >= PAIR // 2, q_t, zero)], axis=1)


def _diag_distance(off, q0, tk, tq):
    kpos = off + lax.broadcasted_iota(I32, (tk, 1), 0)
    qpos = q0 + lax.broadcasted_iota(I32, (1, tq), 1)
    return jnp.maximum(kpos - qpos, 0).astype(F32), kpos, qpos


def _online_update(s, v_aug, chunk_base, m_ref, acc_ref, idx):
    m_old = m_ref[idx]
    m_new = jnp.maximum(m_old, jnp.max(s, axis=0, keepdims=True) + chunk_base)
    p = jnp.exp2(s - (m_new - chunk_base)).astype(BF16)
    pv = jnp.dot(v_aug, p, preferred_element_type=F32)
    acc_ref[idx] = jnp.exp2(m_old - m_new) * acc_ref[idx] + pv
    m_ref[idx] = m_new


def _largest_norm(q_t):
    q = q_t.astype(F32)
    return jnp.sqrt(jnp.max(jnp.sum(q * q, axis=0, keepdims=True)))


def _chunk_key_norms(k16, tk):
    b, lk, _ = k16.shape
    k = k16.astype(F32).reshape(b, lk // tk, tk, WIDTH // PAIR, PAIR)
    return jnp.swapaxes(jnp.sqrt(jnp.max(jnp.sum(k * k, axis=-1), axis=2)), 1, 2)


def _first_live_chunk(q_norm, k_norms, slope, ref_pos, tk):
    chunk = lax.broadcasted_iota(I32, k_norms.shape, 1)
    newest_key = ((chunk + 1) * tk - 1).astype(F32)
    bound = q_norm * k_norms * NORM_SLACK + 1.0 - slope * LOG2E * (ref_pos - newest_key)
    live = jnp.where(bound > -SKIP_EXP2, chunk, k_norms.shape[1])
    return jnp.min(live)


def _attend(n_groups, group_sets, n_chunks, tk, scores, values, base, frame, m_ref, acc_ref,
            p_ref):
    last = n_chunks - 1
    sub = SUB_KEYS if tk % SUB_KEYS == 0 else LANES

    def chunk_off(c):
        return pl.multiple_of(c * tk, tk)

    acc_ref[...] = jnp.zeros(acc_ref.shape, F32)

    def make_pipeline(at_set, groups, first):
        def weigh(c, diagonal, slot):
            off = chunk_off(c)
            for at, g in enumerate(groups):
                shift = frame(g) - base(g, off)
                for j in range(tk // sub):
                    s = scores(g, off, j, sub, diagonal)
                    p_ref[at_set, slot, at, j * sub:(j + 1) * sub, :] = (
                        jnp.exp2(s - shift).astype(BF16))

        def gather(c, slot):
            off = chunk_off(c)
            for at, g in enumerate(groups):
                acc_ref[g] += jnp.dot(values(g, off, tk), p_ref[at_set, slot, at],
                                      preferred_element_type=F32)

        def step(i, slot):
            gather(jnp.where(i == 0, last, first + i - 1), slot)
            weigh(first + i, False, 1 - slot)

        def two_steps(i2, carry):
            step(2 * i2, 0)
            step(2 * i2 + 1, 1)
            return carry

        def four_steps(i4, carry):
            two_steps(2 * i4, carry)
            return two_steps(2 * i4 + 1, carry)

        count = last - first

        def middle():
            lax.fori_loop(0, count // 4, four_steps, 0)
            lax.fori_loop(2 * (count // 4), count // 2, two_steps, 0)

            @pl.when(lax.rem(count, 2) == 1)
            def _():
                step(count - 1, 0)

        return (lambda: weigh(last, True, 0), middle,
                lambda: gather(jnp.where(count == 0, last, last - 1), lax.rem(count, 2)))

    stages = [make_pipeline(i, groups, first) for i, (groups, first) in enumerate(group_sets)]
    for stage in range(3):
        for pipeline in stages:
            pipeline[stage]()

    low, high, peak = None, None, None
    for g in range(n_groups):
        acc = acc_ref[g]
        sums = acc[acc.shape[0] - ONES_ROWS:acc.shape[0] - ONES_ROWS + 1]
        mag = jnp.max(jnp.abs(acc), axis=0, keepdims=True)
        low = sums if low is None else jnp.minimum(low, sums)
        high = sums if high is None else jnp.maximum(high, sums)
        peak = mag if peak is None else jnp.maximum(peak, mag)
    trusted = jnp.logical_and(jnp.min(low) >= 2.0 ** -MAX_EXP2,
                              jnp.logical_and(jnp.max(high) <= 2.0 ** MAX_EXP2,
                                              jnp.max(peak) <= 2.0 ** (2 * MAX_EXP2 - 1)))

    @pl.when(jnp.logical_not(trusted))
    def _():
        m_ref[...] = jnp.full(m_ref.shape, NEG, F32)
        acc_ref[...] = jnp.zeros(acc_ref.shape, F32)

        def chunk(c, diagonal):
            off = chunk_off(c)
            for g in range(n_groups):
                _online_update(scores(g, off, 0, tk, diagonal), values(g, off, tk),
                               base(g, off), m_ref, acc_ref, g)

        def body(c, carry):
            chunk(c, False)
            return carry

        lax.fori_loop(0, last, body, 0)
        chunk(last, True)


def _attn_a_kernel(lq1, lk1, lq2, lk2, subg_ref, e_ref, kn_ref, qt_ref, k_ref, vt_ref, o_ref,
                   m_sc, acc_sc, p_sc, *, tq, tk, q_pos0, lam_init):
    q0 = q_pos0 + pl.program_id(1) * tq
    n_chunks = _visible_chunks(q0, tq, tk)
    lam = (jnp.exp(jnp.sum(lq1[...] * lk1[...], axis=-1, keepdims=True))
           - jnp.exp(jnp.sum(lq2[...] * lk2[...], axis=-1, keepdims=True)) + lam_init)
    slopes = [_alibi_slope(h, A_HEADS) for h in range(A_HEADS)]
    q_aug, first = [], []
    for h in range(A_HEADS):
        q_t = qt_ref[0, h * PAIR:(h + 1) * PAIR, :]
        q_aug.append(jnp.concatenate([_split_halves(q_t, tq),
                                      _alibi_query_rows(slopes[h], 2 * tq)], axis=0))
        first.append(_first_live_chunk(_largest_norm(q_t), kn_ref[0, h:h + 1, :], slopes[h],
                                       q0.astype(F32), tk))
    group_sets = [((2 * i, 2 * i + 1), jnp.minimum(jnp.minimum(first[2 * i], first[2 * i + 1]),
                                                   n_chunks - 1))
                  for i in range(A_HEADS // 2)]
    ahead, kpos, qpos = _diag_distance((n_chunks - 1) * tk, q0, tk, tq)
    ahead = jnp.where(lax.shift_right_logical(kpos, CHUNK_SHIFT)
                      <= lax.shift_right_logical(qpos, CHUNK_SHIFT), ahead * (-2.0 * LOG2E), NEG)

    def scores(h, off, j, size, diagonal):
        hs = slice(h * PAIR, (h + 1) * PAIR)
        start = pl.multiple_of(off + j * size, size)
        k_aug = jnp.concatenate([k_ref[0, pl.ds(start, size), hs],
                                 e_ref[j * size:(j + 1) * size, :]], axis=1)
        s = jnp.dot(k_aug, q_aug[h], preferred_element_type=F32)
        if diagonal:
            fix = ahead[j * size:(j + 1) * size, :] * slopes[h]
            s = s + jnp.concatenate([fix, fix], axis=1)
        return s

    def values(h, start, size):
        start = pl.multiple_of(start, size)
        return vt_ref[0, h * V_ROWS:(h + 1) * V_ROWS, pl.ds(start, size)]

    def base(h, off):
        return off.astype(F32) * (slopes[h] * LOG2E)

    qpos2 = jnp.concatenate([qpos, qpos], axis=1).astype(F32)

    def frame(h):
        return qpos2 * (slopes[h] * LOG2E)

    _attend(A_HEADS, group_sets, n_chunks, tk, scores, values, base, frame, m_sc, acc_sc, p_sc)

    for h in range(A_HEADS):
        acc = acc_sc[h]
        o = acc[:PAIR] / acc[PAIR:PAIR + 1]
        o = o[:, :tq] - lam * o[:, tq:]
        o = o * lax.rsqrt(jnp.mean(o * o, axis=0, keepdims=True) + EPS) * subg_ref[...]
        o_ref[0, :, h * PAIR:(h + 1) * PAIR] = (o * (1.0 - lam_init)).T.astype(BF16)


def _attn_a(qa_t, ka16, va_t, k_norms, lam_params, subln_g, *, tq, tk, q_pos0, lam_init):
    b, _, lq = qa_t.shape
    lkp = ka16.shape[1]
    small = lambda bi, i: (0, 0)
    return pl.pallas_call(
        functools.partial(_attn_a_kernel, tq=tq, tk=tk, q_pos0=q_pos0, lam_init=lam_init),
        grid=(b, lq // tq),
        in_specs=[pl.BlockSpec((1, A_DH), small)] * 4 + [
            pl.BlockSpec((2 * A_DH, 1), small), pl.BlockSpec((tk, LANES), small),
            pl.BlockSpec((1, A_HEADS, lkp // tk), lambda bi, i: (bi, 0, 0)),
            pl.BlockSpec((1, WIDTH, tq), lambda bi, i: (bi, 0, i)),
            _resident((1, lkp, WIDTH), lambda bi, i: (bi, 0, 0)),
            _resident((1, A_HEADS * V_ROWS, lkp), lambda bi, i: (bi, 0, 0))],
        out_specs=pl.BlockSpec((1, tq, WIDTH), lambda bi, i: (bi, i, 0)),
        out_shape=jax.ShapeDtypeStruct((b, lq, WIDTH), BF16),
        scratch_shapes=[pltpu.VMEM((A_HEADS, 1, 2 * tq), F32),
                        pltpu.VMEM((A_HEADS, PAIR + ONES_ROWS, 2 * tq), F32),
                        pltpu.VMEM((2, 2, 2, tk, 2 * tq), BF16)],
        compiler_params=pltpu.CompilerParams(dimension_semantics=("arbitrary", "arbitrary"),
                                             vmem_limit_bytes=VMEM_LIMIT),
        name="attn_a",
    )(*lam_params, subln_g.reshape(2 * A_DH, 1), _alibi_key_tile(tk), k_norms, qa_t, ka16, va_t)


def _sum_keys(x):
    rows, tq = x.shape
    lanes_of_adds = 64
    if rows % lanes_of_adds == 0 and rows > lanes_of_adds:
        x = jnp.sum(x.reshape(rows // lanes_of_adds, lanes_of_adds, tq), axis=0)
    return jnp.sum(x, axis=0, keepdims=True)


def _count_true(hit):
    rows, tq = hit.shape
    if rows % 64 != 0:
        return _sum_keys(jnp.where(hit, 1.0, 0.0))
    acc = jnp.zeros((64, tq), F32)
    for i in range(rows // 64):
        acc = jnp.where(hit[i * 64:(i + 1) * 64], acc + 1.0, acc)
    return jnp.sum(acc, axis=0, keepdims=True)


def _ordered_key(x):
    bits = pltpu.bitcast(x, I32)
    return bits ^ (lax.shift_right_arithmetic(bits, 31) & INT_MAX)


def _ordered_key_inv(k):
    return pltpu.bitcast(k ^ (lax.shift_right_arithmetic(k, 31) & INT_MAX), F32)


def _attn_b_kernel(e_ref, kn_ref, qbt_ref, qit_ref, wt_ref, kidx_ref, kb_ref, vbt_ref, o_ref,
                   sc, thr_sc, m_sc, acc_sc, p_sc, *, tq, tk, q_pos0, topk, idx_bits):
    q0 = q_pos0 + pl.program_id(1) * tq
    n_chunks = _visible_chunks(q0, tq, tk)
    last = n_chunks - 1
    topk_f = float(topk)
    qpos = q0 + lax.broadcasted_iota(I32, (1, tq), 1)
    qchunk = lax.shift_right_logical(qpos, CHUNK_SHIFT)
    n_valid = ((qchunk + 1) * CHUNK).astype(F32)

    def chunk_off(c):
        return pl.multiple_of(c * tk, tk)

    def key_pos(off):
        return off + lax.broadcasted_iota(I32, (tk, 1), 0)

    def visible(off):
        return lax.shift_right_logical(key_pos(off), CHUNK_SHIFT) <= qchunk

    qi_all = jnp.concatenate([qit_ref[0, h * IDX_DH:(h + 1) * IDX_DH, :]
                              for h in range(IDX_HEADS)], axis=1)
    w_rows = [wt_ref[0, h:h + 1, :] for h in range(IDX_HEADS)]

    sub = SUB_KEYS if tk % SUB_KEYS == 0 else LANES

    def merge_stats(a, b):
        return (jnp.maximum(a[0], b[0]), jnp.minimum(a[1], b[1]), a[2] + b[2], a[3] + b[3])

    def score_chunk(c, masked):
        stats = None
        for j in range(tk // sub):
            start = pl.multiple_of(c * tk + j * sub, sub)
            logits = jnp.dot(kidx_ref[0, pl.ds(start, sub), :], qi_all,
                             preferred_element_type=F32)
            score = jnp.zeros((sub, tq), F32)
            for h in range(IDX_HEADS):
                score = score + jnp.maximum(logits[:, h * tq:(h + 1) * tq], 0.0) * w_rows[h]
            lowest = score
            if masked:
                vis = (lax.shift_right_logical(start + lax.broadcasted_iota(I32, (sub, 1), 0),
                                               CHUNK_SHIFT) <= qchunk)
                score = jnp.where(vis, score, -jnp.inf)
                lowest = jnp.where(vis, score, jnp.inf)
            sc[pl.ds(start, sub), :] = score
            part = (jnp.max(score, axis=0, keepdims=True), jnp.min(lowest, axis=0, keepdims=True),
                    _count_true(score >= 0.0), _count_true(score > 0.0))
            stats = part if stats is None else merge_stats(stats, part)
        return stats

    def score_two(i2, st):
        st = merge_stats(st, score_chunk(2 * i2, False))
        return merge_stats(st, score_chunk(2 * i2 + 1, False))

    stats = lax.fori_loop(
        0, last // 4, lambda i4, st: score_two(2 * i4 + 1, score_two(2 * i4, st)),
        (jnp.full((1, tq), -jnp.inf, F32), jnp.full((1, tq), jnp.inf, F32),
         jnp.zeros((1, tq), F32), jnp.zeros((1, tq), F32)))
    stats = lax.fori_loop(2 * (last // 4), last // 2, score_two, stats)
    stats = lax.cond(lax.rem(last, 2) == 1,
                     lambda st: merge_stats(st, score_chunk(last - 1, False)),
                     lambda st: st, stats)
    row_max, row_min, n_ge0, n_pos = merge_stats(stats, score_chunk(last, True))

    def count(pred):
        def one(c):
            off = chunk_off(c)
            return _count_true(pred(sc[pl.ds(off, tk), :], off))

        def body(i, acc):
            for u in range(SCAN_CHUNKS):
                c = SCAN_CHUNKS * i + u
                part = one(jnp.minimum(c, last))
                acc = acc + (part if u == 0 else jnp.where(c <= last, 1.0, 0.0) * part)
            return acc
        return lax.fori_loop(0, (n_chunks + SCAN_CHUNKS - 1) // SCAN_CHUNKS, body,
                             jnp.zeros((1, tq), F32))

    def count_ge(t):
        return count(lambda s, off: s >= t)

    def spread(cnt):
        c = jnp.clip(cnt, 0.5, n_valid - 0.5)
        return jnp.log2(c / (n_valid - c))

    target = spread(jnp.full((1, tq), topk_f, F32))
    key_lo, key_hi = _ordered_key(row_min), _ordered_key(row_max) + 1
    cnt_lo = n_valid
    t_a, f_a = row_min, spread(n_valid) - target
    t_b, f_b = row_max, spread(jnp.zeros((1, tq), F32)) - target
    zero_up = n_ge0 >= topk_f
    key_lo = jnp.where(zero_up, jnp.maximum(key_lo, 0), key_lo)
    cnt_lo = jnp.where(zero_up, n_ge0, cnt_lo)
    t_a = jnp.where(zero_up, 0.0, t_a)
    f_a = jnp.where(zero_up, spread(n_ge0) - target, f_a)
    zero_down = n_pos < topk_f
    key_hi = jnp.where(zero_down, jnp.minimum(key_hi, 1), key_hi)
    t_b = jnp.where(zero_down, 0.0, t_b)
    f_b = jnp.where(zero_down, spread(n_pos) - target, f_b)
    state0 = (jnp.int32(0), key_lo, key_hi, cnt_lo, t_a, f_a, t_b, f_b, jnp.zeros((1, tq), F32))

    def span(klo, khi):
        return khi - klo

    def finished(klo, khi, cnt_lo):
        d = span(klo, khi)
        tight = jnp.logical_or(d == 0, d == 1)
        return jnp.logical_or(jnp.logical_or(cnt_lo == topk_f, tight), n_valid <= topk_f)

    def search_cond(st):
        it, klo, khi, cnt_lo = st[:4]
        todo = jnp.where(finished(klo, khi, cnt_lo), 0.0, 1.0)
        return jnp.logical_and(it < MAX_SEARCH_STEPS, jnp.max(todo) > 0.0)

    def search_body(st):
        it, klo, khi, cnt_lo, ta, fa, tb, fb, side = st
        done = finished(klo, khi, cnt_lo)
        guess = ta + (tb - ta) * (fa / (fa - fb))
        guess = jnp.where(guess == guess, guess, ta)
        guess = jnp.clip(guess, -3e38, 3e38)
        mid = klo + lax.shift_right_logical(span(klo, khi), 1)
        cand = jnp.where(it % BISECT_EVERY == BISECT_EVERY - 1, mid, _ordered_key(guess))
        cand = jnp.minimum(jnp.maximum(cand, klo + 1), khi - 1)
        cand = jnp.where(done, klo, cand)
        t = _ordered_key_inv(cand)
        cnt = count_ge(t)
        up = jnp.logical_and(cnt >= topk_f, jnp.logical_not(done))
        down = jnp.logical_and(cnt < topk_f, jnp.logical_not(done))
        f = spread(cnt) - target
        fb = jnp.where(jnp.logical_and(up, side > 0.0), 0.5 * fb, fb)
        fa = jnp.where(jnp.logical_and(down, side < 0.0), 0.5 * fa, fa)
        return (it + 1, jnp.where(up, cand, klo), jnp.where(down, cand, khi),
                jnp.where(up, cnt, cnt_lo),
                jnp.where(up, t, ta), jnp.where(up, f, fa),
                jnp.where(down, t, tb), jnp.where(down, f, fb),
                jnp.where(up, 1.0, jnp.where(down, -1.0, side)))

    st = lax.while_loop(search_cond, lambda st: search_body(search_body(st)), state0)
    key_lo, cnt_lo = st[1], st[3]
    thr = _ordered_key_inv(key_lo)

    thr_sc[...] = jnp.full((1, tq), INT_MAX, I32)
    tied = jnp.where(jnp.logical_and(cnt_lo != topk_f, n_valid > topk_f), 1.0, 0.0)
    any_tied = jnp.max(tied) > 0.0

    @pl.when(any_tied)
    def _():
        room = topk_f - count(lambda s, off: s > thr)
        jmax = jnp.zeros((1, tq), I32)
        for bit in range(idx_bits - 1, -1, -1):
            cand = jmax | (1 << bit)
            g = count(lambda s, off: jnp.logical_and(s == thr, key_pos(off) <= cand))
            jmax = jnp.where(g <= room, cand, jmax)
        thr_sc[...] = jmax

    jmax = thr_sc[...]

    ahead, _, _ = _diag_distance(chunk_off(last), q0, tk, tq)

    def mask_pass(with_ties):
        def mask_chunk(c, masked):
            off = chunk_off(c)
            s = sc[pl.ds(off, tk), :]
            if with_ties:
                sel = jnp.logical_or(s > thr, jnp.logical_and(s == thr, key_pos(off) <= jmax))
            else:
                sel = s >= thr
            eff = key_pos(off).astype(F32)
            if masked:
                sel = jnp.logical_and(sel, visible(off))
                eff = eff - 2.0 * ahead
            sc[pl.ds(off, tk), :] = jnp.where(sel, 0.0, NEG)
            return jnp.max(jnp.where(sel, eff, -jnp.inf), axis=0, keepdims=True)

        near = lax.fori_loop(0, last, lambda c, n: jnp.maximum(n, mask_chunk(c, False)),
                             jnp.full((1, tq), -jnp.inf, F32))
        return jnp.maximum(near, mask_chunk(last, True))

    nearest = lax.cond(any_tied, lambda: mask_pass(True), lambda: mask_pass(False))

    lane2 = lax.broadcasted_iota(I32, (1, 2 * tq), 1)
    n_pairs = B_HEADS // 2
    slope_rows, q_aug, first = [], [], []
    nearest_min = jnp.min(nearest)
    for pi in range(n_pairs):
        slope_row = jnp.where(lane2 < tq, _alibi_slope(2 * pi, B_HEADS),
                              _alibi_slope(2 * pi + 1, B_HEADS)).astype(F32)
        slope_rows.append(slope_row)
        q_t = qbt_ref[0, pi * PAIR:(pi + 1) * PAIR, :]
        q_aug.append(jnp.concatenate([_split_halves(q_t, tq),
                                      _alibi_query_rows(slope_row, 2 * tq)], axis=0))
        first.append(_first_live_chunk(_largest_norm(q_t), kn_ref[0, pi:pi + 1, :],
                                       _alibi_slope(2 * pi + 1, B_HEADS), nearest_min, tk))
    group_sets = [((2 * i, 2 * i + 1), jnp.minimum(jnp.minimum(first[2 * i], first[2 * i + 1]),
                                                   last))
                  for i in range(n_pairs // 2)]
    ahead2 = jnp.concatenate([ahead, ahead], axis=1) * (-2.0 * LOG2E)
    nearest2 = jnp.concatenate([nearest, nearest], axis=1)

    def scores(pi, off, j, size, diagonal):
        start = pl.multiple_of(off + j * size, size)
        k_aug = jnp.concatenate([kb_ref[0, pl.ds(start, size), pi * PAIR:(pi + 1) * PAIR],
                                 e_ref[j * size:(j + 1) * size, :]], axis=1)
        mask = sc[pl.ds(start, size), :]
        s = (jnp.dot(k_aug, q_aug[pi], preferred_element_type=F32)
             + jnp.concatenate([mask, mask], axis=1))
        if diagonal:
            s = s + ahead2[j * size:(j + 1) * size, :] * slope_rows[pi]
        return s

    def frame(pi):
        return nearest2 * (slope_rows[pi] * LOG2E)

    def values(pi, start, size):
        start = pl.multiple_of(start, size)
        return vbt_ref[0, pi * V_ROWS:(pi + 1) * V_ROWS, pl.ds(start, size)]

    def base(pi, off):
        return off.astype(F32) * (slope_rows[pi] * LOG2E)

    _attend(n_pairs, group_sets, n_chunks, tk, scores, values, base, frame, m_sc, acc_sc, p_sc)

    for pi in range(n_pairs):
        acc = acc_sc[pi]
        o = acc[:PAIR] / acc[PAIR:PAIR + 1]
        o = jnp.concatenate([o[:B_DH, :tq], o[B_DH:, tq:]], axis=0)
        o_ref[0, :, pi * PAIR:(pi + 1) * PAIR] = o.T.astype(BF16)


def _attn_b(qb_t, qi_t, w_t, kidx16, kb16, vb_t, k_norms, *, tq, tk, q_pos0, topk):
    b, _, lq = qb_t.shape
    lkp = kb16.shape[1]
    q_blk = lambda rows: pl.BlockSpec((1, rows, tq), lambda bi, i: (bi, 0, i))
    per_batch = lambda bi, i: (bi, 0, 0)
    return pl.pallas_call(
        functools.partial(_attn_b_kernel, tq=tq, tk=tk, q_pos0=q_pos0, topk=topk,
                          idx_bits=max(1, (lkp - 1).bit_length())),
        grid=(b, lq // tq),
        in_specs=[pl.BlockSpec((tk, LANES), lambda bi, i: (0, 0)),
                  pl.BlockSpec((1, B_HEADS // 2, lkp // tk), per_batch),
                  q_blk(WIDTH), q_blk(WIDTH), q_blk(IDX_HEADS),
                  _resident((1, lkp, IDX_DH), per_batch),
                  _resident((1, lkp, WIDTH), per_batch),
                  _resident((1, B_HEADS // 2 * V_ROWS, lkp), per_batch)],
        out_specs=pl.BlockSpec((1, tq, WIDTH), lambda bi, i: (bi, i, 0)),
        out_shape=jax.ShapeDtypeStruct((b, lq, WIDTH), BF16),
        scratch_shapes=[pltpu.VMEM((lkp, tq), F32), pltpu.VMEM((1, tq), I32),
                        pltpu.VMEM((B_HEADS // 2, 1, 2 * tq), F32),
                        pltpu.VMEM((B_HEADS // 2, PAIR + ONES_ROWS, 2 * tq), F32),
                        pltpu.VMEM((2, 2, 2, tk, 2 * tq), BF16)],
        compiler_params=pltpu.CompilerParams(dimension_semantics=("arbitrary", "arbitrary"),
                                             vmem_limit_bytes=VMEM_LIMIT),
        name="attn_b",
    )(_alibi_key_tile(tk), k_norms, qb_t, qi_t, w_t, kidx16, kb16, vb_t)


def _nt_dot(a, b):
    return lax.dot_general(a, b, (((1,), (1,)), ((), ())), preferred_element_type=F32)


def _post_kernel(x_ref, oa_ref, ob_ref, qm_ref, ga_ref, gb_ref, gm_ref, gates_ref, mk_ref,
                 mv_ref, bg_ref, wa_ref, wb_ref, wm_ref, wo_ref, fg_ref, y_ref):
    d = x_ref.shape[-1]
    om = []
    for h in range(M_HEADS):
        hs = slice(h * M_DH, (h + 1) * M_DH)
        s = _nt_dot(qm_ref[0, :, hs], mk_ref[0, :, hs])
        p = jnp.exp2(s - jnp.max(s, axis=-1, keepdims=True))
        o = jnp.dot(p.astype(BF16), mv_ref[0, :, hs], preferred_element_type=F32)
        om.append(o / jnp.sum(p, axis=-1, keepdims=True))
    om = jnp.concatenate(om, axis=1)

    def branch(gate_ref, o, w_ref):
        g = gate_ref[0].astype(F32)
        return jnp.dot((g * jax.nn.sigmoid(g) * o).astype(BF16), w_ref[...],
                       preferred_element_type=F32)

    pa = branch(ga_ref, oa_ref[0].astype(F32), wa_ref)
    pb = branch(gb_ref, ob_ref[0].astype(F32), wb_ref)
    pm = branch(gm_ref, om, wm_ref)
    g = jax.nn.sigmoid(gates_ref[0].astype(F32) + bg_ref[...])
    merged = g[:, :d] * pa + g[:, d:2 * d] * pb + g[:, 2 * d:] * pm
    y = x_ref[0] + jnp.dot(merged.astype(BF16), wo_ref[...], preferred_element_type=F32)
    y_ref[0] = y * lax.rsqrt(jnp.mean(y * y, axis=-1, keepdims=True) + EPS) * fg_ref[...]


def _post(x, oa, ob, qm, ga, gb, gm, gates, mk16, mv16, b_gate, wa, wb, wm, wo, final_g, tm):
    b, lq, d = x.shape
    n_mem = mk16.shape[1]
    blk = lambda w: pl.BlockSpec((1, tm, w), lambda bi, i: (bi, i, 0))
    per_batch = pl.BlockSpec((1, n_mem, WIDTH), lambda bi, i: (bi, 0, 0))
    full = lambda r, c: pl.BlockSpec((r, c), lambda bi, i: (0, 0))
    return pl.pallas_call(
        _post_kernel,
        grid=(b, lq // tm),
        in_specs=[blk(d)] + [blk(WIDTH)] * 6 + [blk(N_BRANCH * d), per_batch, per_batch,
                  full(1, N_BRANCH * d), full(WIDTH, d), full(WIDTH, d), full(WIDTH, d),
                  full(d, d), full(1, d)],
        out_specs=blk(d),
        out_shape=jax.ShapeDtypeStruct((b, lq, d), F32),
        compiler_params=pltpu.CompilerParams(dimension_semantics=("arbitrary", "arbitrary"),
                                             vmem_limit_bytes=VMEM_LIMIT),
        name="post",
    )(x, oa, ob, qm, ga, gb, gm, gates, mk16, mv16, b_gate.reshape(1, -1), wa, wb, wm, wo,
      final_g.reshape(1, d))


def _values_on_lanes(v16):
    b, lk, _ = v16.shape
    v_t = jnp.swapaxes(v16, 1, 2).reshape(b, WIDTH // PAIR, PAIR, lk)
    ones = jnp.ones((b, WIDTH // PAIR, ONES_ROWS, lk), v16.dtype)
    return jnp.concatenate([v_t, ones], axis=2).reshape(b, WIDTH // PAIR * V_ROWS, lk)


def _pad_axis(x, size, axis):
    pad = size - x.shape[axis]
    if pad == 0:
        return x
    widths = [(0, 0)] * x.ndim
    widths[axis] = (0, pad)
    return jnp.pad(x, widths)


def _layer(x, past, mk16, mv16, q_pos0, norm_g, w16, b_gate, lam_params, lam_init, subln_g,
           wa, wb, wm, wo, final_g, *, tm, tq_a, tq, tk):
    b, lq, d = x.shape
    p = _proj(x, norm_g, w16, tm)
    new_rows = (
        jnp.transpose(p["ka_t32"].reshape(b, A_HEADS, 2, A_DH, lq), (0, 4, 1, 2, 3)),
        p["va4"].reshape(b, lq, A_HEADS, 2 * A_DH),
        jnp.transpose(p["kb_t32"].reshape(b, B_HEADS, B_DH, lq), (0, 3, 1, 2)),
        jnp.transpose(p["vb_t32"].reshape(b, B_HEADS, B_DH, lq), (0, 3, 1, 2)),
        jnp.swapaxes(p["kw_t32"][:, :IDX_DH], 1, 2))
    p["w_t"] = p["kw_t32"][:, IDX_DH:IDX_DH + IDX_HEADS]
    ka16, kb16, kidx16, va_t, vb_t = p["ka16"], p["kb16"], p["kidx16"], p["va_t"], p["vb_t"]
    if past is not None:
        pa_k, pa_v, pb_k, pb_v, p_kidx = past
        n_past = pa_k.shape[1]
        flat16 = lambda t: t.reshape(b, n_past, -1).astype(BF16)
        rows = lambda old, new: jnp.concatenate([flat16(old), new], axis=1)
        lanes = lambda old, new: jnp.concatenate([_values_on_lanes(flat16(old)), new], axis=2)
        ka16, kb16, kidx16 = rows(pa_k, ka16), rows(pb_k, kb16), rows(p_kidx, kidx16)
        va_t, vb_t = lanes(pa_v, va_t), lanes(pb_v, vb_t)
    lk = ka16.shape[1]
    topk = min(TOPK_MAX, lk // 4)
    lkp = -(-lk // tk) * tk
    ka16, kb16, kidx16 = [_pad_axis(t, lkp, 1) for t in (ka16, kb16, kidx16)]
    va_t, vb_t = _pad_axis(va_t, lkp, 2), _pad_axis(vb_t, lkp, 2)
    lqp = -(-lq // max(tq_a, tq)) * max(tq_a, tq)
    qa_t, qb_t, qi_t, w_t = [_pad_axis(p[n], lqp, 2) for n in ("qa_t", "qb_t", "qi_t", "w_t")]
    if past is None and lkp == lq and tk % tm == 0:
        kn = jnp.max(p["kn"][..., 0].reshape(b, lkp // tk, tk // tm, 2 * (WIDTH // PAIR)), axis=2)
        kn = jnp.swapaxes(kn, 1, 2)
        kn_a, kn_b = kn[:, :WIDTH // PAIR], kn[:, WIDTH // PAIR:]
    else:
        kn_a, kn_b = _chunk_key_norms(ka16, tk), _chunk_key_norms(kb16, tk)

    oa = _attn_a(qa_t, ka16, va_t, kn_a, lam_params, subln_g, tq=tq_a, tk=tk, q_pos0=q_pos0,
                 lam_init=lam_init)[:, :lq]
    ob = _attn_b(qb_t, qi_t, w_t, kidx16, kb16, vb_t, kn_b, tq=tq, tk=tk, q_pos0=q_pos0,
                 topk=topk)[:, :lq]
    y = _post(x, oa, ob, p["qm"], p["ga"], p["gb"], p["gm"], p["gates"], mk16, mv16, b_gate,
              wa, wb, wm, wo, final_g, tm)
    return y, new_rows


def kernel(x_prompt, x_sample, mem_prompt, cache_a_k, cache_a_v, cache_b_k, cache_b_v, cache_b_kidx, cache_mem_k, cache_mem_v, norm_g, w_in, b_gate, lam_q1, lam_k1, lam_q2, lam_k2, subln_g, mem_norm_g, w_mem_kv, w_br_a, w_br_b, w_br_m, w_out, final_g):
    depth, d, _ = w_in.shape
    assert depth == 1, "single-layer step only"
    l = 0
    lam_init = 0.8 - 0.6 * math.exp(-0.3 * l)
    w16 = jnp.concatenate([w_in[l][:, :IN_KW_END], jnp.zeros((d, KW_PAD), w_in.dtype),
                           w_in[l][:, IN_KW_END:]], axis=1).astype(BF16)
    lam_params = [p[l].reshape(1, A_DH) for p in (lam_q1, lam_k1, lam_q2, lam_k2)]
    shared = (norm_g[l], w16, b_gate[l], lam_params, lam_init, subln_g[l],
              w_br_a[l].astype(BF16), w_br_b[l].astype(BF16), w_br_m[l].astype(BF16),
              w_out[l].astype(BF16), final_g)

    bp, n_mem, _ = mem_prompt.shape
    mk32, mv32, mk16, mv16 = _memkv(mem_prompt.reshape(bp * n_mem, d), mem_norm_g[l],
                                    w_mem_kv[l].astype(BF16), tm=256)
    y_p, rows_p = _layer(x_prompt, None, mk16.reshape(bp, n_mem, WIDTH),
                         mv16.reshape(bp, n_mem, WIDTH), 0, *shared, tm=256, tq_a=256, tq=128,
                         tk=1024)

    bs, n_past = cache_a_k.shape[1], cache_a_k.shape[2]
    past = (cache_a_k[l], cache_a_v[l], cache_b_k[l], cache_b_v[l], cache_b_kidx[l])
    ls = x_sample.shape[1]
    y_s, rows_s = _layer(x_sample, past, cache_mem_k[l].reshape(bs, n_mem, WIDTH).astype(BF16),
                         cache_mem_v[l].reshape(bs, n_mem, WIDTH).astype(BF16), n_past, *shared,
                         tm=ls, tq_a=128, tq=128, tk=384)

    mem_shape = (1, bp, n_mem, M_HEADS, M_DH)
    return (y_p, y_s, *[r[None] for r in rows_p], mk32.reshape(mem_shape),
            mv32.reshape(mem_shape), *[r[None] for r in rows_s])
```
